```python
import math
import jax, jax.numpy as jnp
from jax import lax
import numpy as np

D_MODEL = 1024
BATCH = 2
SEQ = 8192
DEPTH = 4

N_EVEN = (DEPTH + 1) // 2
N_ODD = DEPTH // 2

A_CHUNK = 128
A_GROUPS = 4
A_WIDTH = D_MODEL // 2
A_GROUP_DIM = A_WIDTH // A_GROUPS

B_HEADS = 8
B_NOPE = 64
B_ROPE = 32
B_VDIM = 64
B_Q_RANK = 384
B_KV_RANK = 256
B_BLOCK = 128
B_WIDTH = B_HEADS * B_VDIM
ROPE_THETA = 10000.0

EVEN_IN = 2 * A_WIDTH + B_Q_RANK + B_KV_RANK + B_ROPE

C_HEADS = 4
C_DK = D_MODEL // 2 // C_HEADS
C_DV = D_MODEL // C_HEADS
C_GATE_RANK = 16
C_GATE_TAU = 16.0
C_CHUNK = 64
ODD_IN = 2 * C_HEADS * C_DK + 2 * C_HEADS * C_DV + C_GATE_RANK

D_FF = 4 * D_MODEL

ALPHA = (2.0 * DEPTH) ** 0.25
BETA = (8.0 * DEPTH) ** -0.25
LN_EPS = 1e-5

kernel_name = "hybrid_gmlp_mla_gla_deepnorm"


def layer_norm(x, g, b):
    xf = x.astype(jnp.float32)
    mu = xf.mean(-1, keepdims=True)
    var = jnp.square(xf - mu).mean(-1, keepdims=True)
    return ((xf - mu) * lax.rsqrt(var + LN_EPS) * g + b).astype(x.dtype)


def rms_norm(x, g):
    xf = x.astype(jnp.float32)
    ms = jnp.square(xf).mean(-1, keepdims=True)
    return (xf * lax.rsqrt(ms + LN_EPS) * g).astype(x.dtype)


def rope(x, positions):
    half = x.shape[-1] // 2
    inv = ROPE_THETA ** (-jnp.arange(half, dtype=jnp.float32) / half)
    ang = positions.astype(jnp.float32)[..., None] * inv
    ang = ang.reshape(ang.shape[:2] + (1,) * (x.ndim - 3) + (half,))
    cos, sin = jnp.cos(ang), jnp.sin(ang)
    xf = x.astype(jnp.float32)
    x1, x2 = xf[..., :half], xf[..., half:]
    return jnp.concatenate([x1 * cos - x2 * sin, x2 * cos + x1 * sin], -1).astype(x.dtype)


def chunk_gmlp(u, v, w_s, b_s, g_v, bias_v):
    bn, s, _ = u.shape
    nc = s // A_CHUNK
    v = v.reshape(bn, nc, A_CHUNK, A_GROUPS, A_GROUP_DIM)
    v = layer_norm(v, g_v, bias_v)
    causal = jnp.tril(jnp.ones((A_CHUNK, A_CHUNK), dtype=bool))
    w = jnp.where(causal[None], w_s, 0.0)
    mixed = jnp.einsum('gts,bcsgd->bctgd', w, v) + b_s.T[None, None, :, :, None]
    return (u.reshape(mixed.shape) * mixed).reshape(bn, s, A_WIDTH)


def mla(c_q, c_kv, k_r, positions, g_q, g_kv, w_uq, w_ukv):
    bn, s, _ = c_q.shape
    q = (rms_norm(c_q, g_q) @ w_uq).reshape(bn, s, B_HEADS, B_NOPE + B_ROPE)
    kv = (rms_norm(c_kv, g_kv) @ w_ukv).reshape(bn, s, B_HEADS, B_NOPE + B_VDIM)
    q = jnp.concatenate([q[..., :B_NOPE], rope(q[..., B_NOPE:], positions)], -1)
    k_r = rope(k_r, positions)
    k = jnp.concatenate([kv[..., :B_NOPE],
                         jnp.broadcast_to(k_r[:, :, None, :], (bn, s, B_HEADS, B_ROPE))], -1)
    v = kv[..., B_NOPE:]
    scale = (B_NOPE + B_ROPE) ** -0.5
    nq = s // B_BLOCK
    qb = q.reshape(bn, nq, B_BLOCK, B_HEADS, B_NOPE + B_ROPE).transpose(1, 0, 2, 3, 4)
    kpos = jnp.arange(s)

    def block(args):
        qi, i = args
        sc = jnp.einsum('bqhd,bkhd->bhqk', qi, k, preferred_element_type=jnp.float32) * scale
        qpos = i * B_BLOCK + jnp.arange(B_BLOCK)
        sc = jnp.where(kpos[None, :] <= qpos[:, None], sc, -jnp.inf)
        p = jax.nn.softmax(sc, axis=-1)
        return jnp.einsum('bhqk,bkhd->bqhd', p.astype(v.dtype), v)

    o = lax.map(block, (qb, jnp.arange(nq)))
    return o.transpose(1, 0, 2, 3, 4).reshape(bn, s, B_WIDTH)


def gla(q, k, v, log_a):
    bn, s = q.shape[:2]
    nc = s // C_CHUNK

    def to_chunks(t):
        return t.reshape(bn, nc, C_CHUNK, C_HEADS, -1).transpose(1, 0, 3, 2, 4).astype(jnp.float32)

    qc, kc, vc, gc = to_chunks(q * C_DK ** -0.5), to_chunks(k), to_chunks(v), to_chunks(log_a)
    causal = jnp.tril(jnp.ones((C_CHUNK, C_CHUNK), dtype=bool))

    def step(state, xs):
        qi, ki, vi, gi = xs
        b = jnp.cumsum(gi, axis=2)
        o_inter = jnp.einsum('bhld,bhde->bhle', qi * jnp.exp(b), state)
        diff = b[:, :, :, None, :] - b[:, :, None, :, :]
        decay = jnp.exp(jnp.where(causal[:, :, None], diff, -jnp.inf))
        attn = jnp.einsum('bhtsd,bhsd->bhts', qi[:, :, :, None, :] * decay, ki)
        o = o_inter + jnp.einsum('bhts,bhse->bhte', attn, vi)
        b_last = b[:, :, -1:, :]
        state = (jnp.exp(b_last[:, :, 0, :])[..., None] * state
                 + jnp.einsum('bhsd,bhse->bhde', ki * jnp.exp(b_last - b), vi))
        return state, o

    state0 = jnp.zeros((bn, C_HEADS, C_DK, C_DV), jnp.float32)
    _, o = lax.scan(step, state0, (qc, kc, vc, gc))
    return o.transpose(1, 0, 3, 2, 4).reshape(bn, s, C_HEADS, C_DV)


def sqrelu_mlp(x, w1, w2):
    return jnp.square(jax.nn.relu(x @ w1)) @ w2


def setup_inputs(seed: int = 0) -> dict:
    key = jax.random.key(seed)
    ks = jax.random.split(key, 26)

    def nrm(k, shape, scale):
        return jax.random.normal(k, shape, jnp.float32) * scale

    def gain(k, shape):
        return 1.0 + nrm(k, shape, 0.02)

    d = D_MODEL
    return {
        "x": nrm(ks[0], (BATCH, SEQ, d), 1.0),
        "positions": jnp.broadcast_to(jnp.arange(SEQ, dtype=jnp.int32), (BATCH, SEQ)),
        "ln1_g": gain(ks[1], (DEPTH, d)),
        "ln1_b": nrm(ks[2], (DEPTH, d), 0.02),
        "ln2_g": gain(ks[3], (DEPTH, d)),
        "ln2_b": nrm(ks[4], (DEPTH, d), 0.02),
        "w_in_even": nrm(ks[5], (N_EVEN, d, EVEN_IN), d ** -0.5),
        "a_w_s": nrm(ks[6], (N_EVEN, A_GROUPS, A_CHUNK, A_CHUNK), A_CHUNK ** -0.5),
        "a_b_s": 1.0 + nrm(ks[7], (N_EVEN, A_GROUPS, A_CHUNK), 0.1),
        "a_ln_g": gain(ks[8], (N_EVEN, A_GROUPS, A_GROUP_DIM)),
        "a_ln_b": nrm(ks[9], (N_EVEN, A_GROUPS, A_GROUP_DIM), 0.02),
        "b_q_norm": gain(ks[10], (N_EVEN, B_Q_RANK)),
        "b_kv_norm": gain(ks[11], (N_EVEN, B_KV_RANK)),
        "b_w_uq": nrm(ks[12], (N_EVEN, B_Q_RANK, B_HEADS * (B_NOPE + B_ROPE)), B_Q_RANK ** -0.5),
        "b_w_ukv": nrm(ks[13], (N_EVEN, B_KV_RANK, B_HEADS * (B_NOPE + B_VDIM)), B_KV_RANK ** -0.5),
        "w_out_even": nrm(ks[14], (N_EVEN, A_WIDTH + B_WIDTH, d), BETA * (A_WIDTH + B_WIDTH) ** -0.5),
        "w_in_odd": nrm(ks[15], (N_ODD, d, ODD_IN), d ** -0.5),
        "c_w_gate": nrm(ks[16], (N_ODD, C_GATE_RANK, C_HEADS * C_DK), C_GATE_RANK ** -0.5),
        "c_b_gate": nrm(ks[17], (N_ODD, C_HEADS * C_DK), 0.1),
        "c_ln_g": gain(ks[18], (N_ODD, C_DV)),
        "c_ln_b": nrm(ks[19], (N_ODD, C_DV), 0.02),
        "w_out_odd": nrm(ks[20], (N_ODD, C_HEADS * C_DV, d), BETA * (C_HEADS * C_DV) ** -0.5),
        "w_ff1": nrm(ks[21], (DEPTH, d, D_FF), d ** -0.5),
        "w_ff2": nrm(ks[22], (DEPTH, D_FF, d), BETA * D_FF ** -0.5),
    }


def reference(x, positions, ln1_g, ln1_b, ln2_g, ln2_b,
              w_in_even, a_w_s, a_b_s, a_ln_g, a_ln_b,
              b_q_norm, b_kv_norm, b_w_uq, b_w_ukv, w_out_even,
              w_in_odd, c_w_gate, c_b_gate, c_ln_g, c_ln_b, w_out_odd,
              w_ff1, w_ff2):
    bn, s, _ = x.shape
    for layer in range(DEPTH):
        j = layer // 2
        if layer % 2 == 0:
            z = x @ w_in_even[j]
            o1 = A_WIDTH
            o2 = o1 + A_WIDTH
            o3 = o2 + B_Q_RANK
            o4 = o3 + B_KV_RANK
            u = jax.nn.gelu(z[..., :o1])
            vv = jax.nn.gelu(z[..., o1:o2])
            y_a = chunk_gmlp(u, vv, a_w_s[j], a_b_s[j], a_ln_g[j], a_ln_b[j])
            y_b = mla(z[..., o2:o3], z[..., o3:o4], z[..., o4:], positions,
                      b_q_norm[j], b_kv_norm[j], b_w_uq[j], b_w_ukv[j])
            y = jnp.concatenate([y_a, y_b], -1) @ w_out_even[j]
        else:
            z = x @ w_in_odd[j]
            hk, hv = C_HEADS * C_DK, C_HEADS * C_DV
            q = z[..., :hk].reshape(bn, s, C_HEADS, C_DK)
            k = z[..., hk:2 * hk].reshape(bn, s, C_HEADS, C_DK)
            v = z[..., 2 * hk:2 * hk + hv].reshape(bn, s, C_HEADS, C_DV)
            g = z[..., 2 * hk + hv:2 * hk + 2 * hv]
            zg = z[..., 2 * hk + 2 * hv:]
            logits = (zg @ c_w_gate[j] + c_b_gate[j]).astype(jnp.float32)
            log_a = (jax.nn.log_sigmoid(logits) / C_GATE_TAU).reshape(bn, s, C_HEADS, C_DK)
            o = gla(q, k, v, log_a)
            o = layer_norm(o, c_ln_g[j], c_ln_b[j]).astype(x.dtype).reshape(bn, s, hv)
            y = (o * jax.nn.silu(g)) @ w_out_odd[j]
        x = layer_norm(ALPHA * x + y, ln1_g[layer], ln1_b[layer])
        x = layer_norm(ALPHA * x + sqrelu_mlp(x, w_ff1[layer], w_ff2[layer]), ln2_g[layer], ln2_b[layer])
    return x
```

```python
import functools

import jax
import jax.numpy as jnp
from jax import lax
from jax.experimental import pallas as pl
from jax.experimental.pallas import tpu as pltpu

F32 = jnp.float32
BF16 = jnp.bfloat16

D_MODEL = 1024
DEPTH = 4

A_CHUNK = 128
A_GROUPS = 4
A_WIDTH = 512
A_GROUP_DIM = 128

B_HEADS = 8
B_NOPE = 64
B_ROPE = 32
B_VDIM = 64
B_Q_RANK = 384
B_KV_RANK = 256
ROPE_THETA = 10000.0
HEAD_PAD = 128

C_HEADS = 4
C_DK = 128
C_DV = 256
C_GATE_RANK = 16
C_GATE_TAU = 16.0
GLA_CHUNK = 64
GLA_SUB = 16

D_FF = 4 * D_MODEL
ALPHA = (2.0 * DEPTH) ** 0.25
LN_EPS = 1e-5

VMEM_LIMIT = 56 * 1024 * 1024


def _dot(a, b):
    return jnp.dot(a, b, preferred_element_type=F32)


def _dot_nt(a, b):
    return lax.dot_general(a, b, (((1,), (1,)), ((), ())), preferred_element_type=F32)


def _dot_tn(a, b):
    return lax.dot_general(a, b, (((0,), (0,)), ((), ())), preferred_element_type=F32)


def _ln(x, g, b):
    mu = jnp.mean(x, axis=-1, keepdims=True)
    xc = x - mu
    var = jnp.mean(xc * xc, axis=-1, keepdims=True)
    return xc * lax.rsqrt(var + LN_EPS) * g + b


def _rms(x, g):
    ms = jnp.mean(x * x, axis=-1, keepdims=True)
    return x * lax.rsqrt(ms + LN_EPS) * g


def _gelu(x):
    return 0.5 * x * (1.0 + jnp.tanh(0.7978845608028654 * (x + 0.044715 * (x * x * x))))


def _even_pre_kernel(x_ref, pos_ref, inv_ref, wuv_ref, wcq_ref, wckv_ref, wkr_ref,
                     ws_ref, bs_ref, alng_ref, alnb_ref, gq_ref, gkv_ref,
                     wuq_ref, wuk_ref, wuvv_ref,
                     ya_ref, q_ref, k_ref, v_ref):
    tm = x_ref.shape[0]
    xb = x_ref[...].astype(BF16)

    zuv = _dot(xb, wuv_ref[...])
    for g in range(A_GROUPS):
        lo = g * A_GROUP_DIM
        u = _gelu(zuv[:, lo:lo + A_GROUP_DIM])
        vv = _gelu(zuv[:, A_WIDTH + lo:A_WIDTH + lo + A_GROUP_DIM])
        vn = _ln(vv, alng_ref[g:g + 1, :], alnb_ref[g:g + 1, :]).astype(BF16)
        for c in range(tm // A_CHUNK):
            r = c * A_CHUNK
            mixed = _dot(ws_ref[g], vn[r:r + A_CHUNK, :]) + bs_ref[g]
            ya_ref[r:r + A_CHUNK, lo:lo + A_GROUP_DIM] = (u[r:r + A_CHUNK, :] * mixed).astype(ya_ref.dtype)

    ang = pos_ref[...].astype(F32) * inv_ref[...]
    cos = jnp.cos(ang)
    sin = jnp.sin(ang)
    lane = lax.broadcasted_iota(jnp.int32, (1, HEAD_PAD), 1)
    half = B_ROPE // 2
    s1 = jnp.where((lane >= B_NOPE) & (lane < B_NOPE + half), -sin, 0.0)
    s2 = jnp.where((lane >= B_NOPE + half) & (lane < B_NOPE + B_ROPE), sin, 0.0)

    def rope(t):
        return t * cos + pltpu.roll(t, HEAD_PAD - half, 1) * s1 + pltpu.roll(t, half, 1) * s2

    scale = (B_NOPE + B_ROPE) ** -0.5
    cqn = _rms(_dot(xb, wcq_ref[...]), gq_ref[...]).astype(BF16)
    qf = _dot(cqn, wuq_ref[...])
    ckvn = _rms(_dot(xb, wckv_ref[...]), gkv_ref[...]).astype(BF16)
    kn = _dot(ckvn, wuk_ref[...])
    kr = rope(_dot(xb, wkr_ref[...]))
    for h in range(B_HEADS):
        lo = h * HEAD_PAD
        q_ref[:, lo:lo + HEAD_PAD] = (rope(qf[:, lo:lo + HEAD_PAD]) * scale).astype(q_ref.dtype)
        k_ref[:, lo:lo + HEAD_PAD] = (kn[:, lo:lo + HEAD_PAD] + kr).astype(k_ref.dtype)
    v_ref[...] = _dot(ckvn, wuvv_ref[...]).astype(v_ref.dtype)


def _even_pre(x, pos, inv_lane, wuv, wcq, wckv, wkr, ws, bs, alng, alnb, gq, gkv, wuq, wuk, wuvv, *, tm):
    n = x.shape[0]
    row = lambda w: pl.BlockSpec((tm, w), lambda i: (i, 0))
    full2 = lambda a: pl.BlockSpec(a.shape, lambda i: (0, 0))
    full3 = lambda a: pl.BlockSpec(a.shape, lambda i: (0, 0, 0))
    hp = B_HEADS * HEAD_PAD
    return pl.pallas_call(
        _even_pre_kernel,
        grid=(n // tm,),
        in_specs=[row(D_MODEL), row(1), full2(inv_lane), full2(wuv), full2(wcq), full2(wckv), full2(wkr),
                  full3(ws), full3(bs), full2(alng), full2(alnb), full2(gq), full2(gkv),
                  full2(wuq), full2(wuk), full2(wuvv)],
        out_specs=[row(A_WIDTH), row(hp), row(hp), row(B_HEADS * B_VDIM)],
        out_shape=[jax.ShapeDtypeStruct((n, A_WIDTH), BF16),
                   jax.ShapeDtypeStruct((n, hp), BF16),
                   jax.ShapeDtypeStruct((n, hp), BF16),
                   jax.ShapeDtypeStruct((n, B_HEADS * B_VDIM), BF16)],
        compiler_params=pltpu.CompilerParams(dimension_semantics=("parallel",),
                                             vmem_limit_bytes=VMEM_LIMIT),
        name="even_pre",
    )(x, pos, inv_lane, wuv, wcq, wckv, wkr, ws, bs, alng, alnb, gq, gkv, wuq, wuk, wuvv)


def _attn_kernel(q_ref, k_ref, v_ref, o_ref):
    tq = q_ref.shape[0]
    qi = pl.program_id(2)
    row_ids = lax.broadcasted_iota(jnp.int32, (tq, tq), 0)
    col_ids = lax.broadcasted_iota(jnp.int32, (tq, tq), 1)
    outs = []
    for hh in range(2):
        lo = hh * HEAD_PAD
        q = q_ref[:, lo:lo + HEAD_PAD]

        def step(j, carry, masked):
            m, l, acc = carry
            start = pl.multiple_of(j * tq, tq)
            kb = k_ref[pl.ds(start, tq), lo:lo + HEAD_PAD]
            vb = v_ref[pl.ds(start, tq), :]
            s = _dot_nt(q, kb)
            if masked:
                s = jnp.where(col_ids <= row_ids, s, -jnp.inf)
            m_new = jnp.maximum(m, jnp.max(s, axis=-1, keepdims=True))
            p = jnp.exp(s - m_new)
            a = jnp.exp(m - m_new)
            l = a * l + jnp.sum(p, axis=-1, keepdims=True)
            acc = a * acc + _dot(p.astype(BF16), vb)
            return m_new, l, acc

        init = (jnp.full((tq, 1), -jnp.inf, F32), jnp.zeros((tq, 1), F32),
                jnp.zeros((tq, 2 * B_VDIM), F32))
        carry = lax.fori_loop(0, qi, functools.partial(step, masked=False), init)
        m, l, acc = step(qi, carry, True)
        outs.append(acc / l)
    lane = lax.broadcasted_iota(jnp.int32, (1, 2 * B_VDIM), 1)
    o_ref[...] = jnp.where(lane < B_VDIM, outs[0], outs[1]).astype(o_ref.dtype)


def _attn(q, k, v, *, batch, seq, tq):
    n = q.shape[0]
    nq = seq // tq
    return pl.pallas_call(
        _attn_kernel,
        grid=(batch, B_HEADS // 2, nq),
        in_specs=[pl.BlockSpec((tq, 2 * HEAD_PAD), lambda b, h, i: (b * nq + i, h)),
                  pl.BlockSpec((seq, 2 * HEAD_PAD), lambda b, h, i: (b, h)),
                  pl.BlockSpec((seq, 2 * B_VDIM), lambda b, h, i: (b, h))],
        out_specs=pl.BlockSpec((tq, 2 * B_VDIM), lambda b, h, i: (b * nq + i, h)),
        out_shape=jax.ShapeDtypeStruct((n, B_HEADS * B_VDIM), BF16),
        compiler_params=pltpu.CompilerParams(dimension_semantics=("parallel", "parallel", "arbitrary"),
                                             vmem_limit_bytes=VMEM_LIMIT),
        name="attn",
    )(q, k, v)


def _odd_pre_kernel(x_ref, wq_ref, wk_ref, wv_ref, wg_ref, wzg_ref, wgate_ref, bgate_ref,
                    q_ref, k_ref, v_ref, sg_ref, la_ref):
    xb = x_ref[...].astype(BF16)
    q_ref[...] = _dot(xb, wq_ref[...]) * (C_DK ** -0.5)
    k_ref[...] = _dot(xb, wk_ref[...])
    v_ref[...] = _dot(xb, wv_ref[...]).astype(v_ref.dtype)
    g = _dot(xb, wg_ref[...])
    sg_ref[...] = g * jax.nn.sigmoid(g)
    zg = _dot(xb, wzg_ref[...]).astype(BF16)
    logits = _dot(zg, wgate_ref[...]) + bgate_ref[...]
    la_ref[...] = -(jnp.maximum(-logits, 0.0) + jnp.log1p(jnp.exp(-jnp.abs(logits)))) * (1.0 / C_GATE_TAU)


def _odd_pre(x, wq, wk, wv, wg, wzg, wgate, bgate, *, tm):
    n = x.shape[0]
    row = lambda w: pl.BlockSpec((tm, w), lambda i: (i, 0))
    full2 = lambda a: pl.BlockSpec(a.shape, lambda i: (0, 0))
    hk, hv = C_HEADS * C_DK, C_HEADS * C_DV
    return pl.pallas_call(
        _odd_pre_kernel,
        grid=(n // tm,),
        in_specs=[row(D_MODEL), full2(wq), full2(wk), full2(wv), full2(wg), full2(wzg), full2(wgate), full2(bgate)],
        out_specs=[row(hk), row(hk), row(hv), row(hv), row(hk)],
        out_shape=[jax.ShapeDtypeStruct((n, hk), F32),
                   jax.ShapeDtypeStruct((n, hk), F32),
                   jax.ShapeDtypeStruct((n, hv), BF16),
                   jax.ShapeDtypeStruct((n, hv), F32),
                   jax.ShapeDtypeStruct((n, hk), F32)],
        compiler_params=pltpu.CompilerParams(dimension_semantics=("parallel",),
                                             vmem_limit_bytes=VMEM_LIMIT),
        name="odd_pre",
    )(x, wq, wk, wv, wg, wzg, wgate, bgate)


def _gla_kernel(q_ref, k_ref, v_ref, la_ref, sg_ref, lng_ref, lnb_ref, y_ref, st_ref, b_scr):
    tc = q_ref.shape[0]
    C, c = GLA_CHUNK, GLA_SUB

    @pl.when(pl.program_id(2) == 0)
    def _():
        st_ref[...] = jnp.zeros_like(st_ref)

    tri = (lax.broadcasted_iota(jnp.int32, (C, C), 1) <= lax.broadcasted_iota(jnp.int32, (C, C), 0)).astype(F32)
    lane_c = lax.broadcasted_iota(jnp.int32, (1, C), 1)
    t_ids = lax.broadcasted_iota(jnp.int32, (c, 1), 0)

    def chunk(ci, _):
        r0 = pl.multiple_of(ci * C, C)
        b = jnp.dot(tri, la_ref[pl.ds(r0, C), :], preferred_element_type=F32,
                    precision=lax.Precision.HIGHEST)
        b_scr[...] = b
        q = q_ref[pl.ds(r0, C), :]
        k = k_ref[pl.ds(r0, C), :]
        v = v_ref[pl.ds(r0, C), :]
        st = st_ref[...]
        o = _dot_nt((q * jnp.exp(b)).astype(BF16), st.astype(BF16))
        b_last = b[C - 1:C, :]

        rows = []
        for i in range(C // c):
            lo = i * c
            bi = b[lo:lo + c, :]
            qi = q[lo:lo + c, :]
            ref_b = b[lo - 1:lo, :] if i > 0 else jnp.zeros((1, C_DK), F32)
            qt = (qi * jnp.exp(bi - ref_b)).astype(BF16)
            kt = (k * jnp.exp(jnp.minimum(ref_b - b, 0.0))).astype(BF16)
            a_blk = jnp.where(lane_c < lo, _dot_nt(qt, kt), 0.0)

            def diag(s, a_blk, lo=lo, bi=bi, qi=qi):
                bs = b_scr[pl.ds(lo + s, 1), :]
                ks = k_ref[pl.ds(r0 + lo + s, 1), :]
                e = jnp.exp(jnp.minimum(bi - bs, 0.0))
                col = jnp.sum(qi * e * ks, axis=-1, keepdims=True)
                return a_blk + jnp.where((lane_c == lo + s) & (t_ids >= s), col, 0.0)

            rows.append(lax.fori_loop(0, c, diag, a_blk))
        attn = jnp.concatenate(rows, axis=0)
        o = o + _dot(attn.astype(BF16), v)

        kd = (k * jnp.exp(b_last - b)).astype(BF16)
        st_ref[...] = st * jnp.exp(b_last) + _dot_tn(v, kd)

        on = _ln(o, lng_ref[...], lnb_ref[...])
        y_ref[pl.ds(r0, C), :] = (on * sg_ref[pl.ds(r0, C), :]).astype(y_ref.dtype)
        return 0

    lax.fori_loop(0, tc // C, chunk, 0)


def _gla(q, k, v, la, sg, lng, lnb, *, batch, seq, tc):
    n = q.shape[0]
    nt = seq // tc
    kblk = pl.BlockSpec((tc, C_DK), lambda b, h, i: (b * nt + i, h))
    vblk = pl.BlockSpec((tc, C_DV), lambda b, h, i: (b * nt + i, h))
    vec = pl.BlockSpec((1, C_DV), lambda b, h, i: (0, 0))
    return pl.pallas_call(
        _gla_kernel,
        grid=(batch, C_HEADS, nt),
        in_specs=[kblk, kblk, vblk, kblk, vblk, vec, vec],
        out_specs=vblk,
        out_shape=jax.ShapeDtypeStruct((n, C_HEADS * C_DV), BF16),
        scratch_shapes=[pltpu.VMEM((C_DV, C_DK), F32), pltpu.VMEM((GLA_CHUNK, C_DK), F32)],
        compiler_params=pltpu.CompilerParams(dimension_semantics=("parallel", "parallel", "arbitrary"),
                                             vmem_limit_bytes=VMEM_LIMIT),
        name="gla",
    )(q, k, v, la, sg, lng, lnb)


def _post_ffn_kernel(x_ref, y1_ref, y2_ref, wo_ref, g1_ref, b1_ref, w1_ref, w2_ref, g2_ref, b2_ref,
                     o_ref, x1_ref, x1b_ref, acc_ref):
    j = pl.program_id(1)
    half = y1_ref.shape[1]

    @pl.when(j == 0)
    def _():
        y = _dot(y1_ref[...], wo_ref[0:half, :]) + _dot(y2_ref[...], wo_ref[half:2 * half, :])
        x1 = _ln(ALPHA * x_ref[...] + y, g1_ref[...], b1_ref[...])
        x1_ref[...] = x1
        x1b_ref[...] = x1.astype(BF16)
        acc_ref[...] = jnp.zeros_like(acc_ref)

    h = _dot(x1b_ref[...], w1_ref[...])
    h = jnp.square(jnp.maximum(h, 0.0)).astype(BF16)
    acc_ref[...] += _dot(h, w2_ref[...])

    @pl.when(j == pl.num_programs(1) - 1)
    def _():
        o_ref[...] = _ln(ALPHA * x1_ref[...] + acc_ref[...], g2_ref[...], b2_ref[...])


def _post_ffn(x, y1, y2, y2_col, wo, g1, b1, w1, w2, g2, b2, *, tm, tf):
    n = x.shape[0]
    half = D_MODEL // 2
    row = pl.BlockSpec((tm, D_MODEL), lambda i, j: (i, 0))
    vec = pl.BlockSpec((1, D_MODEL), lambda i, j: (0, 0))
    return pl.pallas_call(
        _post_ffn_kernel,
        grid=(n // tm, D_FF // tf),
        in_specs=[row,
                  pl.BlockSpec((tm, half), lambda i, j: (i, 0)),
                  pl.BlockSpec((tm, half), lambda i, j: (i, y2_col)),
                  pl.BlockSpec((D_MODEL, D_MODEL), lambda i, j: (0, 0)),
                  vec, vec,
                  pl.BlockSpec((D_MODEL, tf), lambda i, j: (0, j)),
                  pl.BlockSpec((tf, D_MODEL), lambda i, j: (j, 0)),
                  vec, vec],
        out_specs=row,
        out_shape=jax.ShapeDtypeStruct((n, D_MODEL), F32),
        scratch_shapes=[pltpu.VMEM((tm, D_MODEL), F32), pltpu.VMEM((tm, D_MODEL), BF16),
                        pltpu.VMEM((tm, D_MODEL), F32)],
        compiler_params=pltpu.CompilerParams(dimension_semantics=("parallel", "arbitrary"),
                                             vmem_limit_bytes=VMEM_LIMIT),
        name="post_ffn",
    )(x, y1, y2, wo, g1, b1, w1, w2, g2, b2)


def _rope_inv_lanes():
    half = B_ROPE // 2
    inv = ROPE_THETA ** (-jnp.arange(half, dtype=F32) / half)
    lanes = jnp.zeros((1, HEAD_PAD), F32)
    lanes = lanes.at[0, B_NOPE:B_NOPE + half].set(inv)
    lanes = lanes.at[0, B_NOPE + half:B_NOPE + B_ROPE].set(inv)
    return lanes


def kernel(x, positions, ln1_g, ln1_b, ln2_g, ln2_b, w_in_even, a_w_s, a_b_s, a_ln_g, a_ln_b, b_q_norm, b_kv_norm, b_w_uq, b_w_ukv, w_out_even, w_in_odd, c_w_gate, c_b_gate, c_ln_g, c_ln_b, w_out_odd, w_ff1, w_ff2):
    bn, s, d = x.shape
    n = bn * s
    xf = x.reshape(n, d)
    pos = positions.reshape(n, 1).astype(jnp.int32)
    inv_lane = _rope_inv_lanes()
    causal = jnp.tril(jnp.ones((A_CHUNK, A_CHUNK), dtype=bool))
    vec = lambda a: a.reshape(1, -1)

    for layer in range(DEPTH):
        j = layer // 2
        if layer % 2 == 0:
            w = w_in_even[j]
            o1, o2, o3, o4 = A_WIDTH, 2 * A_WIDTH, 2 * A_WIDTH + B_Q_RANK, 2 * A_WIDTH + B_Q_RANK + B_KV_RANK
            wuv = w[:, :o2].astype(BF16)
            wcq = w[:, o2:o3].astype(BF16)
            wckv = w[:, o3:o4].astype(BF16)
            wkr = jnp.zeros((d, HEAD_PAD), F32).at[:, B_NOPE:B_NOPE + B_ROPE].set(w[:, o4:]).astype(BF16)
            wuq = b_w_uq[j].reshape(B_Q_RANK, B_HEADS, B_NOPE + B_ROPE)
            wuq = jnp.pad(wuq, ((0, 0), (0, 0), (0, HEAD_PAD - B_NOPE - B_ROPE)))
            wuq = wuq.reshape(B_Q_RANK, B_HEADS * HEAD_PAD).astype(BF16)
            wukv = b_w_ukv[j].reshape(B_KV_RANK, B_HEADS, B_NOPE + B_VDIM)
            wuk = jnp.pad(wukv[:, :, :B_NOPE], ((0, 0), (0, 0), (0, HEAD_PAD - B_NOPE)))
            wuk = wuk.reshape(B_KV_RANK, B_HEADS * HEAD_PAD).astype(BF16)
            wuvv = wukv[:, :, B_NOPE:].reshape(B_KV_RANK, B_HEADS * B_VDIM).astype(BF16)
            ws = jnp.where(causal[None], a_w_s[j], 0.0).astype(BF16)
            bs = jnp.broadcast_to(a_b_s[j][:, :, None], (A_GROUPS, A_CHUNK, A_GROUP_DIM))
            ya, q, k, v = _even_pre(xf, pos, inv_lane, wuv, wcq, wckv, wkr, ws, bs,
                                    a_ln_g[j], a_ln_b[j], vec(b_q_norm[j]), vec(b_kv_norm[j]),
                                    wuq, wuk, wuvv, tm=256)
            yb = _attn(q, k, v, batch=bn, seq=s, tq=256)
            y1, y2, y2_col = ya, yb, 0
            wo = w_out_even[j].astype(BF16)
        else:
            w = w_in_odd[j]
            hk, hv = C_HEADS * C_DK, C_HEADS * C_DV
            wq = w[:, :hk].astype(BF16)
            wk = w[:, hk:2 * hk].astype(BF16)
            wv = w[:, 2 * hk:2 * hk + hv].astype(BF16)
            wg = w[:, 2 * hk + hv:2 * hk + 2 * hv].astype(BF16)
            wzg = jnp.zeros((d, 128), F32).at[:, :C_GATE_RANK].set(w[:, 2 * hk + 2 * hv:]).astype(BF16)
            wgate = jnp.zeros((128, hk), F32).at[:C_GATE_RANK].set(c_w_gate[j]).astype(BF16)
            q, k, v, sg, la = _odd_pre(xf, wq, wk, wv, wg, wzg, wgate, vec(c_b_gate[j]), tm=256)
            y = _gla(q, k, v, la, sg, vec(c_ln_g[j]), vec(c_ln_b[j]), batch=bn, seq=s, tc=512)
            y1, y2, y2_col = y, y, 1
            wo = w_out_odd[j].astype(BF16)
        xf = _post_ffn(xf, y1, y2, y2_col, wo, vec(ln1_g[layer]), vec(ln1_b[layer]),
                       w_ff1[layer].astype(BF16), w_ff2[layer].astype(BF16),
                       vec(ln2_g[layer]), vec(ln2_b[layer]), tm=512, tf=1024)
    return xf.reshape(bn, s, d)
```

```python
import functools

import jax
import jax.numpy as jnp
from jax import lax
from jax.experimental import pallas as pl
from jax.experimental.pallas import tpu as pltpu

F32 = jnp.float32
BF16 = jnp.bfloat16

D_MODEL = 1024
DEPTH = 4

A_CHUNK = 128
A_GROUPS = 4
A_WIDTH = 512
A_GROUP_DIM = 128

B_HEADS = 8
B_NOPE = 64
B_ROPE = 32
B_VDIM = 64
B_Q_RANK = 384
B_KV_RANK = 256
ROPE_THETA = 10000.0
HEAD_PAD = 128

C_HEADS = 4
C_DK = 128
C_DV = 256
C_GATE_RANK = 16
C_GATE_TAU = 16.0
GLA_CHUNK = 64
GLA_SUB = 16

D_FF = 4 * D_MODEL
ALPHA = (2.0 * DEPTH) ** 0.25
LN_EPS = 1e-5
LOG2_E = 1.4426950408889634

VMEM_LIMIT = 56 * 1024 * 1024


def _dot(a, b):
    return jnp.dot(a, b, preferred_element_type=F32)


def _dot_nt(a, b):
    return lax.dot_general(a, b, (((1,), (1,)), ((), ())), preferred_element_type=F32)


def _dot_tn(a, b):
    return lax.dot_general(a, b, (((0,), (0,)), ((), ())), preferred_element_type=F32)


def _ln(x, g, b):
    mu = jnp.mean(x, axis=-1, keepdims=True)
    xc = x - mu
    var = jnp.mean(xc * xc, axis=-1, keepdims=True)
    return xc * lax.rsqrt(var + LN_EPS) * g + b


def _rms(x, g):
    ms = jnp.mean(x * x, axis=-1, keepdims=True)
    return x * lax.rsqrt(ms + LN_EPS) * g


def _gelu(x):
    return 0.5 * x * (1.0 + jnp.tanh(0.7978845608028654 * (x + 0.044715 * (x * x * x))))


def _even_pre_kernel(x_ref, pos_ref, inv_ref, wuv_ref, wcq_ref, wckv_ref, wkr_ref,
                     ws_ref, bs_ref, alng_ref, alnb_ref, gq_ref, gkv_ref,
                     wuq_ref, wuk_ref, wuvv_ref,
                     ya_ref, qt_ref, k_ref, vt_ref):
    tm = x_ref.shape[0]
    xb = x_ref[...].astype(BF16)

    zuv = _dot(xb, wuv_ref[...])
    for g in range(A_GROUPS):
        lo = g * A_GROUP_DIM
        u = _gelu(zuv[:, lo:lo + A_GROUP_DIM])
        vv = _gelu(zuv[:, A_WIDTH + lo:A_WIDTH + lo + A_GROUP_DIM])
        vn = _ln(vv, alng_ref[g:g + 1, :], alnb_ref[g:g + 1, :]).astype(BF16)
        for c in range(tm // A_CHUNK):
            r = c * A_CHUNK
            mixed = _dot(ws_ref[g], vn[r:r + A_CHUNK, :]) + bs_ref[g]
            ya_ref[r:r + A_CHUNK, lo:lo + A_GROUP_DIM] = (u[r:r + A_CHUNK, :] * mixed).astype(ya_ref.dtype)

    ang = pos_ref[...].astype(F32) * inv_ref[...]
    cos = jnp.cos(ang)
    sin = jnp.sin(ang)
    lane = lax.broadcasted_iota(jnp.int32, (1, HEAD_PAD), 1)
    half = B_ROPE // 2
    s1 = jnp.where((lane >= B_NOPE) & (lane < B_NOPE + half), -sin, 0.0)
    s2 = jnp.where((lane >= B_NOPE + half) & (lane < B_NOPE + B_ROPE), sin, 0.0)

    def rope(t):
        return t * cos + pltpu.roll(t, HEAD_PAD - half, 1) * s1 + pltpu.roll(t, half, 1) * s2

    scale = (B_NOPE + B_ROPE) ** -0.5 * LOG2_E
    cqn = _rms(_dot(xb, wcq_ref[...]), gq_ref[...]).astype(BF16)
    qf = _dot(cqn, wuq_ref[...])
    ckvn = _rms(_dot(xb, wckv_ref[...]), gkv_ref[...]).astype(BF16)
    kn = _dot(ckvn, wuk_ref[...])
    kr = rope(_dot(xb, wkr_ref[...]))
    for h in range(B_HEADS):
        lo = h * HEAD_PAD
        qh = rope(qf[:, lo:lo + HEAD_PAD]) * scale
        qt_ref[0, lo:lo + HEAD_PAD, :] = qh.T.astype(qt_ref.dtype)
        k_ref[:, lo:lo + HEAD_PAD] = (kn[:, lo:lo + HEAD_PAD] + kr).astype(k_ref.dtype)
    vt_ref[0] = _dot(ckvn, wuvv_ref[...]).T.astype(vt_ref.dtype)


def _even_pre(x, pos, inv_lane, wuv, wcq, wckv, wkr, ws, bs, alng, alnb, gq, gkv, wuq, wuk, wuvv, *, tm):
    n = x.shape[0]
    row = lambda w: pl.BlockSpec((tm, w), lambda i: (i, 0))
    full2 = lambda a: pl.BlockSpec(a.shape, lambda i: (0, 0))
    full3 = lambda a: pl.BlockSpec(a.shape, lambda i: (0, 0, 0))
    hp = B_HEADS * HEAD_PAD
    return pl.pallas_call(
        _even_pre_kernel,
        grid=(n // tm,),
        in_specs=[row(D_MODEL), row(1), full2(inv_lane), full2(wuv), full2(wcq), full2(wckv), full2(wkr),
                  full3(ws), full3(bs), full2(alng), full2(alnb), full2(gq), full2(gkv),
                  full2(wuq), full2(wuk), full2(wuvv)],
        out_specs=[row(A_WIDTH),
                   pl.BlockSpec((1, hp, tm), lambda i: (i, 0, 0)),
                   row(hp),
                   pl.BlockSpec((1, B_HEADS * B_VDIM, tm), lambda i: (i, 0, 0))],
        out_shape=[jax.ShapeDtypeStruct((n, A_WIDTH), BF16),
                   jax.ShapeDtypeStruct((n // tm, hp, tm), BF16),
                   jax.ShapeDtypeStruct((n, hp), BF16),
                   jax.ShapeDtypeStruct((n // tm, B_HEADS * B_VDIM, tm), BF16)],
        compiler_params=pltpu.CompilerParams(dimension_semantics=("parallel",),
                                             vmem_limit_bytes=VMEM_LIMIT),
        name="even_pre",
    )(x, pos, inv_lane, wuv, wcq, wckv, wkr, ws, bs, alng, alnb, gq, gkv, wuq, wuk, wuvv)


def _attn_kernel(qt_ref, k_ref, vt_ref, o_ref, sa_ref, sb_ref):
    t = o_ref.shape[0]
    qi = pl.program_id(2)

    def scores(j, dst):
        start = pl.multiple_of(j * t, t)
        for hh in range(2):
            lo = hh * HEAD_PAD
            dst[hh] = _dot(k_ref[pl.ds(start, t), lo:lo + HEAD_PAD], qt_ref[0, lo:lo + HEAD_PAD, :])

    def update(j, src, stats, masked):
        vt = vt_ref[j]
        new = []
        for hh in range(2):
            m, l, acc = stats[hh]
            s = src[hh]
            if masked:
                key_ids = lax.broadcasted_iota(jnp.int32, (t, t), 0)
                qry_ids = lax.broadcasted_iota(jnp.int32, (t, t), 1)
                s = jnp.where(key_ids <= qry_ids, s, -jnp.inf)
            m_new = jnp.maximum(m, jnp.max(s, axis=0, keepdims=True))
            p = jnp.exp2(s - m_new)
            a = jnp.exp2(m - m_new)
            l = a * l + jnp.sum(p, axis=0, keepdims=True)
            acc = a * acc + _dot(vt[hh * B_VDIM:(hh + 1) * B_VDIM, :], p.astype(BF16))
            new.append((m_new, l, acc))
        return tuple(new)

    def two_blocks(i, stats):
        j = 2 * i
        scores(j + 1, sb_ref)
        stats = update(j, sa_ref, stats, False)
        scores(j + 2, sa_ref)
        return update(j + 1, sb_ref, stats, False)

    def tail_odd(stats):
        scores(qi, sb_ref)
        stats = update(qi - 1, sa_ref, stats, False)
        return update(qi, sb_ref, stats, True)

    def tail_even(stats):
        return update(qi, sa_ref, stats, True)

    scores(0, sa_ref)
    init1 = (jnp.full((1, t), -jnp.inf, F32), jnp.zeros((1, t), F32), jnp.zeros((B_VDIM, t), F32))
    stats = lax.fori_loop(0, qi // 2, two_blocks, (init1, init1))
    stats = lax.cond(qi % 2 == 1, tail_odd, tail_even, stats)
    out_t = jnp.concatenate([acc / l for (_, l, acc) in stats], axis=0)
    o_ref[...] = out_t.T.astype(o_ref.dtype)


def _attn(qt, k, vt, *, batch, seq):
    t = qt.shape[2]
    n = k.shape[0]
    nq = seq // t
    return pl.pallas_call(
        _attn_kernel,
        grid=(batch, B_HEADS // 2, nq),
        in_specs=[pl.BlockSpec((1, 2 * HEAD_PAD, t), lambda b, h, i: (b * nq + i, h, 0)),
                  pl.BlockSpec((seq, 2 * HEAD_PAD), lambda b, h, i: (b, h)),
                  pl.BlockSpec((nq, 2 * B_VDIM, t), lambda b, h, i: (b, h, 0))],
        out_specs=pl.BlockSpec((t, 2 * B_VDIM), lambda b, h, i: (b * nq + i, h)),
        out_shape=jax.ShapeDtypeStruct((n, B_HEADS * B_VDIM), BF16),
        scratch_shapes=[pltpu.VMEM((2, t, t), F32), pltpu.VMEM((2, t, t), F32)],
        compiler_params=pltpu.CompilerParams(dimension_semantics=("parallel", "parallel", "arbitrary"),
                                             vmem_limit_bytes=VMEM_LIMIT),
        name="attn",
    )(qt, k, vt)


def _odd_pre_kernel(x_ref, wq_ref, wk_ref, wv_ref, wg_ref, wzg_ref, wgate_ref, bgate_ref,
                    q_ref, k_ref, v_ref, sg_ref, la_ref):
    xb = x_ref[...].astype(BF16)
    q_ref[...] = _dot(xb, wq_ref[...]) * (C_DK ** -0.5)
    k_ref[...] = _dot(xb, wk_ref[...])
    v_ref[...] = _dot(xb, wv_ref[...]).astype(v_ref.dtype)
    g = _dot(xb, wg_ref[...])
    sg_ref[...] = g * jax.nn.sigmoid(g)
    zg = _dot(xb, wzg_ref[...]).astype(BF16)
    logits = _dot(zg, wgate_ref[...]) + bgate_ref[...]
    la_ref[...] = -(jnp.maximum(-logits, 0.0) + jnp.log1p(jnp.exp(-jnp.abs(logits)))) * (1.0 / C_GATE_TAU)


def _odd_pre(x, wq, wk, wv, wg, wzg, wgate, bgate, *, tm):
    n = x.shape[0]
    row = lambda w: pl.BlockSpec((tm, w), lambda i: (i, 0))
    full2 = lambda a: pl.BlockSpec(a.shape, lambda i: (0, 0))
    hk, hv = C_HEADS * C_DK, C_HEADS * C_DV
    return pl.pallas_call(
        _odd_pre_kernel,
        grid=(n // tm,),
        in_specs=[row(D_MODEL), full2(wq), full2(wk), full2(wv), full2(wg), full2(wzg), full2(wgate), full2(bgate)],
        out_specs=[row(hk), row(hk), row(hv), row(hv), row(hk)],
        out_shape=[jax.ShapeDtypeStruct((n, hk), F32),
                   jax.ShapeDtypeStruct((n, hk), F32),
                   jax.ShapeDtypeStruct((n, hv), BF16),
                   jax.ShapeDtypeStruct((n, hv), F32),
                   jax.ShapeDtypeStruct((n, hk), F32)],
        compiler_params=pltpu.CompilerParams(dimension_semantics=("parallel",),
                                             vmem_limit_bytes=VMEM_LIMIT),
        name="odd_pre",
    )(x, wq, wk, wv, wg, wzg, wgate, bgate)


def _gla_kernel(q_ref, k_ref, v_ref, la_ref, sg_ref, lng_ref, lnb_ref, y_ref, st_ref, b_scr):
    tc = q_ref.shape[0]
    C, c = GLA_CHUNK, GLA_SUB

    @pl.when(pl.program_id(2) == 0)
    def _():
        st_ref[...] = jnp.zeros_like(st_ref)

    tri = (lax.broadcasted_iota(jnp.int32, (C, C), 1) <= lax.broadcasted_iota(jnp.int32, (C, C), 0)).astype(F32)
    lane_c = lax.broadcasted_iota(jnp.int32, (1, C), 1)
    t_ids = lax.broadcasted_iota(jnp.int32, (c, 1), 0)

    def chunk(ci, _):
        r0 = pl.multiple_of(ci * C, C)
        b = jnp.dot(tri, la_ref[pl.ds(r0, C), :], preferred_element_type=F32,
                    precision=lax.Precision.HIGHEST)
        b_scr[...] = b
        q = q_ref[pl.ds(r0, C), :]
        k = k_ref[pl.ds(r0, C), :]
        v = v_ref[pl.ds(r0, C), :]
        st = st_ref[...]
        o = _dot_nt((q * jnp.exp(b)).astype(BF16), st.astype(BF16))
        b_last = b[C - 1:C, :]

        rows = []
        for i in range(C // c):
            lo = i * c
            bi = b[lo:lo + c, :]
            qi = q[lo:lo + c, :]
            ref_b = b[lo - 1:lo, :] if i > 0 else jnp.zeros((1, C_DK), F32)
            qt = (qi * jnp.exp(bi - ref_b)).astype(BF16)
            kt = (k * jnp.exp(jnp.minimum(ref_b - b, 0.0))).astype(BF16)
            a_blk = jnp.where(lane_c < lo, _dot_nt(qt, kt), 0.0)

            def diag(s, a_blk, lo=lo, bi=bi, qi=qi):
                bs = b_scr[pl.ds(lo + s, 1), :]
                ks = k_ref[pl.ds(r0 + lo + s, 1), :]
                e = jnp.exp(jnp.minimum(bi - bs, 0.0))
                col = jnp.sum(qi * e * ks, axis=-1, keepdims=True)
                return a_blk + jnp.where((lane_c == lo + s) & (t_ids >= s), col, 0.0)

            rows.append(lax.fori_loop(0, c, diag, a_blk))
        attn = jnp.concatenate(rows, axis=0)
        o = o + _dot(attn.astype(BF16), v)

        kd = (k * jnp.exp(b_last - b)).astype(BF16)
        st_ref[...] = st * jnp.exp(b_last) + _dot_tn(v, kd)

        on = _ln(o, lng_ref[...], lnb_ref[...])
        y_ref[pl.ds(r0, C), :] = (on * sg_ref[pl.ds(r0, C), :]).astype(y_ref.dtype)
        return 0

    lax.fori_loop(0, tc // C, chunk, 0)


def _gla(q, k, v, la, sg, lng, lnb, *, batch, seq, tc):
    n = q.shape[0]
    nt = seq // tc
    kblk = pl.BlockSpec((tc, C_DK), lambda b, h, i: (b * nt + i, h))
    vblk = pl.BlockSpec((tc, C_DV), lambda b, h, i: (b * nt + i, h))
    vec = pl.BlockSpec((1, C_DV), lambda b, h, i: (0, 0))
    return pl.pallas_call(
        _gla_kernel,
        grid=(batch, C_HEADS, nt),
        in_specs=[kblk, kblk, vblk, kblk, vblk, vec, vec],
        out_specs=vblk,
        out_shape=jax.ShapeDtypeStruct((n, C_HEADS * C_DV), BF16),
        scratch_shapes=[pltpu.VMEM((C_DV, C_DK), F32), pltpu.VMEM((GLA_CHUNK, C_DK), F32)],
        compiler_params=pltpu.CompilerParams(dimension_semantics=("parallel", "parallel", "arbitrary"),
                                             vmem_limit_bytes=VMEM_LIMIT),
        name="gla",
    )(q, k, v, la, sg, lng, lnb)


def _post_ffn_kernel(x_ref, y1_ref, y2_ref, wo_ref, g1_ref, b1_ref, w1_ref, w2_ref, g2_ref, b2_ref,
                     o_ref, x1_ref, x1b_ref, acc_ref):
    j = pl.program_id(1)
    half = y1_ref.shape[1]

    @pl.when(j == 0)
    def _():
        y = _dot(y1_ref[...], wo_ref[0:half, :]) + _dot(y2_ref[...], wo_ref[half:2 * half, :])
        x1 = _ln(ALPHA * x_ref[...] + y, g1_ref[...], b1_ref[...])
        x1_ref[...] = x1
        x1b_ref[...] = x1.astype(BF16)
        acc_ref[...] = jnp.zeros_like(acc_ref)

    h = _dot(x1b_ref[...], w1_ref[...])
    h = jnp.square(jnp.maximum(h, 0.0)).astype(BF16)
    acc_ref[...] += _dot(h, w2_ref[...])

    @pl.when(j == pl.num_programs(1) - 1)
    def _():
        o_ref[...] = _ln(ALPHA * x1_ref[...] + acc_ref[...], g2_ref[...], b2_ref[...])


def _post_ffn(x, y1, y2, y2_col, wo, g1, b1, w1, w2, g2, b2, *, tm, tf):
    n = x.shape[0]
    half = D_MODEL // 2
    row = pl.BlockSpec((tm, D_MODEL), lambda i, j: (i, 0))
    vec = pl.BlockSpec((1, D_MODEL), lambda i, j: (0, 0))
    return pl.pallas_call(
        _post_ffn_kernel,
        grid=(n // tm, D_FF // tf),
        in_specs=[row,
                  pl.BlockSpec((tm, half), lambda i, j: (i, 0)),
                  pl.BlockSpec((tm, half), lambda i, j: (i, y2_col)),
                  pl.BlockSpec((D_MODEL, D_MODEL), lambda i, j: (0, 0)),
                  vec, vec,
                  pl.BlockSpec((D_MODEL, tf), lambda i, j: (0, j)),
                  pl.BlockSpec((tf, D_MODEL), lambda i, j: (j, 0)),
                  vec, vec],
        out_specs=row,
        out_shape=jax.ShapeDtypeStruct((n, D_MODEL), F32),
        scratch_shapes=[pltpu.VMEM((tm, D_MODEL), F32), pltpu.VMEM((tm, D_MODEL), BF16),
                        pltpu.VMEM((tm, D_MODEL), F32)],
        compiler_params=pltpu.CompilerParams(dimension_semantics=("parallel", "arbitrary"),
                                             vmem_limit_bytes=VMEM_LIMIT),
        name="post_ffn",
    )(x, y1, y2, wo, g1, b1, w1, w2, g2, b2)


def _rope_inv_lanes():
    half = B_ROPE // 2
    inv = ROPE_THETA ** (-jnp.arange(half, dtype=F32) / half)
    lanes = jnp.zeros((1, HEAD_PAD), F32)
    lanes = lanes.at[0, B_NOPE:B_NOPE + half].set(inv)
    lanes = lanes.at[0, B_NOPE + half:B_NOPE + B_ROPE].set(inv)
    return lanes


def kernel(x, positions, ln1_g, ln1_b, ln2_g, ln2_b, w_in_even, a_w_s, a_b_s, a_ln_g, a_ln_b, b_q_norm, b_kv_norm, b_w_uq, b_w_ukv, w_out_even, w_in_odd, c_w_gate, c_b_gate, c_ln_g, c_ln_b, w_out_odd, w_ff1, w_ff2):
    bn, s, d = x.shape
    n = bn * s
    xf = x.reshape(n, d)
    pos = positions.reshape(n, 1).astype(jnp.int32)
    inv_lane = _rope_inv_lanes()
    causal = jnp.tril(jnp.ones((A_CHUNK, A_CHUNK), dtype=bool))
    vec = lambda a: a.reshape(1, -1)

    for layer in range(DEPTH):
        j = layer // 2
        if layer % 2 == 0:
            w = w_in_even[j]
            o1, o2, o3, o4 = A_WIDTH, 2 * A_WIDTH, 2 * A_WIDTH + B_Q_RANK, 2 * A_WIDTH + B_Q_RANK + B_KV_RANK
            wuv = w[:, :o2].astype(BF16)
            wcq = w[:, o2:o3].astype(BF16)
            wckv = w[:, o3:o4].astype(BF16)
            wkr = jnp.zeros((d, HEAD_PAD), F32).at[:, B_NOPE:B_NOPE + B_ROPE].set(w[:, o4:]).astype(BF16)
            wuq = b_w_uq[j].reshape(B_Q_RANK, B_HEADS, B_NOPE + B_ROPE)
            wuq = jnp.pad(wuq, ((0, 0), (0, 0), (0, HEAD_PAD - B_NOPE - B_ROPE)))
            wuq = wuq.reshape(B_Q_RANK, B_HEADS * HEAD_PAD).astype(BF16)
            wukv = b_w_ukv[j].reshape(B_KV_RANK, B_HEADS, B_NOPE + B_VDIM)
            wuk = jnp.pad(wukv[:, :, :B_NOPE], ((0, 0), (0, 0), (0, HEAD_PAD - B_NOPE)))
            wuk = wuk.reshape(B_KV_RANK, B_HEADS * HEAD_PAD).astype(BF16)
            wuvv = wukv[:, :, B_NOPE:].reshape(B_KV_RANK, B_HEADS * B_VDIM).astype(BF16)
            ws = jnp.where(causal[None], a_w_s[j], 0.0).astype(BF16)
            bs = jnp.broadcast_to(a_b_s[j][:, :, None], (A_GROUPS, A_CHUNK, A_GROUP_DIM))
            ya, qt, k, vt = _even_pre(xf, pos, inv_lane, wuv, wcq, wckv, wkr, ws, bs,
                                    a_ln_g[j], a_ln_b[j], vec(b_q_norm[j]), vec(b_kv_norm[j]),
                                    wuq, wuk, wuvv, tm=512)
            yb = _attn(qt, k, vt, batch=bn, seq=s)
            y1, y2, y2_col = ya, yb, 0
            wo = w_out_even[j].astype(BF16)
        else:
            w = w_in_odd[j]
            hk, hv = C_HEADS * C_DK, C_HEADS * C_DV
            wq = w[:, :hk].astype(BF16)
            wk = w[:, hk:2 * hk].astype(BF16)
            wv = w[:, 2 * hk:2 * hk + hv].astype(BF16)
            wg = w[:, 2 * hk + hv:2 * hk + 2 * hv].astype(BF16)
            wzg = jnp.zeros((d, 128), F32).at[:, :C_GATE_RANK].set(w[:, 2 * hk + 2 * hv:]).astype(BF16)
            wgate = jnp.zeros((128, hk), F32).at[:C_GATE_RANK].set(c_w_gate[j]).astype(BF16)
            q, k, v, sg, la = _odd_pre(xf, wq, wk, wv, wg, wzg, wgate, vec(c_b_gate[j]), tm=256)
            y = _gla(q, k, v, la, sg, vec(c_ln_g[j]), vec(c_ln_b[j]), batch=bn, seq=s, tc=512)
            y1, y2, y2_col = y, y, 1
            wo = w_out_odd[j].astype(BF16)
        xf = _post_ffn(xf, y1, y2, y2_col, wo, vec(ln1_g[layer]), vec(ln1_b[layer]),
                       w_ff1[layer].astype(BF16), w_ff2[layer].astype(BF16),
                       vec(ln2_g[layer]), vec(ln2_b[layer]), tm=512, tf=1024)
    return xf.reshape(bn, s, d)
```

```python
import functools

import jax
import jax.numpy as jnp
from jax import lax
from jax.experimental import pallas as pl
from jax.experimental.pallas import tpu as pltpu

F32 = jnp.float32
BF16 = jnp.bfloat16

D_MODEL = 1024
DEPTH = 4

A_CHUNK = 128
A_GROUPS = 4
A_WIDTH = 512
A_GROUP_DIM = 128

B_HEADS = 8
B_NOPE = 64
B_ROPE = 32
B_VDIM = 64
B_Q_RANK = 384
B_KV_RANK = 256
ROPE_THETA = 10000.0
HEAD_PAD = 128

C_HEADS = 4
C_DK = 128
C_DV = 256
C_GATE_RANK = 16
C_GATE_TAU = 16.0
GLA_CHUNK = 256
GLA_FINE_LEVELS = (1, 2, 4)

D_FF = 4 * D_MODEL
ALPHA = (2.0 * DEPTH) ** 0.25
LN_EPS = 1e-5
LOG2_E = 1.4426950408889634

VMEM_LIMIT = 56 * 1024 * 1024


def _dot(a, b):
    return jnp.dot(a, b, preferred_element_type=F32)


def _dot_nt(a, b):
    return lax.dot_general(a, b, (((1,), (1,)), ((), ())), preferred_element_type=F32)


def _dot_tn(a, b):
    return lax.dot_general(a, b, (((0,), (0,)), ((), ())), preferred_element_type=F32)


def _ln(x, g, b):
    mu = jnp.mean(x, axis=-1, keepdims=True)
    xc = x - mu
    var = jnp.mean(xc * xc, axis=-1, keepdims=True)
    return xc * lax.rsqrt(var + LN_EPS) * g + b


def _rms(x, g):
    ms = jnp.mean(x * x, axis=-1, keepdims=True)
    return x * lax.rsqrt(ms + LN_EPS) * g


def _gelu(x):
    return 0.5 * x * (1.0 + jnp.tanh(0.7978845608028654 * (x + 0.044715 * (x * x * x))))


def _even_pre_kernel(x_ref, pos_ref, inv_ref, wuv_ref, wcq_ref, wckv_ref, wkr_ref,
                     ws_ref, bs_ref, alng_ref, alnb_ref, gq_ref, gkv_ref,
                     wuq_ref, wuk_ref, wuvv_ref,
                     ya_ref, qt_ref, k_ref, vt_ref):
    tm = x_ref.shape[0]
    xb = x_ref[...].astype(BF16)

    zuv = _dot(xb, wuv_ref[...])
    for g in range(A_GROUPS):
        lo = g * A_GROUP_DIM
        u = _gelu(zuv[:, lo:lo + A_GROUP_DIM])
        vv = _gelu(zuv[:, A_WIDTH + lo:A_WIDTH + lo + A_GROUP_DIM])
        vn = _ln(vv, alng_ref[g:g + 1, :], alnb_ref[g:g + 1, :]).astype(BF16)
        for c in range(tm // A_CHUNK):
            r = c * A_CHUNK
            mixed = _dot(ws_ref[g], vn[r:r + A_CHUNK, :]) + bs_ref[g]
            ya_ref[r:r + A_CHUNK, lo:lo + A_GROUP_DIM] = (u[r:r + A_CHUNK, :] * mixed).astype(ya_ref.dtype)

    ang = pos_ref[...].astype(F32) * inv_ref[...]
    cos = jnp.cos(ang)
    sin = jnp.sin(ang)
    lane = lax.broadcasted_iota(jnp.int32, (1, HEAD_PAD), 1)
    half = B_ROPE // 2
    s1 = jnp.where((lane >= B_NOPE) & (lane < B_NOPE + half), -sin, 0.0)
    s2 = jnp.where((lane >= B_NOPE + half) & (lane < B_NOPE + B_ROPE), sin, 0.0)

    def rope(t):
        return t * cos + pltpu.roll(t, HEAD_PAD - half, 1) * s1 + pltpu.roll(t, half, 1) * s2

    scale = (B_NOPE + B_ROPE) ** -0.5 * LOG2_E
    cqn = _rms(_dot(xb, wcq_ref[...]), gq_ref[...]).astype(BF16)
    qf = _dot(cqn, wuq_ref[...])
    ckvn = _rms(_dot(xb, wckv_ref[...]), gkv_ref[...]).astype(BF16)
    kn = _dot(ckvn, wuk_ref[...])
    kr = rope(_dot(xb, wkr_ref[...]))
    for h in range(B_HEADS):
        lo = h * HEAD_PAD
        qh = rope(qf[:, lo:lo + HEAD_PAD]) * scale
        qt_ref[0, lo:lo + HEAD_PAD, :] = qh.T.astype(qt_ref.dtype)
        k_ref[:, lo:lo + HEAD_PAD] = (kn[:, lo:lo + HEAD_PAD] + kr).astype(k_ref.dtype)
    vt_ref[0] = _dot(ckvn, wuvv_ref[...]).T.astype(vt_ref.dtype)


def _even_pre(x, pos, inv_lane, wuv, wcq, wckv, wkr, ws, bs, alng, alnb, gq, gkv, wuq, wuk, wuvv, *, tm):
    n = x.shape[0]
    row = lambda w: pl.BlockSpec((tm, w), lambda i: (i, 0))
    full2 = lambda a: pl.BlockSpec(a.shape, lambda i: (0, 0))
    full3 = lambda a: pl.BlockSpec(a.shape, lambda i: (0, 0, 0))
    hp = B_HEADS * HEAD_PAD
    return pl.pallas_call(
        _even_pre_kernel,
        grid=(n // tm,),
        in_specs=[row(D_MODEL), row(1), full2(inv_lane), full2(wuv), full2(wcq), full2(wckv), full2(wkr),
                  full3(ws), full3(bs), full2(alng), full2(alnb), full2(gq), full2(gkv),
                  full2(wuq), full2(wuk), full2(wuvv)],
        out_specs=[row(A_WIDTH),
                   pl.BlockSpec((1, hp, tm), lambda i: (i, 0, 0)),
                   row(hp),
                   pl.BlockSpec((1, B_HEADS * B_VDIM, tm), lambda i: (i, 0, 0))],
        out_shape=[jax.ShapeDtypeStruct((n, A_WIDTH), BF16),
                   jax.ShapeDtypeStruct((n // tm, hp, tm), BF16),
                   jax.ShapeDtypeStruct((n, hp), BF16),
                   jax.ShapeDtypeStruct((n // tm, B_HEADS * B_VDIM, tm), BF16)],
        compiler_params=pltpu.CompilerParams(dimension_semantics=("parallel",),
                                             vmem_limit_bytes=VMEM_LIMIT),
        name="even_pre",
    )(x, pos, inv_lane, wuv, wcq, wckv, wkr, ws, bs, alng, alnb, gq, gkv, wuq, wuk, wuvv)


def _attn_kernel(qt_ref, k_ref, vt_ref, o_ref, sa_ref, sb_ref):
    t = o_ref.shape[0]
    qi = pl.program_id(2)

    def scores(j, dst):
        start = pl.multiple_of(j * t, t)
        for hh in range(2):
            lo = hh * HEAD_PAD
            dst[hh] = _dot(k_ref[pl.ds(start, t), lo:lo + HEAD_PAD], qt_ref[0, lo:lo + HEAD_PAD, :])

    def update(j, src, stats, masked):
        vt = vt_ref[j]
        new = []
        for hh in range(2):
            m, l, acc = stats[hh]
            s = src[hh]
            if masked:
                key_ids = lax.broadcasted_iota(jnp.int32, (t, t), 0)
                qry_ids = lax.broadcasted_iota(jnp.int32, (t, t), 1)
                s = jnp.where(key_ids <= qry_ids, s, -jnp.inf)
            m_new = jnp.maximum(m, jnp.max(s, axis=0, keepdims=True))
            p = jnp.exp2(s - m_new)
            a = jnp.exp2(m - m_new)
            l = a * l + jnp.sum(p, axis=0, keepdims=True)
            acc = a * acc + _dot(vt[hh * B_VDIM:(hh + 1) * B_VDIM, :], p.astype(BF16))
            new.append((m_new, l, acc))
        return tuple(new)

    def two_blocks(i, stats):
        j = 2 * i
        scores(j + 1, sb_ref)
        stats = update(j, sa_ref, stats, False)
        scores(j + 2, sa_ref)
        return update(j + 1, sb_ref, stats, False)

    def tail_odd(stats):
        scores(qi, sb_ref)
        stats = update(qi - 1, sa_ref, stats, False)
        return update(qi, sb_ref, stats, True)

    def tail_even(stats):
        return update(qi, sa_ref, stats, True)

    scores(0, sa_ref)
    init1 = (jnp.full((1, t), -jnp.inf, F32), jnp.zeros((1, t), F32), jnp.zeros((B_VDIM, t), F32))
    stats = lax.fori_loop(0, qi // 2, two_blocks, (init1, init1))
    stats = lax.cond(qi % 2 == 1, tail_odd, tail_even, stats)
    out_t = jnp.concatenate([acc / l for (_, l, acc) in stats], axis=0)
    o_ref[...] = out_t.T.astype(o_ref.dtype)


def _attn(qt, k, vt, *, batch, seq):
    t = qt.shape[2]
    n = k.shape[0]
    nq = seq // t
    return pl.pallas_call(
        _attn_kernel,
        grid=(batch, B_HEADS // 2, nq),
        in_specs=[pl.BlockSpec((1, 2 * HEAD_PAD, t), lambda b, h, i: (b * nq + i, h, 0)),
                  pl.BlockSpec((seq, 2 * HEAD_PAD), lambda b, h, i: (b, h)),
                  pl.BlockSpec((nq, 2 * B_VDIM, t), lambda b, h, i: (b, h, 0))],
        out_specs=pl.BlockSpec((t, 2 * B_VDIM), lambda b, h, i: (b * nq + i, h)),
        out_shape=jax.ShapeDtypeStruct((n, B_HEADS * B_VDIM), BF16),
        scratch_shapes=[pltpu.VMEM((2, t, t), F32), pltpu.VMEM((2, t, t), F32)],
        compiler_params=pltpu.CompilerParams(dimension_semantics=("parallel", "parallel", "arbitrary"),
                                             vmem_limit_bytes=VMEM_LIMIT),
        name="attn",
    )(qt, k, vt)


def _odd_pre_kernel(x_ref, wq_ref, wk_ref, wv_ref, wg_ref, wzg_ref, wgate_ref, bgate_ref,
                    q_ref, k_ref, v_ref, sg_ref, la_ref):
    xb = x_ref[...].astype(BF16)
    q_ref[...] = _dot(xb, wq_ref[...]) * (C_DK ** -0.5)
    k_ref[...] = _dot(xb, wk_ref[...])
    v_ref[...] = _dot(xb, wv_ref[...]).astype(v_ref.dtype)
    g = _dot(xb, wg_ref[...])
    sg_ref[...] = g * jax.nn.sigmoid(g)
    zg = _dot(xb, wzg_ref[...]).astype(BF16)
    logits = _dot(zg, wgate_ref[...]) + bgate_ref[...]
    la_ref[...] = -(jnp.maximum(-logits, 0.0) + jnp.log1p(jnp.exp(-jnp.abs(logits)))) * (1.0 / C_GATE_TAU)


def _odd_pre(x, wq, wk, wv, wg, wzg, wgate, bgate, *, tm):
    n = x.shape[0]
    row = lambda w: pl.BlockSpec((tm, w), lambda i: (i, 0))
    full2 = lambda a: pl.BlockSpec(a.shape, lambda i: (0, 0))
    hk, hv = C_HEADS * C_DK, C_HEADS * C_DV
    return pl.pallas_call(
        _odd_pre_kernel,
        grid=(n // tm,),
        in_specs=[row(D_MODEL), full2(wq), full2(wk), full2(wv), full2(wg), full2(wzg), full2(wgate), full2(bgate)],
        out_specs=[row(hk), row(hk), row(hv), row(hv), row(hk)],
        out_shape=[jax.ShapeDtypeStruct((n, hk), F32),
                   jax.ShapeDtypeStruct((n, hk), F32),
                   jax.ShapeDtypeStruct((n, hv), BF16),
                   jax.ShapeDtypeStruct((n, hv), F32),
                   jax.ShapeDtypeStruct((n, hk), F32)],
        compiler_params=pltpu.CompilerParams(dimension_semantics=("parallel",),
                                             vmem_limit_bytes=VMEM_LIMIT),
        name="odd_pre",
    )(x, wq, wk, wv, wg, wzg, wgate, bgate)


def _gla_level_matrix():
    c = GLA_CHUNK
    t = jnp.arange(c)[:, None]
    u = jnp.arange(c)[None, :]
    mats = []
    for m in GLA_FINE_LEVELS:
        r = t % (2 * m)
        p = t - r + m - 1
        upper = (r >= m) & (u > p) & (u <= t)
        lower = (r < m) & (u > t) & (u <= p)
        mats.append(upper | lower)
    return jnp.concatenate(mats, axis=0).astype(BF16)


def _split_bf16(x, pieces):
    out = []
    for _ in range(pieces - 1):
        hi = x.astype(BF16)
        out.append(hi)
        x = x - hi.astype(F32)
    out.append(x.astype(BF16))
    return out


def _gla_kernel(q_ref, k_ref, v_ref, la_ref, sg_ref, lng_ref, lnb_ref, tri_ref, lvl_ref,
                y_ref, st_ref, a_scr):
    c = GLA_CHUNK
    hk = C_HEADS * C_DK

    @pl.when(pl.program_id(1) == 0)
    def _():
        st_ref[...] = jnp.zeros_like(st_ref)

    q = q_ref[...]
    k = k_ref[...]
    g = _split_bf16(la_ref[...], 3)
    tri = tri_ref[...]
    b = _dot(tri, g[0]) + _dot(tri, g[1]) + _dot(tri, g[2])
    fine = _dot(lvl_ref[...], g[0]) + _dot(lvl_ref[...], g[1])

    row = lax.broadcasted_iota(jnp.int32, (c, 1), 0)
    pair = lax.broadcasted_iota(jnp.int32, (c, c), 0) ^ lax.broadcasted_iota(jnp.int32, (c, c), 1)

    qb = q.astype(BF16)
    kb = k.astype(BF16)
    for h in range(C_HEADS):
        hs = slice(h * C_DK, (h + 1) * C_DK)
        a_scr[h] = jnp.where(pair == 0, _dot_nt(qb[:, hs], kb[:, hs]), 0.0)

    m = 1
    while m < c:
        if m in GLA_FINE_LEVELS:
            i = GLA_FINE_LEVELS.index(m)
            d = fine[i * c:(i + 1) * c, :]
        else:
            b3 = b.reshape(c // (2 * m), 2 * m, hk)
            d3 = b3 - b3[:, m - 1:m, :]
            sub = lax.broadcasted_iota(jnp.int32, (1, 2 * m, 1), 1)
            d = jnp.where(sub >= m, d3, -d3).reshape(c, hk)
        e = jnp.exp(d)
        upper = (row & m) != 0
        qe = jnp.where(upper, q * e, 0.0).astype(BF16)
        ke = jnp.where(upper, 0.0, k * e).astype(BF16)
        level = (pair >= m) & (pair < 2 * m)
        for h in range(C_HEADS):
            hs = slice(h * C_DK, (h + 1) * C_DK)
            a_scr[h] = jnp.where(level, _dot_nt(qe[:, hs], ke[:, hs]), a_scr[h])
        m *= 2

    b_last = b[c - 1:c, :]
    qs = (q * jnp.exp(b)).astype(BF16)
    kd = (k * jnp.exp(b_last - b)).astype(BF16)
    decay = jnp.exp(b_last)
    for h in range(C_HEADS):
        hs = slice(h * C_DK, (h + 1) * C_DK)
        vs = slice(h * C_DV, (h + 1) * C_DV)
        v = v_ref[:, vs]
        st = st_ref[h]
        o = _dot(a_scr[h].astype(BF16), v) + _dot_nt(qs[:, hs], st.astype(BF16))
        st_ref[h] = st * decay[:, hs] + _dot_tn(v, kd[:, hs])
        on = _ln(o, lng_ref[...], lnb_ref[...])
        y_ref[:, vs] = (on * sg_ref[:, vs]).astype(y_ref.dtype)


def _gla(q, k, v, la, sg, lng, lnb, *, batch, seq):
    n = q.shape[0]
    c = GLA_CHUNK
    nt = seq // c
    hk, hv = C_HEADS * C_DK, C_HEADS * C_DV
    tri = jnp.tril(jnp.ones((c, c), F32)).astype(BF16)
    lvl = _gla_level_matrix()
    kblk = pl.BlockSpec((c, hk), lambda b, i: (b * nt + i, 0))
    vblk = pl.BlockSpec((c, hv), lambda b, i: (b * nt + i, 0))
    full2 = lambda a: pl.BlockSpec(a.shape, lambda b, i: (0, 0))
    return pl.pallas_call(
        _gla_kernel,
        grid=(batch, nt),
        in_specs=[kblk, kblk, vblk, kblk, vblk, full2(lng), full2(lnb), full2(tri), full2(lvl)],
        out_specs=vblk,
        out_shape=jax.ShapeDtypeStruct((n, hv), BF16),
        scratch_shapes=[pltpu.VMEM((C_HEADS, C_DV, C_DK), F32), pltpu.VMEM((C_HEADS, c, c), F32)],
        compiler_params=pltpu.CompilerParams(dimension_semantics=("parallel", "arbitrary"),
                                             vmem_limit_bytes=VMEM_LIMIT),
        name="gla",
    )(q, k, v, la, sg, lng, lnb, tri, lvl)


def _post_ffn_kernel(x_ref, y1_ref, y2_ref, wo_ref, g1_ref, b1_ref, w1_ref, w2_ref, g2_ref, b2_ref,
                     o_ref, x1_ref, x1b_ref, acc_ref):
    j = pl.program_id(1)
    half = y1_ref.shape[1]

    @pl.when(j == 0)
    def _():
        y = _dot(y1_ref[...], wo_ref[0:half, :]) + _dot(y2_ref[...], wo_ref[half:2 * half, :])
        x1 = _ln(ALPHA * x_ref[...] + y, g1_ref[...], b1_ref[...])
        x1_ref[...] = x1
        x1b_ref[...] = x1.astype(BF16)
        acc_ref[...] = jnp.zeros_like(acc_ref)

    h = _dot(x1b_ref[...], w1_ref[...])
    h = jnp.square(jnp.maximum(h, 0.0)).astype(BF16)
    acc_ref[...] += _dot(h, w2_ref[...])

    @pl.when(j == pl.num_programs(1) - 1)
    def _():
        o_ref[...] = _ln(ALPHA * x1_ref[...] + acc_ref[...], g2_ref[...], b2_ref[...])


def _post_ffn(x, y1, y2, y2_col, wo, g1, b1, w1, w2, g2, b2, *, tm, tf):
    n = x.shape[0]
    half = D_MODEL // 2
    row = pl.BlockSpec((tm, D_MODEL), lambda i, j: (i, 0))
    vec = pl.BlockSpec((1, D_MODEL), lambda i, j: (0, 0))
    return pl.pallas_call(
        _post_ffn_kernel,
        grid=(n // tm, D_FF // tf),
        in_specs=[row,
                  pl.BlockSpec((tm, half), lambda i, j: (i, 0)),
                  pl.BlockSpec((tm, half), lambda i, j: (i, y2_col)),
                  pl.BlockSpec((D_MODEL, D_MODEL), lambda i, j: (0, 0)),
                  vec, vec,
                  pl.BlockSpec((D_MODEL, tf), lambda i, j: (0, j)),
                  pl.BlockSpec((tf, D_MODEL), lambda i, j: (j, 0)),
                  vec, vec],
        out_specs=row,
        out_shape=jax.ShapeDtypeStruct((n, D_MODEL), F32),
        scratch_shapes=[pltpu.VMEM((tm, D_MODEL), F32), pltpu.VMEM((tm, D_MODEL), BF16),
                        pltpu.VMEM((tm, D_MODEL), F32)],
        compiler_params=pltpu.CompilerParams(dimension_semantics=("parallel", "arbitrary"),
                                             vmem_limit_bytes=VMEM_LIMIT),
        name="post_ffn",
    )(x, y1, y2, wo, g1, b1, w1, w2, g2, b2)


def _rope_inv_lanes():
    half = B_ROPE // 2
    inv = ROPE_THETA ** (-jnp.arange(half, dtype=F32) / half)
    lanes = jnp.zeros((1, HEAD_PAD), F32)
    lanes = lanes.at[0, B_NOPE:B_NOPE + half].set(inv)
    lanes = lanes.at[0, B_NOPE + half:B_NOPE + B_ROPE].set(inv)
    return lanes


def kernel(x, positions, ln1_g, ln1_b, ln2_g, ln2_b, w_in_even, a_w_s, a_b_s, a_ln_g, a_ln_b, b_q_norm, b_kv_norm, b_w_uq, b_w_ukv, w_out_even, w_in_odd, c_w_gate, c_b_gate, c_ln_g, c_ln_b, w_out_odd, w_ff1, w_ff2):
    bn, s, d = x.shape
    n = bn * s
    xf = x.reshape(n, d)
    pos = positions.reshape(n, 1).astype(jnp.int32)
    inv_lane = _rope_inv_lanes()
    causal = jnp.tril(jnp.ones((A_CHUNK, A_CHUNK), dtype=bool))
    vec = lambda a: a.reshape(1, -1)

    for layer in range(DEPTH):
        j = layer // 2
        if layer % 2 == 0:
            w = w_in_even[j]
            o1, o2, o3, o4 = A_WIDTH, 2 * A_WIDTH, 2 * A_WIDTH + B_Q_RANK, 2 * A_WIDTH + B_Q_RANK + B_KV_RANK
            wuv = w[:, :o2].astype(BF16)
            wcq = w[:, o2:o3].astype(BF16)
            wckv = w[:, o3:o4].astype(BF16)
            wkr = jnp.zeros((d, HEAD_PAD), F32).at[:, B_NOPE:B_NOPE + B_ROPE].set(w[:, o4:]).astype(BF16)
            wuq = b_w_uq[j].reshape(B_Q_RANK, B_HEADS, B_NOPE + B_ROPE)
            wuq = jnp.pad(wuq, ((0, 0), (0, 0), (0, HEAD_PAD - B_NOPE - B_ROPE)))
            wuq = wuq.reshape(B_Q_RANK, B_HEADS * HEAD_PAD).astype(BF16)
            wukv = b_w_ukv[j].reshape(B_KV_RANK, B_HEADS, B_NOPE + B_VDIM)
            wuk = jnp.pad(wukv[:, :, :B_NOPE], ((0, 0), (0, 0), (0, HEAD_PAD - B_NOPE)))
            wuk = wuk.reshape(B_KV_RANK, B_HEADS * HEAD_PAD).astype(BF16)
            wuvv = wukv[:, :, B_NOPE:].reshape(B_KV_RANK, B_HEADS * B_VDIM).astype(BF16)
            ws = jnp.where(causal[None], a_w_s[j], 0.0).astype(BF16)
            bs = jnp.broadcast_to(a_b_s[j][:, :, None], (A_GROUPS, A_CHUNK, A_GROUP_DIM))
            ya, qt, k, vt = _even_pre(xf, pos, inv_lane, wuv, wcq, wckv, wkr, ws, bs,
                                    a_ln_g[j], a_ln_b[j], vec(b_q_norm[j]), vec(b_kv_norm[j]),
                                    wuq, wuk, wuvv, tm=512)
            yb = _attn(qt, k, vt, batch=bn, seq=s)
            y1, y2, y2_col = ya, yb, 0
            wo = w_out_even[j].astype(BF16)
        else:
            w = w_in_odd[j]
            hk, hv = C_HEADS * C_DK, C_HEADS * C_DV
            wq = w[:, :hk].astype(BF16)
            wk = w[:, hk:2 * hk].astype(BF16)
            wv = w[:, 2 * hk:2 * hk + hv].astype(BF16)
            wg = w[:, 2 * hk + hv:2 * hk + 2 * hv].astype(BF16)
            wzg = jnp.zeros((d, 128), F32).at[:, :C_GATE_RANK].set(w[:, 2 * hk + 2 * hv:]).astype(BF16)
            wgate = jnp.zeros((128, hk), F32).at[:C_GATE_RANK].set(c_w_gate[j]).astype(BF16)
            q, k, v, sg, la = _odd_pre(xf, wq, wk, wv, wg, wzg, wgate, vec(c_b_gate[j]), tm=256)
            y = _gla(q, k, v, la, sg, vec(c_ln_g[j]), vec(c_ln_b[j]), batch=bn, seq=s)
            y1, y2, y2_col = y, y, 1
            wo = w_out_odd[j].astype(BF16)
        xf = _post_ffn(xf, y1, y2, y2_col, wo, vec(ln1_g[layer]), vec(ln1_b[layer]),
                       w_ff1[layer].astype(BF16), w_ff2[layer].astype(BF16),
                       vec(ln2_g[layer]), vec(ln2_b[layer]), tm=512, tf=1024)
    return xf.reshape(bn, s, d)
```

```python
import jax
import jax.numpy as jnp
from jax import lax
from jax.experimental import pallas as pl
from jax.experimental.pallas import tpu as pltpu

F32 = jnp.float32
BF16 = jnp.bfloat16

D_MODEL = 1024
DEPTH = 4

A_CHUNK = 128
A_GROUPS = 4
A_WIDTH = 512
A_GROUP_DIM = 128

B_HEADS = 8
B_NOPE = 64
B_ROPE = 32
B_VDIM = 64
B_Q_RANK = 384
B_KV_RANK = 256
ROPE_THETA = 10000.0
HEAD_PAD = 128
ATTN_SUM_ROWS = 16

C_HEADS = 4
C_DK = 128
C_DV = 256
C_GATE_RANK = 16
C_GATE_TAU = 16.0
GLA_CHUNK = 256
GLA_FINE_LEVELS = (1, 2, 4)

D_FF = 4 * D_MODEL
ALPHA = (2.0 * DEPTH) ** 0.25
LN_EPS = 1e-5
LOG2_E = 1.4426950408889634

VMEM_LIMIT = 56 * 1024 * 1024

TILES = {
    "attn": 512,
    "odd_pre": 256,
    "ffn_rows": 512,
    "ffn_cols": 1024,
}
FFN_ROW_BLOCK = 256


def _dot(a, b):
    return jnp.dot(a, b, preferred_element_type=F32)


def _dot_nt(a, b):
    return lax.dot_general(a, b, (((1,), (1,)), ((), ())), preferred_element_type=F32)


def _dot_tn(a, b):
    return lax.dot_general(a, b, (((0,), (0,)), ((), ())), preferred_element_type=F32)


def _ln(x, g, b):
    mu = jnp.mean(x, axis=-1, keepdims=True)
    xc = x - mu
    var = jnp.mean(xc * xc, axis=-1, keepdims=True)
    return xc * lax.rsqrt(var + LN_EPS) * g + b


def _rms(x, g):
    ms = jnp.mean(x * x, axis=-1, keepdims=True)
    return x * lax.rsqrt(ms + LN_EPS) * g


def _gelu(x):
    return 0.5 * x * (1.0 + jnp.tanh(0.7978845608028654 * (x + 0.044715 * (x * x * x))))


def _layer_spec(a, layer):
    zeros = (0,) * (a.ndim - 1)
    return pl.BlockSpec((None,) + a.shape[1:], lambda *_: (layer,) + zeros)


def _even_pre_kernel(x_ref, pos_ref, inv_ref, win_ref,
                     ws_ref, bs_ref, alng_ref, alnb_ref, gq_ref, gkv_ref,
                     wuq_ref, wuk_ref, wuvv_ref,
                     ya_ref, qt_ref, k_ref, vt_ref):
    tm = x_ref.shape[0]
    z = _dot(x_ref[...].astype(BF16), win_ref[...])
    o_cq = 2 * A_WIDTH
    o_ckv = o_cq + B_Q_RANK
    o_kr = o_ckv + B_KV_RANK

    zuv = z[:, :o_cq]
    for g in range(A_GROUPS):
        lo = g * A_GROUP_DIM
        u = _gelu(zuv[:, lo:lo + A_GROUP_DIM])
        vv = _gelu(zuv[:, A_WIDTH + lo:A_WIDTH + lo + A_GROUP_DIM])
        vn = _ln(vv, alng_ref[g:g + 1, :], alnb_ref[g:g + 1, :]).astype(BF16)
        for c in range(tm // A_CHUNK):
            r = c * A_CHUNK
            mixed = _dot(ws_ref[g], vn[r:r + A_CHUNK, :]) + bs_ref[g]
            ya_ref[r:r + A_CHUNK, lo:lo + A_GROUP_DIM] = (u[r:r + A_CHUNK, :] * mixed).astype(ya_ref.dtype)

    ang = pos_ref[...].astype(F32) * inv_ref[...]
    cos = jnp.cos(ang)
    sin = jnp.sin(ang)
    lane = lax.broadcasted_iota(jnp.int32, (1, HEAD_PAD), 1)
    half = B_ROPE // 2
    s1 = jnp.where((lane >= B_NOPE) & (lane < B_NOPE + half), -sin, 0.0)
    s2 = jnp.where((lane >= B_NOPE + half) & (lane < B_NOPE + B_ROPE), sin, 0.0)

    def rope(t):
        return t * cos + pltpu.roll(t, HEAD_PAD - half, 1) * s1 + pltpu.roll(t, half, 1) * s2

    scale = (B_NOPE + B_ROPE) ** -0.5 * LOG2_E
    cqn = _rms(z[:, o_cq:o_ckv], gq_ref[...]).astype(BF16)
    qf = _dot(cqn, wuq_ref[...])
    ckvn = _rms(z[:, o_ckv:o_kr], gkv_ref[...]).astype(BF16)
    kn = _dot(ckvn, wuk_ref[...])
    kr = rope(z[:, o_kr:o_kr + HEAD_PAD])
    for h in range(B_HEADS):
        lo = h * HEAD_PAD
        qh = rope(qf[:, lo:lo + HEAD_PAD]) * scale
        qt_ref[0, lo:lo + HEAD_PAD, :] = qh.T.astype(qt_ref.dtype)
        k_ref[:, lo:lo + HEAD_PAD] = (kn[:, lo:lo + HEAD_PAD] + kr).astype(k_ref.dtype)
    vt_ref[0] = _dot(ckvn, wuvv_ref[...]).T.astype(vt_ref.dtype)


def _even_pre(x, pos, inv_lane, *params, layer):
    n = x.shape[0]
    tm = TILES["attn"]
    row = lambda w: pl.BlockSpec((tm, w), lambda i: (i, 0))
    hp = B_HEADS * HEAD_PAD
    return pl.pallas_call(
        _even_pre_kernel,
        grid=(n // tm,),
        in_specs=[row(D_MODEL), row(1), pl.BlockSpec(inv_lane.shape, lambda i: (0, 0))]
                 + [_layer_spec(a, layer) for a in params],
        out_specs=[row(A_WIDTH),
                   pl.BlockSpec((1, hp, tm), lambda i: (i, 0, 0)),
                   row(hp),
                   pl.BlockSpec((1, B_HEADS * B_VDIM, tm), lambda i: (i, 0, 0))],
        out_shape=[jax.ShapeDtypeStruct((n, A_WIDTH), BF16),
                   jax.ShapeDtypeStruct((n // tm, hp, tm), BF16),
                   jax.ShapeDtypeStruct((n, hp), BF16),
                   jax.ShapeDtypeStruct((n // tm, B_HEADS * B_VDIM, tm), BF16)],
        compiler_params=pltpu.CompilerParams(dimension_semantics=("parallel",),
                                             vmem_limit_bytes=VMEM_LIMIT),
        name="even_pre",
    )(x, pos, inv_lane, *params)


def _attn_kernel(qt_ref, k_ref, vt_ref, o_ref, sa_ref, sb_ref):
    t = o_ref.shape[0]
    qi = pl.program_id(2)

    def scores(j, dst):
        start = pl.multiple_of(j * t, t)
        for hh in range(2):
            lo = hh * HEAD_PAD
            dst[hh] = _dot(k_ref[pl.ds(start, t), lo:lo + HEAD_PAD], qt_ref[0, lo:lo + HEAD_PAD, :])

    ones = jnp.ones((ATTN_SUM_ROWS, t), BF16)

    def update(j, src, stats, masked):
        vt = vt_ref[j]
        new = []
        for hh in range(2):
            m, acc = stats[hh]
            s = src[hh]
            if masked:
                key_ids = lax.broadcasted_iota(jnp.int32, (t, t), 0)
                qry_ids = lax.broadcasted_iota(jnp.int32, (t, t), 1)
                s = jnp.where(key_ids <= qry_ids, s, -jnp.inf)
            m_new = jnp.maximum(m, jnp.max(s, axis=0, keepdims=True))
            p = jnp.exp2(s - m_new).astype(BF16)
            a = jnp.exp2(m - m_new)
            v_ext = jnp.concatenate([vt[hh * B_VDIM:(hh + 1) * B_VDIM, :], ones], axis=0)
            acc = a * acc + _dot(v_ext, p)
            new.append((m_new, acc))
        return tuple(new)

    def two_blocks(i, stats):
        j = 2 * i
        scores(j + 1, sb_ref)
        stats = update(j, sa_ref, stats, False)
        scores(j + 2, sa_ref)
        return update(j + 1, sb_ref, stats, False)

    def tail_odd(stats):
        scores(qi, sb_ref)
        stats = update(qi - 1, sa_ref, stats, False)
        return update(qi, sb_ref, stats, True)

    def tail_even(stats):
        return update(qi, sa_ref, stats, True)

    scores(0, sa_ref)
    init1 = (jnp.full((1, t), -jnp.inf, F32), jnp.zeros((B_VDIM + ATTN_SUM_ROWS, t), F32))
    def four_blocks(i, stats):
        return two_blocks(2 * i + 1, two_blocks(2 * i, stats))

    stats = lax.fori_loop(0, qi // 4, four_blocks, (init1, init1))
    stats = lax.fori_loop(2 * (qi // 4), qi // 2, two_blocks, stats)
    stats = lax.cond(qi % 2 == 1, tail_odd, tail_even, stats)
    out_t = jnp.concatenate([acc[:B_VDIM] / acc[B_VDIM:B_VDIM + 1] for (_, acc) in stats], axis=0)
    o_ref[...] = out_t.T.astype(o_ref.dtype)


def _attn(qt, k, vt, *, batch, seq):
    t = qt.shape[2]
    n = k.shape[0]
    nq = seq // t
    return pl.pallas_call(
        _attn_kernel,
        grid=(batch, B_HEADS // 2, nq),
        in_specs=[pl.BlockSpec((1, 2 * HEAD_PAD, t), lambda b, h, i: (b * nq + i, h, 0)),
                  pl.BlockSpec((seq, 2 * HEAD_PAD), lambda b, h, i: (b, h)),
                  pl.BlockSpec((nq, 2 * B_VDIM, t), lambda b, h, i: (b, h, 0))],
        out_specs=pl.BlockSpec((t, 2 * B_VDIM), lambda b, h, i: (b * nq + i, h)),
        out_shape=jax.ShapeDtypeStruct((n, B_HEADS * B_VDIM), BF16),
        scratch_shapes=[pltpu.VMEM((2, t, t), F32), pltpu.VMEM((2, t, t), F32)],
        compiler_params=pltpu.CompilerParams(dimension_semantics=("parallel", "parallel", "arbitrary"),
                                             vmem_limit_bytes=VMEM_LIMIT),
        name="attn",
    )(qt, k, vt)


def _odd_pre_kernel(x_ref, win_ref, wzg_ref, wgate_ref, bgate_ref,
                    q_ref, k_ref, v_ref, sg_ref, la_ref):
    xb = x_ref[...].astype(BF16)
    hk, hv = C_HEADS * C_DK, C_HEADS * C_DV
    z = _dot(xb, win_ref[...])
    q_ref[...] = z[:, :hk] * (C_DK ** -0.5)
    k_ref[...] = z[:, hk:2 * hk]
    v_ref[...] = z[:, 2 * hk:2 * hk + hv].astype(v_ref.dtype)
    g = z[:, 2 * hk + hv:]
    sg_ref[...] = g * jax.nn.sigmoid(g)
    zg = _dot(xb, wzg_ref[...]).astype(BF16)
    logits = _dot(zg, wgate_ref[...]) + bgate_ref[...]
    la_ref[...] = -(jnp.maximum(-logits, 0.0) + jnp.log1p(jnp.exp(-jnp.abs(logits)))) * (1.0 / C_GATE_TAU)


def _odd_pre(x, *params, layer):
    n = x.shape[0]
    tm = TILES["odd_pre"]
    row = lambda w: pl.BlockSpec((tm, w), lambda i: (i, 0))
    hk, hv = C_HEADS * C_DK, C_HEADS * C_DV
    return pl.pallas_call(
        _odd_pre_kernel,
        grid=(n // tm,),
        in_specs=[row(D_MODEL)] + [_layer_spec(a, layer) for a in params],
        out_specs=[row(hk), row(hk), row(hv), row(hv), row(hk)],
        out_shape=[jax.ShapeDtypeStruct((n, hk), F32),
                   jax.ShapeDtypeStruct((n, hk), F32),
                   jax.ShapeDtypeStruct((n, hv), BF16),
                   jax.ShapeDtypeStruct((n, hv), F32),
                   jax.ShapeDtypeStruct((n, hk), F32)],
        compiler_params=pltpu.CompilerParams(dimension_semantics=("parallel",),
                                             vmem_limit_bytes=VMEM_LIMIT),
        name="odd_pre",
    )(x, *params)


def _gla_level_matrix():
    c = GLA_CHUNK
    t = jnp.arange(c)[:, None]
    u = jnp.arange(c)[None, :]
    mats = []
    for m in GLA_FINE_LEVELS:
        r = t % (2 * m)
        p = t - r + m - 1
        upper = (r >= m) & (u > p) & (u <= t)
        lower = (r < m) & (u > t) & (u <= p)
        mats.append(upper | lower)
    return jnp.concatenate(mats, axis=0).astype(BF16)


def _split_bf16(x, pieces):
    out = []
    for _ in range(pieces - 1):
        hi = x.astype(BF16)
        out.append(hi)
        x = x - hi.astype(F32)
    out.append(x.astype(BF16))
    return out


def _gla_kernel(q_ref, k_ref, v_ref, la_ref, sg_ref, lng_ref, lnb_ref, tri_ref, lvl_ref,
                y_ref, st_ref, a_scr):
    c = GLA_CHUNK
    hk = C_HEADS * C_DK

    @pl.when(pl.program_id(1) == 0)
    def _():
        st_ref[...] = jnp.zeros_like(st_ref)

    q = q_ref[...]
    k = k_ref[...]
    g = _split_bf16(la_ref[...], 3)
    tri = tri_ref[...]
    b = _dot(tri, g[0]) + _dot(tri, g[1]) + _dot(tri, g[2])
    fine = _dot(lvl_ref[...], g[0]) + _dot(lvl_ref[...], g[1])

    row = lax.broadcasted_iota(jnp.int32, (c, 1), 0)
    pair = lax.broadcasted_iota(jnp.int32, (c, c), 0) ^ lax.broadcasted_iota(jnp.int32, (c, c), 1)

    qb = q.astype(BF16)
    kb = k.astype(BF16)
    for h in range(C_HEADS):
        hs = slice(h * C_DK, (h + 1) * C_DK)
        a_scr[h] = jnp.where(pair == 0, _dot_nt(qb[:, hs], kb[:, hs]), 0.0)

    m = 1
    while m < c:
        if m in GLA_FINE_LEVELS:
            i = GLA_FINE_LEVELS.index(m)
            d = fine[i * c:(i + 1) * c, :]
        else:
            b3 = b.reshape(c // (2 * m), 2 * m, hk)
            d3 = b3 - b3[:, m - 1:m, :]
            sub = lax.broadcasted_iota(jnp.int32, (1, 2 * m, 1), 1)
            d = jnp.where(sub >= m, d3, -d3).reshape(c, hk)
        e = jnp.exp(d)
        upper = (row & m) != 0
        qe = jnp.where(upper, q * e, 0.0).astype(BF16)
        ke = jnp.where(upper, 0.0, k * e).astype(BF16)
        level = (pair >= m) & (pair < 2 * m)
        for h in range(C_HEADS):
            hs = slice(h * C_DK, (h + 1) * C_DK)
            a_scr[h] = jnp.where(level, _dot_nt(qe[:, hs], ke[:, hs]), a_scr[h])
        m *= 2

    b_last = b[c - 1:c, :]
    qs = (q * jnp.exp(b)).astype(BF16)
    kd = (k * jnp.exp(b_last - b)).astype(BF16)
    decay = jnp.exp(b_last)
    for h in range(C_HEADS):
        hs = slice(h * C_DK, (h + 1) * C_DK)
        vs = slice(h * C_DV, (h + 1) * C_DV)
        v = v_ref[:, vs]
        st = st_ref[h]
        o = _dot(a_scr[h].astype(BF16), v) + _dot_nt(qs[:, hs], st.astype(BF16))
        st_ref[h] = st * decay[:, hs] + _dot_tn(v, kd[:, hs])
        on = _ln(o, lng_ref[...], lnb_ref[...])
        y_ref[:, vs] = (on * sg_ref[:, vs]).astype(y_ref.dtype)


def _gla(q, k, v, la, sg, lng, lnb, *, batch, seq, layer):
    n = q.shape[0]
    c = GLA_CHUNK
    nt = seq // c
    hk, hv = C_HEADS * C_DK, C_HEADS * C_DV
    tri = jnp.tril(jnp.ones((c, c), F32)).astype(BF16)
    lvl = _gla_level_matrix()
    kblk = pl.BlockSpec((c, hk), lambda b, i: (b * nt + i, 0))
    vblk = pl.BlockSpec((c, hv), lambda b, i: (b * nt + i, 0))
    full2 = lambda a: pl.BlockSpec(a.shape, lambda b, i: (0, 0))
    return pl.pallas_call(
        _gla_kernel,
        grid=(batch, nt),
        in_specs=[kblk, kblk, vblk, kblk, vblk, _layer_spec(lng, layer), _layer_spec(lnb, layer),
                  full2(tri), full2(lvl)],
        out_specs=vblk,
        out_shape=jax.ShapeDtypeStruct((n, hv), BF16),
        scratch_shapes=[pltpu.VMEM((C_HEADS, C_DV, C_DK), F32), pltpu.VMEM((C_HEADS, c, c), F32)],
        compiler_params=pltpu.CompilerParams(dimension_semantics=("parallel", "arbitrary"),
                                             vmem_limit_bytes=VMEM_LIMIT),
        name="gla",
    )(q, k, v, la, sg, lng, lnb, tri, lvl)


def _post_ffn_kernel(x_ref, y1_ref, y2_ref, wo_ref, g1_ref, b1_ref, w1_ref, w2_ref, g2_ref, b2_ref,
                     o_ref, x1_ref, x1b_ref, acc_ref):
    j = pl.program_id(1)
    last = pl.num_programs(1) - 1
    half = y1_ref.shape[1]
    tm = x_ref.shape[0]
    blocks = [slice(r, r + FFN_ROW_BLOCK) for r in range(0, tm, FFN_ROW_BLOCK)]

    def mlp(x1b):
        h = _dot(x1b, w1_ref[...])
        return _dot(jnp.square(jnp.maximum(h, 0.0)).astype(BF16), w2_ref[...])

    @pl.when(j == 0)
    def _():
        for rows in blocks:
            y = _dot(y1_ref[rows, :], wo_ref[0:half, :]) + _dot(y2_ref[rows, :], wo_ref[half:2 * half, :])
            x1 = _ln(ALPHA * x_ref[rows, :] + y, g1_ref[...], b1_ref[...])
            x1_ref[rows, :] = x1
            x1b = x1.astype(BF16)
            x1b_ref[rows, :] = x1b
            acc_ref[rows, :] = mlp(x1b)

    @pl.when((j > 0) & (j < last))
    def _():
        acc_ref[...] += mlp(x1b_ref[...])

    @pl.when(j == last)
    def _():
        for rows in blocks:
            acc = acc_ref[rows, :] + mlp(x1b_ref[rows, :])
            o_ref[rows, :] = _ln(ALPHA * x1_ref[rows, :] + acc, g2_ref[...], b2_ref[...])


def _post_ffn(x, y1, y2, y2_col, wo, g1, b1, w1, w2, g2, b2, *, layer):
    n = x.shape[0]
    tm, tf = TILES["ffn_rows"], TILES["ffn_cols"]
    assert D_FF // tf >= 2
    half = D_MODEL // 2
    row = pl.BlockSpec((tm, D_MODEL), lambda i, j: (i, 0))
    vec = pl.BlockSpec((None, 1, D_MODEL), lambda i, j: (layer, 0, 0))
    return pl.pallas_call(
        _post_ffn_kernel,
        grid=(n // tm, D_FF // tf),
        in_specs=[row,
                  pl.BlockSpec((tm, half), lambda i, j: (i, 0)),
                  pl.BlockSpec((tm, half), lambda i, j: (i, y2_col)),
                  pl.BlockSpec((None, D_MODEL, D_MODEL), lambda i, j: (layer, 0, 0)),
                  vec, vec,
                  pl.BlockSpec((None, D_MODEL, tf), lambda i, j: (layer, 0, j)),
                  pl.BlockSpec((None, tf, D_MODEL), lambda i, j: (layer, j, 0)),
                  vec, vec],
        out_specs=row,
        out_shape=jax.ShapeDtypeStruct((n, D_MODEL), F32),
        scratch_shapes=[pltpu.VMEM((tm, D_MODEL), F32), pltpu.VMEM((tm, D_MODEL), BF16),
                        pltpu.VMEM((tm, D_MODEL), F32)],
        compiler_params=pltpu.CompilerParams(dimension_semantics=("parallel", "arbitrary"),
                                             vmem_limit_bytes=VMEM_LIMIT),
        name="post_ffn",
    )(x, y1, y2, wo, g1, b1, w1, w2, g2, b2)


def _rope_inv_lanes():
    half = B_ROPE // 2
    inv = ROPE_THETA ** (-jnp.arange(half, dtype=F32) / half)
    lanes = jnp.zeros((1, HEAD_PAD), F32)
    lanes = lanes.at[0, B_NOPE:B_NOPE + half].set(inv)
    lanes = lanes.at[0, B_NOPE + half:B_NOPE + B_ROPE].set(inv)
    return lanes


def kernel(x, positions, ln1_g, ln1_b, ln2_g, ln2_b, w_in_even, a_w_s, a_b_s, a_ln_g, a_ln_b, b_q_norm, b_kv_norm, b_w_uq, b_w_ukv, w_out_even, w_in_odd, c_w_gate, c_b_gate, c_ln_g, c_ln_b, w_out_odd, w_ff1, w_ff2):
    bn, s, d = x.shape
    n = bn * s
    xf = x.reshape(n, d)
    pos = positions.reshape(n, 1).astype(jnp.int32)
    inv_lane = _rope_inv_lanes()
    n_even = w_in_even.shape[0]
    vec = lambda a: a.reshape(a.shape[0], 1, -1)

    o_kr = 2 * A_WIDTH + B_Q_RANK + B_KV_RANK
    pad = lambda width: jnp.zeros((n_even, d, width), F32)
    win_even = jnp.concatenate([w_in_even[:, :, :o_kr], pad(B_NOPE), w_in_even[:, :, o_kr:],
                                pad(HEAD_PAD - B_NOPE - B_ROPE)], axis=-1).astype(BF16)
    wuq = b_w_uq.reshape(n_even, B_Q_RANK, B_HEADS, B_NOPE + B_ROPE)
    wuq = jnp.pad(wuq, ((0, 0), (0, 0), (0, 0), (0, HEAD_PAD - B_NOPE - B_ROPE)))
    wuq = wuq.reshape(n_even, B_Q_RANK, B_HEADS * HEAD_PAD).astype(BF16)
    wukv = b_w_ukv.reshape(n_even, B_KV_RANK, B_HEADS, B_NOPE + B_VDIM)
    wuk = jnp.pad(wukv[..., :B_NOPE], ((0, 0), (0, 0), (0, 0), (0, HEAD_PAD - B_NOPE)))
    wuk = wuk.reshape(n_even, B_KV_RANK, B_HEADS * HEAD_PAD).astype(BF16)
    wuvv = wukv[..., B_NOPE:].reshape(n_even, B_KV_RANK, B_HEADS * B_VDIM).astype(BF16)
    causal = jnp.tril(jnp.ones((A_CHUNK, A_CHUNK), dtype=bool))
    ws = jnp.where(causal, a_w_s, 0.0).astype(BF16)
    bs = jnp.broadcast_to(a_b_s[..., None], a_b_s.shape + (A_GROUP_DIM,))
    even_params = (win_even, ws, bs, a_ln_g, a_ln_b, vec(b_q_norm), vec(b_kv_norm), wuq, wuk, wuvv)

    hk, hv = C_HEADS * C_DK, C_HEADS * C_DV
    win_odd = w_in_odd[:, :, :2 * hk + 2 * hv].astype(BF16)
    wzg = jnp.pad(w_in_odd[:, :, 2 * hk + 2 * hv:], ((0, 0), (0, 0), (0, 128 - C_GATE_RANK))).astype(BF16)
    wgate = jnp.pad(c_w_gate, ((0, 0), (0, 128 - C_GATE_RANK), (0, 0))).astype(BF16)
    odd_params = (win_odd, wzg, wgate, vec(c_b_gate))
    c_lng, c_lnb = vec(c_ln_g), vec(c_ln_b)

    w_out = jnp.stack([(w_out_even if layer % 2 == 0 else w_out_odd)[layer // 2]
                       for layer in range(DEPTH)]).astype(BF16)
    w1, w2 = w_ff1.astype(BF16), w_ff2.astype(BF16)
    g1, b1, g2, b2 = vec(ln1_g), vec(ln1_b), vec(ln2_g), vec(ln2_b)

    for layer in range(DEPTH):
        j = layer // 2
        if layer % 2 == 0:
            ya, qt, k, vt = _even_pre(xf, pos, inv_lane, *even_params, layer=j)
            yb = _attn(qt, k, vt, batch=bn, seq=s)
            y1, y2, y2_col = ya, yb, 0
        else:
            q, k, v, sg, la = _odd_pre(xf, *odd_params, layer=j)
            y = _gla(q, k, v, la, sg, c_lng, c_lnb, batch=bn, seq=s, layer=j)
            y1, y2, y2_col = y, y, 1
        xf = _post_ffn(xf, y1, y2, y2_col, w_out, g1, b1, w1, w2, g2, b2, layer=layer)
    return xf.reshape(bn, s, d)
```

```python
import jax
import jax.numpy as jnp
from jax import lax
from jax.experimental import pallas as pl
from jax.experimental.pallas import tpu as pltpu

F32 = jnp.float32
BF16 = jnp.bfloat16

D_MODEL = 1024
DEPTH = 4

A_CHUNK = 128
A_GROUPS = 4
A_WIDTH = 512
A_GROUP_DIM = 128

B_HEADS = 8
B_NOPE = 64
B_ROPE = 32
B_VDIM = 64
B_Q_RANK = 384
B_KV_RANK = 256
ROPE_THETA = 10000.0
HEAD_PAD = 128
ATTN_SUM_ROWS = 16

C_HEADS = 4
C_DK = 128
C_DV = 256
C_GATE_RANK = 16
C_GATE_TAU = 16.0
GLA_CHUNK = 256
GLA_FINE_LEVELS = (1, 2, 4)

D_FF = 4 * D_MODEL
ALPHA = (2.0 * DEPTH) ** 0.25
LN_EPS = 1e-5
LOG2_E = 1.4426950408889634

VMEM_LIMIT = 56 * 1024 * 1024

TILES = {
    "attn": 512,
    "odd_pre": 256,
    "rope": 2048,
    "ffn_rows": 1024,
    "ffn_cols": 1024,
}
FFN_ROW_BLOCK = 256


def _dot(a, b):
    return jnp.dot(a, b, preferred_element_type=F32)


def _dot_nt(a, b):
    return lax.dot_general(a, b, (((1,), (1,)), ((), ())), preferred_element_type=F32)


def _dot_tn(a, b):
    return lax.dot_general(a, b, (((0,), (0,)), ((), ())), preferred_element_type=F32)


def _ln(x, g, b):
    mu = jnp.mean(x, axis=-1, keepdims=True)
    xc = x - mu
    var = jnp.mean(xc * xc, axis=-1, keepdims=True)
    return xc * lax.rsqrt(var + LN_EPS) * g + b


def _rms(x, g):
    ms = jnp.mean(x * x, axis=-1, keepdims=True)
    return x * lax.rsqrt(ms + LN_EPS) * g


def _gelu(x):
    return 0.5 * x * (1.0 + jnp.tanh(0.7978845608028654 * (x + 0.044715 * (x * x * x))))


def _layer_spec(a, layer):
    zeros = (0,) * (a.ndim - 1)
    return pl.BlockSpec((None,) + a.shape[1:], lambda *_: (layer,) + zeros)


def _rope_table_kernel(pos_ref, inv_ref, cos_ref, sin_ref):
    ang = pos_ref[...].astype(F32) * inv_ref[...]
    cos_ref[...] = jnp.cos(ang)
    sin_ref[...] = jnp.sin(ang)


def _rope_tables(pos, inv_lane):
    n = pos.shape[0]
    tm = TILES["rope"]
    row = lambda w: pl.BlockSpec((tm, w), lambda i: (i, 0))
    table = jax.ShapeDtypeStruct((n, HEAD_PAD), F32)
    return pl.pallas_call(
        _rope_table_kernel,
        grid=(n // tm,),
        in_specs=[row(1), pl.BlockSpec(inv_lane.shape, lambda i: (0, 0))],
        out_specs=[row(HEAD_PAD), row(HEAD_PAD)],
        out_shape=[table, table],
        compiler_params=pltpu.CompilerParams(dimension_semantics=("parallel",)),
        name="rope_tables",
    )(pos, inv_lane)


def _even_pre_kernel(x_ref, cos_ref, sin_ref, win_ref,
                     ws_ref, bs_ref, alng_ref, alnb_ref, gq_ref, gkv_ref,
                     wuq_ref, wuk_ref, wuvv_ref,
                     ya_ref, qt_ref, k_ref, vt_ref):
    tm = x_ref.shape[0]
    z = _dot(x_ref[...].astype(BF16), win_ref[...])
    o_cq = 2 * A_WIDTH
    o_ckv = o_cq + B_Q_RANK
    o_kr = o_ckv + B_KV_RANK

    zuv = z[:, :o_cq]
    for g in range(A_GROUPS):
        lo = g * A_GROUP_DIM
        u = _gelu(zuv[:, lo:lo + A_GROUP_DIM])
        vv = _gelu(zuv[:, A_WIDTH + lo:A_WIDTH + lo + A_GROUP_DIM])
        vn = _ln(vv, alng_ref[g:g + 1, :], alnb_ref[g:g + 1, :]).astype(BF16)
        for c in range(tm // A_CHUNK):
            r = c * A_CHUNK
            mixed = _dot(ws_ref[g], vn[r:r + A_CHUNK, :]) + bs_ref[g]
            ya_ref[r:r + A_CHUNK, lo:lo + A_GROUP_DIM] = (u[r:r + A_CHUNK, :] * mixed).astype(ya_ref.dtype)

    cos = cos_ref[...]
    sin = sin_ref[...]
    lane = lax.broadcasted_iota(jnp.int32, (1, HEAD_PAD), 1)
    half = B_ROPE // 2
    s1 = jnp.where((lane >= B_NOPE) & (lane < B_NOPE + half), -sin, 0.0)
    s2 = jnp.where((lane >= B_NOPE + half) & (lane < B_NOPE + B_ROPE), sin, 0.0)

    def rope(t):
        return t * cos + pltpu.roll(t, HEAD_PAD - half, 1) * s1 + pltpu.roll(t, half, 1) * s2

    scale = (B_NOPE + B_ROPE) ** -0.5 * LOG2_E
    cqn = _rms(z[:, o_cq:o_ckv], gq_ref[...]).astype(BF16)
    qf = _dot(cqn, wuq_ref[...])
    ckvn = _rms(z[:, o_ckv:o_kr], gkv_ref[...]).astype(BF16)
    kn = _dot(ckvn, wuk_ref[...])
    kr = rope(z[:, o_kr:o_kr + HEAD_PAD])
    for h in range(B_HEADS):
        lo = h * HEAD_PAD
        qh = rope(qf[:, lo:lo + HEAD_PAD]) * scale
        qt_ref[0, lo:lo + HEAD_PAD, :] = qh.T.astype(qt_ref.dtype)
        k_ref[:, lo:lo + HEAD_PAD] = (kn[:, lo:lo + HEAD_PAD] + kr).astype(k_ref.dtype)
    vt_ref[0] = _dot(ckvn, wuvv_ref[...]).T.astype(vt_ref.dtype)


def _even_pre(x, cos, sin, *params, layer):
    n = x.shape[0]
    tm = TILES["attn"]
    row = lambda w: pl.BlockSpec((tm, w), lambda i: (i, 0))
    hp = B_HEADS * HEAD_PAD
    return pl.pallas_call(
        _even_pre_kernel,
        grid=(n // tm,),
        in_specs=[row(D_MODEL), row(HEAD_PAD), row(HEAD_PAD)] + [_layer_spec(a, layer) for a in params],
        out_specs=[row(A_WIDTH),
                   pl.BlockSpec((1, hp, tm), lambda i: (i, 0, 0)),
                   row(hp),
                   pl.BlockSpec((1, B_HEADS * B_VDIM, tm), lambda i: (i, 0, 0))],
        out_shape=[jax.ShapeDtypeStruct((n, A_WIDTH), BF16),
                   jax.ShapeDtypeStruct((n // tm, hp, tm), BF16),
                   jax.ShapeDtypeStruct((n, hp), BF16),
                   jax.ShapeDtypeStruct((n // tm, B_HEADS * B_VDIM, tm), BF16)],
        compiler_params=pltpu.CompilerParams(dimension_semantics=("parallel",),
                                             vmem_limit_bytes=VMEM_LIMIT),
        name="even_pre",
    )(x, cos, sin, *params)


def _attn_kernel(qt_ref, k_ref, vt_ref, o_ref, sa_ref, sb_ref):
    t = o_ref.shape[0]
    qi = pl.program_id(2)

    def scores(j, dst):
        start = pl.multiple_of(j * t, t)
        for hh in range(2):
            lo = hh * HEAD_PAD
            dst[hh] = _dot(k_ref[pl.ds(start, t), lo:lo + HEAD_PAD], qt_ref[0, lo:lo + HEAD_PAD, :])

    ones = jnp.ones((ATTN_SUM_ROWS, t), BF16)

    def update(j, src, stats, masked):
        vt = vt_ref[j]
        new = []
        for hh in range(2):
            m, acc = stats[hh]
            s = src[hh]
            if masked:
                key_ids = lax.broadcasted_iota(jnp.int32, (t, t), 0)
                qry_ids = lax.broadcasted_iota(jnp.int32, (t, t), 1)
                s = jnp.where(key_ids <= qry_ids, s, -jnp.inf)
            m_new = jnp.maximum(m, jnp.max(s, axis=0, keepdims=True))
            p = jnp.exp2(s - m_new).astype(BF16)
            a = jnp.exp2(m - m_new)
            v_ext = jnp.concatenate([vt[hh * B_VDIM:(hh + 1) * B_VDIM, :], ones], axis=0)
            acc = a * acc + _dot(v_ext, p)
            new.append((m_new, acc))
        return tuple(new)

    def two_blocks(i, stats):
        j = 2 * i
        scores(j + 1, sb_ref)
        stats = update(j, sa_ref, stats, False)
        scores(j + 2, sa_ref)
        return update(j + 1, sb_ref, stats, False)

    def tail_odd(stats):
        scores(qi, sb_ref)
        stats = update(qi - 1, sa_ref, stats, False)
        return update(qi, sb_ref, stats, True)

    def tail_even(stats):
        return update(qi, sa_ref, stats, True)

    scores(0, sa_ref)
    init1 = (jnp.full((1, t), -jnp.inf, F32), jnp.zeros((B_VDIM + ATTN_SUM_ROWS, t), F32))
    def four_blocks(i, stats):
        return two_blocks(2 * i + 1, two_blocks(2 * i, stats))

    stats = lax.fori_loop(0, qi // 4, four_blocks, (init1, init1))
    stats = lax.fori_loop(2 * (qi // 4), qi // 2, two_blocks, stats)
    stats = lax.cond(qi % 2 == 1, tail_odd, tail_even, stats)
    out_t = jnp.concatenate([acc[:B_VDIM] / acc[B_VDIM:B_VDIM + 1] for (_, acc) in stats], axis=0)
    o_ref[...] = out_t.T.astype(o_ref.dtype)


def _attn(qt, k, vt, *, batch, seq):
    t = qt.shape[2]
    n = k.shape[0]
    nq = seq // t
    return pl.pallas_call(
        _attn_kernel,
        grid=(batch, B_HEADS // 2, nq),
        in_specs=[pl.BlockSpec((1, 2 * HEAD_PAD, t), lambda b, h, i: (b * nq + i, h, 0)),
                  pl.BlockSpec((seq, 2 * HEAD_PAD), lambda b, h, i: (b, h)),
                  pl.BlockSpec((nq, 2 * B_VDIM, t), lambda b, h, i: (b, h, 0))],
        out_specs=pl.BlockSpec((t, 2 * B_VDIM), lambda b, h, i: (b * nq + i, h)),
        out_shape=jax.ShapeDtypeStruct((n, B_HEADS * B_VDIM), BF16),
        scratch_shapes=[pltpu.VMEM((2, t, t), F32), pltpu.VMEM((2, t, t), F32)],
        compiler_params=pltpu.CompilerParams(dimension_semantics=("parallel", "parallel", "arbitrary"),
                                             vmem_limit_bytes=VMEM_LIMIT),
        name="attn",
    )(qt, k, vt)


def _odd_pre_kernel(x_ref, win_ref, wzg_ref, wgate_ref, bgate_ref,
                    q_ref, k_ref, v_ref, sg_ref, la_ref):
    xb = x_ref[...].astype(BF16)
    hk, hv = C_HEADS * C_DK, C_HEADS * C_DV
    z = _dot(xb, win_ref[...])
    q_ref[...] = z[:, :hk] * (C_DK ** -0.5)
    k_ref[...] = z[:, hk:2 * hk]
    v_ref[...] = z[:, 2 * hk:2 * hk + hv].astype(v_ref.dtype)
    g = z[:, 2 * hk + hv:]
    sg_ref[...] = g * jax.nn.sigmoid(g)
    zg = _dot(xb, wzg_ref[...]).astype(BF16)
    logits = _dot(zg, wgate_ref[...]) + bgate_ref[...]
    la_ref[...] = -(jnp.maximum(-logits, 0.0) + jnp.log1p(jnp.exp(-jnp.abs(logits)))) * (1.0 / C_GATE_TAU)


def _odd_pre(x, *params, layer):
    n = x.shape[0]
    tm = TILES["odd_pre"]
    row = lambda w: pl.BlockSpec((tm, w), lambda i: (i, 0))
    hk, hv = C_HEADS * C_DK, C_HEADS * C_DV
    return pl.pallas_call(
        _odd_pre_kernel,
        grid=(n // tm,),
        in_specs=[row(D_MODEL)] + [_layer_spec(a, layer) for a in params],
        out_specs=[row(hk), row(hk), row(hv), row(hv), row(hk)],
        out_shape=[jax.ShapeDtypeStruct((n, hk), F32),
                   jax.ShapeDtypeStruct((n, hk), F32),
                   jax.ShapeDtypeStruct((n, hv), BF16),
                   jax.ShapeDtypeStruct((n, hv), F32),
                   jax.ShapeDtypeStruct((n, hk), F32)],
        compiler_params=pltpu.CompilerParams(dimension_semantics=("parallel",),
                                             vmem_limit_bytes=VMEM_LIMIT),
        name="odd_pre",
    )(x, *params)


def _gla_level_matrix():
    c = GLA_CHUNK
    t = jnp.arange(c)[:, None]
    u = jnp.arange(c)[None, :]
    mats = []
    for m in GLA_FINE_LEVELS:
        r = t % (2 * m)
        p = t - r + m - 1
        upper = (r >= m) & (u > p) & (u <= t)
        lower = (r < m) & (u > t) & (u <= p)
        mats.append(upper | lower)
    return jnp.concatenate(mats, axis=0).astype(BF16)


def _split_bf16(x, pieces):
    out = []
    for _ in range(pieces - 1):
        hi = x.astype(BF16)
        out.append(hi)
        x = x - hi.astype(F32)
    out.append(x.astype(BF16))
    return out


def _gla_kernel(q_ref, k_ref, v_ref, la_ref, sg_ref, lng_ref, lnb_ref, tri_ref, lvl_ref,
                y_ref, st_ref, a_scr):
    c = GLA_CHUNK
    hk = C_HEADS * C_DK

    @pl.when(pl.program_id(1) == 0)
    def _():
        st_ref[...] = jnp.zeros_like(st_ref)

    q = q_ref[...]
    k = k_ref[...]
    g = _split_bf16(la_ref[...], 3)
    tri = tri_ref[...]
    b = _dot(tri, g[0]) + _dot(tri, g[1]) + _dot(tri, g[2])
    fine = _dot(lvl_ref[...], g[0]) + _dot(lvl_ref[...], g[1])

    row = lax.broadcasted_iota(jnp.int32, (c, 1), 0)
    pair = lax.broadcasted_iota(jnp.int32, (c, c), 0) ^ lax.broadcasted_iota(jnp.int32, (c, c), 1)

    qb = q.astype(BF16)
    kb = k.astype(BF16)
    for h in range(C_HEADS):
        hs = slice(h * C_DK, (h + 1) * C_DK)
        a_scr[h] = jnp.where(pair == 0, _dot_nt(qb[:, hs], kb[:, hs]), 0.0)

    m = 1
    while m < c:
        if m in GLA_FINE_LEVELS:
            i = GLA_FINE_LEVELS.index(m)
            d = fine[i * c:(i + 1) * c, :]
        else:
            b3 = b.reshape(c // (2 * m), 2 * m, hk)
            d3 = b3 - b3[:, m - 1:m, :]
            sub = lax.broadcasted_iota(jnp.int32, (1, 2 * m, 1), 1)
            d = jnp.where(sub >= m, d3, -d3).reshape(c, hk)
        e = jnp.exp(d)
        upper = (row & m) != 0
        qe = jnp.where(upper, q * e, 0.0).astype(BF16)
        ke = jnp.where(upper, 0.0, k * e).astype(BF16)
        level = (pair >= m) & (pair < 2 * m)
        for h in range(C_HEADS):
            hs = slice(h * C_DK, (h + 1) * C_DK)
            a_scr[h] = jnp.where(level, _dot_nt(qe[:, hs], ke[:, hs]), a_scr[h])
        m *= 2

    b_last = b[c - 1:c, :]
    qs = (q * jnp.exp(b)).astype(BF16)
    kd = (k * jnp.exp(b_last - b)).astype(BF16)
    decay = jnp.exp(b_last)
    for h in range(C_HEADS):
        hs = slice(h * C_DK, (h + 1) * C_DK)
        vs = slice(h * C_DV, (h + 1) * C_DV)
        v = v_ref[:, vs]
        st = st_ref[h]
        o = _dot(a_scr[h].astype(BF16), v) + _dot_nt(qs[:, hs], st.astype(BF16))
        st_ref[h] = st * decay[:, hs] + _dot_tn(v, kd[:, hs])
        on = _ln(o, lng_ref[...], lnb_ref[...])
        y_ref[:, vs] = (on * sg_ref[:, vs]).astype(y_ref.dtype)


def _gla(q, k, v, la, sg, lng, lnb, *, batch, seq, layer):
    n = q.shape[0]
    c = GLA_CHUNK
    nt = seq // c
    hk, hv = C_HEADS * C_DK, C_HEADS * C_DV
    tri = jnp.tril(jnp.ones((c, c), F32)).astype(BF16)
    lvl = _gla_level_matrix()
    kblk = pl.BlockSpec((c, hk), lambda b, i: (b * nt + i, 0))
    vblk = pl.BlockSpec((c, hv), lambda b, i: (b * nt + i, 0))
    full2 = lambda a: pl.BlockSpec(a.shape, lambda b, i: (0, 0))
    return pl.pallas_call(
        _gla_kernel,
        grid=(batch, nt),
        in_specs=[kblk, kblk, vblk, kblk, vblk, _layer_spec(lng, layer), _layer_spec(lnb, layer),
                  full2(tri), full2(lvl)],
        out_specs=vblk,
        out_shape=jax.ShapeDtypeStruct((n, hv), BF16),
        scratch_shapes=[pltpu.VMEM((C_HEADS, C_DV, C_DK), F32), pltpu.VMEM((C_HEADS, c, c), F32)],
        compiler_params=pltpu.CompilerParams(dimension_semantics=("parallel", "arbitrary"),
                                             vmem_limit_bytes=VMEM_LIMIT),
        name="gla",
    )(q, k, v, la, sg, lng, lnb, tri, lvl)


def _post_ffn_kernel(x_ref, y1_ref, y2_ref, wo_ref, g1_ref, b1_ref, w1_ref, w2_ref, g2_ref, b2_ref,
                     o_ref, x1_ref, x1b_ref, acc_ref):
    j = pl.program_id(1)
    last = pl.num_programs(1) - 1
    half = y1_ref.shape[1]
    tm = x_ref.shape[0]
    blocks = [slice(r, r + FFN_ROW_BLOCK) for r in range(0, tm, FFN_ROW_BLOCK)]

    def mlp(x1b):
        h = _dot(x1b, w1_ref[...])
        return _dot(jnp.square(jnp.maximum(h, 0.0)).astype(BF16), w2_ref[...])

    @pl.when(j == 0)
    def _():
        for rows in blocks:
            y = _dot(y1_ref[rows, :], wo_ref[0:half, :]) + _dot(y2_ref[rows, :], wo_ref[half:2 * half, :])
            x1 = _ln(ALPHA * x_ref[rows, :] + y, g1_ref[...], b1_ref[...])
            x1_ref[rows, :] = x1
            x1b = x1.astype(BF16)
            x1b_ref[rows, :] = x1b
            acc_ref[rows, :] = mlp(x1b)

    @pl.when((j > 0) & (j < last))
    def _():
        acc_ref[...] += mlp(x1b_ref[...])

    @pl.when(j == last)
    def _():
        for rows in blocks:
            acc = acc_ref[rows, :] + mlp(x1b_ref[rows, :])
            o_ref[rows, :] = _ln(ALPHA * x1_ref[rows, :] + acc, g2_ref[...], b2_ref[...])


def _post_ffn(x, y1, y2, y2_col, wo, g1, b1, w1, w2, g2, b2, *, layer):
    n = x.shape[0]
    tm, tf = TILES["ffn_rows"], TILES["ffn_cols"]
    assert D_FF // tf >= 2
    half = D_MODEL // 2
    row = pl.BlockSpec((tm, D_MODEL), lambda i, j: (i, 0))
    vec = pl.BlockSpec((None, 1, D_MODEL), lambda i, j: (layer, 0, 0))
    return pl.pallas_call(
        _post_ffn_kernel,
        grid=(n // tm, D_FF // tf),
        in_specs=[row,
                  pl.BlockSpec((tm, half), lambda i, j: (i, 0)),
                  pl.BlockSpec((tm, half), lambda i, j: (i, y2_col)),
                  pl.BlockSpec((None, D_MODEL, D_MODEL), lambda i, j: (layer, 0, 0)),
                  vec, vec,
                  pl.BlockSpec((None, None, D_MODEL, tf), lambda i, j: (layer, j, 0, 0)),
                  pl.BlockSpec((None, tf, D_MODEL), lambda i, j: (layer, j, 0)),
                  vec, vec],
        out_specs=row,
        out_shape=jax.ShapeDtypeStruct((n, D_MODEL), F32),
        scratch_shapes=[pltpu.VMEM((tm, D_MODEL), F32), pltpu.VMEM((tm, D_MODEL), BF16),
                        pltpu.VMEM((tm, D_MODEL), F32)],
        compiler_params=pltpu.CompilerParams(dimension_semantics=("parallel", "arbitrary"),
                                             vmem_limit_bytes=VMEM_LIMIT),
        name="post_ffn",
    )(x, y1, y2, wo, g1, b1, w1, w2, g2, b2)


def _rope_inv_lanes():
    half = B_ROPE // 2
    inv = ROPE_THETA ** (-jnp.arange(half, dtype=F32) / half)
    lanes = jnp.zeros((1, HEAD_PAD), F32)
    lanes = lanes.at[0, B_NOPE:B_NOPE + half].set(inv)
    lanes = lanes.at[0, B_NOPE + half:B_NOPE + B_ROPE].set(inv)
    return lanes


def kernel(x, positions, ln1_g, ln1_b, ln2_g, ln2_b, w_in_even, a_w_s, a_b_s, a_ln_g, a_ln_b, b_q_norm, b_kv_norm, b_w_uq, b_w_ukv, w_out_even, w_in_odd, c_w_gate, c_b_gate, c_ln_g, c_ln_b, w_out_odd, w_ff1, w_ff2):
    bn, s, d = x.shape
    n = bn * s
    xf = x.reshape(n, d)
    pos = positions.reshape(n, 1).astype(jnp.int32)
    cos, sin = _rope_tables(pos, _rope_inv_lanes())
    n_even = w_in_even.shape[0]
    vec = lambda a: a.reshape(a.shape[0], 1, -1)

    o_kr = 2 * A_WIDTH + B_Q_RANK + B_KV_RANK
    pad = lambda width: jnp.zeros((n_even, d, width), F32)
    win_even = jnp.concatenate([w_in_even[:, :, :o_kr], pad(B_NOPE), w_in_even[:, :, o_kr:],
                                pad(HEAD_PAD - B_NOPE - B_ROPE)], axis=-1).astype(BF16)
    wuq = b_w_uq.reshape(n_even, B_Q_RANK, B_HEADS, B_NOPE + B_ROPE)
    wuq = jnp.pad(wuq, ((0, 0), (0, 0), (0, 0), (0, HEAD_PAD - B_NOPE - B_ROPE)))
    wuq = wuq.reshape(n_even, B_Q_RANK, B_HEADS * HEAD_PAD).astype(BF16)
    wukv = b_w_ukv.reshape(n_even, B_KV_RANK, B_HEADS, B_NOPE + B_VDIM)
    wuk = jnp.pad(wukv[..., :B_NOPE], ((0, 0), (0, 0), (0, 0), (0, HEAD_PAD - B_NOPE)))
    wuk = wuk.reshape(n_even, B_KV_RANK, B_HEADS * HEAD_PAD).astype(BF16)
    wuvv = wukv[..., B_NOPE:].reshape(n_even, B_KV_RANK, B_HEADS * B_VDIM).astype(BF16)
    causal = jnp.tril(jnp.ones((A_CHUNK, A_CHUNK), dtype=bool))
    ws = jnp.where(causal, a_w_s, 0.0).astype(BF16)
    bs = jnp.broadcast_to(a_b_s[..., None], a_b_s.shape + (A_GROUP_DIM,))
    even_params = (win_even, ws, bs, a_ln_g, a_ln_b, vec(b_q_norm), vec(b_kv_norm), wuq, wuk, wuvv)

    hk, hv = C_HEADS * C_DK, C_HEADS * C_DV
    win_odd = w_in_odd[:, :, :2 * hk + 2 * hv].astype(BF16)
    wzg = jnp.pad(w_in_odd[:, :, 2 * hk + 2 * hv:], ((0, 0), (0, 0), (0, 128 - C_GATE_RANK))).astype(BF16)
    wgate = jnp.pad(c_w_gate, ((0, 0), (0, 128 - C_GATE_RANK), (0, 0))).astype(BF16)
    odd_params = (win_odd, wzg, wgate, vec(c_b_gate))
    c_lng, c_lnb = vec(c_ln_g), vec(c_ln_b)

    w_out = jnp.stack([(w_out_even if layer % 2 == 0 else w_out_odd)[layer // 2]
                       for layer in range(DEPTH)]).astype(BF16)
    tf = TILES["ffn_cols"]
    w1 = w_ff1.reshape(DEPTH, d, D_FF // tf, tf).transpose(0, 2, 1, 3).astype(BF16)
    w2 = w_ff2.astype(BF16)
    g1, b1, g2, b2 = vec(ln1_g), vec(ln1_b), vec(ln2_g), vec(ln2_b)

    for layer in range(DEPTH):
        j = layer // 2
        if layer % 2 == 0:
            ya, qt, k, vt = _even_pre(xf, cos, sin, *even_params, layer=j)
            yb = _attn(qt, k, vt, batch=bn, seq=s)
            y1, y2, y2_col = ya, yb, 0
        else:
            q, k, v, sg, la = _odd_pre(xf, *odd_params, layer=j)
            y = _gla(q, k, v, la, sg, c_lng, c_lnb, batch=bn, seq=s, layer=j)
            y1, y2, y2_col = y, y, 1
        xf = _post_ffn(xf, y1, y2, y2_col, w_out, g1, b1, w1, w2, g2, b2, layer=layer)
    return xf.reshape(bn, s, d)
```

```python
import jax
import jax.numpy as jnp
from jax import lax
from jax.experimental import pallas as pl
from jax.experimental.pallas import tpu as pltpu

F32 = jnp.float32
BF16 = jnp.bfloat16

D_MODEL = 1024
DEPTH = 4

A_CHUNK = 128
A_GROUPS = 4
A_WIDTH = 512
A_GROUP_DIM = 128

B_HEADS = 8
B_NOPE = 64
B_ROPE = 32
B_VDIM = 64
B_Q_RANK = 384
B_KV_RANK = 256
ROPE_THETA = 10000.0
HEAD_PAD = 128
ATTN_SUM_ROWS = 16

C_HEADS = 4
C_DK = 128
C_DV = 256
C_GATE_RANK = 16
C_GATE_TAU = 16.0
GLA_CHUNK = 256
GLA_FINE_LEVELS = (1, 2, 4)
GLA_BLOCK_LEVELS = (128, 64)

D_FF = 4 * D_MODEL
ALPHA = (2.0 * DEPTH) ** 0.25
LN_EPS = 1e-5
LOG2_E = 1.4426950408889634

VMEM_LIMIT = 56 * 1024 * 1024

TILES = {
    "attn": 512,
    "odd_pre": 512,
    "rope": 2048,
    "ffn_rows": 1024,
    "ffn_cols": 1024,
}
PRE_ROW_BLOCK = 256
ODD_ROW_BLOCK = 256
FFN_ROW_BLOCK = 256


def _dot(a, b):
    return jnp.dot(a, b, preferred_element_type=F32)


def _dot_nt(a, b):
    return lax.dot_general(a, b, (((1,), (1,)), ((), ())), preferred_element_type=F32)


def _dot_tn(a, b):
    return lax.dot_general(a, b, (((0,), (0,)), ((), ())), preferred_element_type=F32)


def _ln(x, g, b):
    mu = jnp.mean(x, axis=-1, keepdims=True)
    xc = x - mu
    var = jnp.mean(xc * xc, axis=-1, keepdims=True)
    return xc * lax.rsqrt(var + LN_EPS) * g + b


def _rms(x, g):
    ms = jnp.mean(x * x, axis=-1, keepdims=True)
    return x * lax.rsqrt(ms + LN_EPS) * g


def _gelu(x):
    return 0.5 * x * (1.0 + jnp.tanh(0.7978845608028654 * (x + 0.044715 * (x * x * x))))


def _layer_spec(a, layer):
    zeros = (0,) * (a.ndim - 1)
    return pl.BlockSpec((None,) + a.shape[1:], lambda *_: (layer,) + zeros)


def _rope_table_kernel(pos_ref, inv_ref, cos_ref, sin_ref):
    ang = pos_ref[...].astype(F32) * inv_ref[...]
    cos_ref[...] = jnp.cos(ang)
    sin_ref[...] = jnp.sin(ang)


def _rope_tables(pos, inv_lane):
    n = pos.shape[0]
    tm = TILES["rope"]
    row = lambda w: pl.BlockSpec((tm, w), lambda i: (i, 0))
    table = jax.ShapeDtypeStruct((n, HEAD_PAD), F32)
    return pl.pallas_call(
        _rope_table_kernel,
        grid=(n // tm,),
        in_specs=[row(1), pl.BlockSpec(inv_lane.shape, lambda i: (0, 0))],
        out_specs=[row(HEAD_PAD), row(HEAD_PAD)],
        out_shape=[table, table],
        compiler_params=pltpu.CompilerParams(dimension_semantics=("parallel",)),
        name="rope_tables",
    )(pos, inv_lane)


def _even_pre_kernel(x_ref, cos_ref, sin_ref, win_ref,
                     ws_ref, bs_ref, alng_ref, alnb_ref, gq_ref, gkv_ref,
                     wuq_ref, wuk_ref, wuvv_ref,
                     ya_ref, qt_ref, k_ref, vt_ref):
    tm = x_ref.shape[0]
    o_cq = 2 * A_WIDTH
    o_ckv = o_cq + B_Q_RANK
    o_kr = o_ckv + B_KV_RANK
    lane = lax.broadcasted_iota(jnp.int32, (1, HEAD_PAD), 1)
    half = B_ROPE // 2
    scale = (B_NOPE + B_ROPE) ** -0.5 * LOG2_E

    def in_proj(rows):
        return _dot(x_ref[rows, :].astype(BF16), win_ref[...])

    def finish(rows, z):
        for g in range(A_GROUPS):
            lo = g * A_GROUP_DIM
            u = _gelu(z[:, lo:lo + A_GROUP_DIM])
            vv = _gelu(z[:, A_WIDTH + lo:A_WIDTH + lo + A_GROUP_DIM])
            vn = _ln(vv, alng_ref[g:g + 1, :], alnb_ref[g:g + 1, :]).astype(BF16)
            for r in range(0, rows.stop - rows.start, A_CHUNK):
                mixed = _dot(ws_ref[g], vn[r:r + A_CHUNK, :]) + bs_ref[g]
                ya_ref[rows.start + r:rows.start + r + A_CHUNK, lo:lo + A_GROUP_DIM] = (
                    u[r:r + A_CHUNK, :] * mixed).astype(ya_ref.dtype)

        cos = cos_ref[rows, :]
        sin = sin_ref[rows, :]
        s1 = jnp.where((lane >= B_NOPE) & (lane < B_NOPE + half), -sin, 0.0)
        s2 = jnp.where((lane >= B_NOPE + half) & (lane < B_NOPE + B_ROPE), sin, 0.0)

        def rope(t):
            return t * cos + pltpu.roll(t, HEAD_PAD - half, 1) * s1 + pltpu.roll(t, half, 1) * s2

        cqn = _rms(z[:, o_cq:o_ckv], gq_ref[...]).astype(BF16)
        qf = _dot(cqn, wuq_ref[...])
        ckvn = _rms(z[:, o_ckv:o_kr], gkv_ref[...]).astype(BF16)
        kn = _dot(ckvn, wuk_ref[...])
        kr = rope(z[:, o_kr:o_kr + HEAD_PAD])
        for h in range(B_HEADS):
            lo = h * HEAD_PAD
            qh = rope(qf[:, lo:lo + HEAD_PAD]) * scale
            qt_ref[0, lo:lo + HEAD_PAD, rows] = qh.T.astype(qt_ref.dtype)
            k_ref[rows, lo:lo + HEAD_PAD] = (kn[:, lo:lo + HEAD_PAD] + kr).astype(k_ref.dtype)
        vt_ref[0, :, rows] = _dot(ckvn, wuvv_ref[...]).T.astype(vt_ref.dtype)

    blocks = [slice(r, r + PRE_ROW_BLOCK) for r in range(0, tm, PRE_ROW_BLOCK)]
    z = in_proj(blocks[0])
    for r, rows in enumerate(blocks):
        z_next = in_proj(blocks[r + 1]) if r + 1 < len(blocks) else None
        finish(rows, z)
        z = z_next


def _even_pre(x, cos, sin, *params, layer):
    n = x.shape[0]
    tm = TILES["attn"]
    row = lambda w: pl.BlockSpec((tm, w), lambda i: (i, 0))
    hp = B_HEADS * HEAD_PAD
    return pl.pallas_call(
        _even_pre_kernel,
        grid=(n // tm,),
        in_specs=[row(D_MODEL), row(HEAD_PAD), row(HEAD_PAD)] + [_layer_spec(a, layer) for a in params],
        out_specs=[row(A_WIDTH),
                   pl.BlockSpec((1, hp, tm), lambda i: (i, 0, 0)),
                   row(hp),
                   pl.BlockSpec((1, B_HEADS * B_VDIM, tm), lambda i: (i, 0, 0))],
        out_shape=[jax.ShapeDtypeStruct((n, A_WIDTH), BF16),
                   jax.ShapeDtypeStruct((n // tm, hp, tm), BF16),
                   jax.ShapeDtypeStruct((n, hp), BF16),
                   jax.ShapeDtypeStruct((n // tm, B_HEADS * B_VDIM, tm), BF16)],
        compiler_params=pltpu.CompilerParams(dimension_semantics=("parallel",),
                                             vmem_limit_bytes=VMEM_LIMIT),
        name="even_pre",
    )(x, cos, sin, *params)


def _attn_kernel(qt_ref, k_ref, vt_ref, o_ref, sa_ref, sb_ref):
    t = o_ref.shape[0]
    qi = pl.program_id(2)

    def scores(j, dst):
        start = pl.multiple_of(j * t, t)
        for hh in range(2):
            lo = hh * HEAD_PAD
            dst[hh] = _dot(k_ref[pl.ds(start, t), lo:lo + HEAD_PAD], qt_ref[0, lo:lo + HEAD_PAD, :])

    ones = jnp.ones((ATTN_SUM_ROWS, t), BF16)

    def update(j, src, stats, masked):
        vt = vt_ref[j]
        new = []
        for hh in range(2):
            m, acc = stats[hh]
            s = src[hh]
            if masked:
                key_ids = lax.broadcasted_iota(jnp.int32, (t, t), 0)
                qry_ids = lax.broadcasted_iota(jnp.int32, (t, t), 1)
                s = jnp.where(key_ids <= qry_ids, s, -jnp.inf)
            m_new = jnp.maximum(m, jnp.max(s, axis=0, keepdims=True))
            p = jnp.exp2(s - m_new).astype(BF16)
            a = jnp.exp2(m - m_new)
            v_ext = jnp.concatenate([vt[hh * B_VDIM:(hh + 1) * B_VDIM, :], ones], axis=0)
            acc = a * acc + _dot(v_ext, p)
            new.append((m_new, acc))
        return tuple(new)

    def two_blocks(i, stats):
        j = 2 * i
        scores(j + 1, sb_ref)
        stats = update(j, sa_ref, stats, False)
        scores(j + 2, sa_ref)
        return update(j + 1, sb_ref, stats, False)

    def tail_odd(stats):
        scores(qi, sb_ref)
        stats = update(qi - 1, sa_ref, stats, False)
        return update(qi, sb_ref, stats, True)

    def tail_even(stats):
        return update(qi, sa_ref, stats, True)

    scores(0, sa_ref)
    init1 = (jnp.full((1, t), -jnp.inf, F32), jnp.zeros((B_VDIM + ATTN_SUM_ROWS, t), F32))
    def four_blocks(i, stats):
        return two_blocks(2 * i + 1, two_blocks(2 * i, stats))

    stats = lax.fori_loop(0, qi // 4, four_blocks, (init1, init1))
    stats = lax.fori_loop(2 * (qi // 4), qi // 2, two_blocks, stats)
    stats = lax.cond(qi % 2 == 1, tail_odd, tail_even, stats)
    out_t = jnp.concatenate([acc[:B_VDIM] / acc[B_VDIM:B_VDIM + 1] for (_, acc) in stats], axis=0)
    o_ref[...] = out_t.T.astype(o_ref.dtype)


def _attn(qt, k, vt, *, batch, seq):
    t = qt.shape[2]
    n = k.shape[0]
    nq = seq // t
    return pl.pallas_call(
        _attn_kernel,
        grid=(batch, B_HEADS // 2, nq),
        in_specs=[pl.BlockSpec((1, 2 * HEAD_PAD, t), lambda b, h, i: (b * nq + i, h, 0)),
                  pl.BlockSpec((seq, 2 * HEAD_PAD), lambda b, h, i: (b, h)),
                  pl.BlockSpec((nq, 2 * B_VDIM, t), lambda b, h, i: (b, h, 0))],
        out_specs=pl.BlockSpec((t, 2 * B_VDIM), lambda b, h, i: (b * nq + i, h)),
        out_shape=jax.ShapeDtypeStruct((n, B_HEADS * B_VDIM), BF16),
        scratch_shapes=[pltpu.VMEM((2, t, t), F32), pltpu.VMEM((2, t, t), F32)],
        compiler_params=pltpu.CompilerParams(dimension_semantics=("parallel", "parallel", "arbitrary"),
                                             vmem_limit_bytes=VMEM_LIMIT),
        name="attn",
    )(qt, k, vt)


def _odd_pre_kernel(x_ref, win_ref, wzg_ref, wgate_ref, bgate_ref,
                    q_ref, k_ref, v_ref, sg_ref, la_ref):
    hk, hv = C_HEADS * C_DK, C_HEADS * C_DV
    tm = x_ref.shape[0]

    def in_proj(rows):
        xb = x_ref[rows, :].astype(BF16)
        return _dot(xb, win_ref[...]), _dot(xb, wzg_ref[...])

    def finish(rows, z, zg):
        q_ref[rows, :] = z[:, :hk] * (C_DK ** -0.5)
        k_ref[rows, :] = z[:, hk:2 * hk]
        v_ref[rows, :] = z[:, 2 * hk:2 * hk + hv].astype(v_ref.dtype)
        g = z[:, 2 * hk + hv:]
        sg_ref[rows, :] = g * jax.nn.sigmoid(g)
        logits = _dot(zg.astype(BF16), wgate_ref[...]) + bgate_ref[...]
        la_ref[rows, :] = -(jnp.maximum(-logits, 0.0) + jnp.log1p(jnp.exp(-jnp.abs(logits)))) * (1.0 / C_GATE_TAU)

    blocks = [slice(r, r + ODD_ROW_BLOCK) for r in range(0, tm, ODD_ROW_BLOCK)]
    cur = in_proj(blocks[0])
    for r, rows in enumerate(blocks):
        nxt = in_proj(blocks[r + 1]) if r + 1 < len(blocks) else None
        finish(rows, *cur)
        cur = nxt


def _odd_pre(x, *params, layer):
    n = x.shape[0]
    tm = TILES["odd_pre"]
    row = lambda w: pl.BlockSpec((tm, w), lambda i: (i, 0))
    hk, hv = C_HEADS * C_DK, C_HEADS * C_DV
    return pl.pallas_call(
        _odd_pre_kernel,
        grid=(n // tm,),
        in_specs=[row(D_MODEL)] + [_layer_spec(a, layer) for a in params],
        out_specs=[row(hk), row(hk), row(hv), row(hv), row(hk)],
        out_shape=[jax.ShapeDtypeStruct((n, hk), F32),
                   jax.ShapeDtypeStruct((n, hk), F32),
                   jax.ShapeDtypeStruct((n, hv), BF16),
                   jax.ShapeDtypeStruct((n, hv), F32),
                   jax.ShapeDtypeStruct((n, hk), F32)],
        compiler_params=pltpu.CompilerParams(dimension_semantics=("parallel",),
                                             vmem_limit_bytes=VMEM_LIMIT),
        name="odd_pre",
    )(x, *params)


def _gla_level_matrix():
    c = GLA_CHUNK
    t = jnp.arange(c)[:, None]
    u = jnp.arange(c)[None, :]
    mats = []
    for m in GLA_FINE_LEVELS:
        r = t % (2 * m)
        p = t - r + m - 1
        upper = (r >= m) & (u > p) & (u <= t)
        lower = (r < m) & (u > t) & (u <= p)
        mats.append(upper | lower)
    return jnp.concatenate(mats, axis=0).astype(BF16)


def _split_bf16(x, pieces):
    out = []
    for _ in range(pieces - 1):
        hi = x.astype(BF16)
        out.append(hi)
        x = x - hi.astype(F32)
    out.append(x.astype(BF16))
    return out


def _gla_kernel(q_ref, k_ref, v_ref, la_ref, sg_ref, lng_ref, lnb_ref, tri_ref, lvl_ref,
                y_ref, st_ref, a_scr):
    c = GLA_CHUNK
    hk = C_HEADS * C_DK

    @pl.when(pl.program_id(1) == 0)
    def _():
        st_ref[...] = jnp.zeros_like(st_ref)

    q = q_ref[...]
    k = k_ref[...]
    g = _split_bf16(la_ref[...], 3)
    tri = tri_ref[...]
    b = _dot(tri, g[0]) + _dot(tri, g[1]) + _dot(tri, g[2])
    fine = _dot(lvl_ref[...], g[0]) + _dot(lvl_ref[...], g[1])

    row = lax.broadcasted_iota(jnp.int32, (c, 1), 0)
    pair = lax.broadcasted_iota(jnp.int32, (c, c), 0) ^ lax.broadcasted_iota(jnp.int32, (c, c), 1)

    qb = q.astype(BF16)
    kb = k.astype(BF16)
    for h in range(C_HEADS):
        hs = slice(h * C_DK, (h + 1) * C_DK)
        a_scr[h] = jnp.where(pair == 0, _dot_nt(qb[:, hs], kb[:, hs]), 0.0)

    for m in GLA_BLOCK_LEVELS:
        for base in range(0, c, 2 * m):
            lower, upper = slice(base, base + m), slice(base + m, base + 2 * m)
            b_ref_row = b[base + m - 1:base + m, :]
            qe = (q[upper, :] * jnp.exp(b[upper, :] - b_ref_row)).astype(BF16)
            ke = (k[lower, :] * jnp.exp(b_ref_row - b[lower, :])).astype(BF16)
            for h in range(C_HEADS):
                hs = slice(h * C_DK, (h + 1) * C_DK)
                a_scr[h, upper, lower] = _dot_nt(qe[:, hs], ke[:, hs])

    m = 1
    while m < min(GLA_BLOCK_LEVELS):
        if m in GLA_FINE_LEVELS:
            i = GLA_FINE_LEVELS.index(m)
            d = fine[i * c:(i + 1) * c, :]
        else:
            b3 = b.reshape(c // (2 * m), 2 * m, hk)
            d3 = b3 - b3[:, m - 1:m, :]
            sub = lax.broadcasted_iota(jnp.int32, (1, 2 * m, 1), 1)
            d = jnp.where(sub >= m, d3, -d3).reshape(c, hk)
        e = jnp.exp(d)
        upper = (row & m) != 0
        qe = jnp.where(upper, q * e, 0.0).astype(BF16)
        ke = jnp.where(upper, 0.0, k * e).astype(BF16)
        level = (pair >= m) & (pair < 2 * m)
        for h in range(C_HEADS):
            hs = slice(h * C_DK, (h + 1) * C_DK)
            a_scr[h] = jnp.where(level, _dot_nt(qe[:, hs], ke[:, hs]), a_scr[h])
        m *= 2

    b_last = b[c - 1:c, :]
    qs = (q * jnp.exp(b)).astype(BF16)
    kd = (k * jnp.exp(b_last - b)).astype(BF16)
    decay = jnp.exp(b_last)
    for h in range(C_HEADS):
        hs = slice(h * C_DK, (h + 1) * C_DK)
        vs = slice(h * C_DV, (h + 1) * C_DV)
        v = v_ref[:, vs]
        st = st_ref[h]
        o = _dot(a_scr[h].astype(BF16), v) + _dot_nt(qs[:, hs], st.astype(BF16))
        st_ref[h] = st * decay[:, hs] + _dot_tn(v, kd[:, hs])
        on = _ln(o, lng_ref[...], lnb_ref[...])
        y_ref[:, vs] = (on * sg_ref[:, vs]).astype(y_ref.dtype)


def _gla(q, k, v, la, sg, lng, lnb, *, batch, seq, layer):
    n = q.shape[0]
    c = GLA_CHUNK
    nt = seq // c
    hk, hv = C_HEADS * C_DK, C_HEADS * C_DV
    tri = jnp.tril(jnp.ones((c, c), F32)).astype(BF16)
    lvl = _gla_level_matrix()
    kblk = pl.BlockSpec((c, hk), lambda b, i: (b * nt + i, 0))
    vblk = pl.BlockSpec((c, hv), lambda b, i: (b * nt + i, 0))
    full2 = lambda a: pl.BlockSpec(a.shape, lambda b, i: (0, 0))
    return pl.pallas_call(
        _gla_kernel,
        grid=(batch, nt),
        in_specs=[kblk, kblk, vblk, kblk, vblk, _layer_spec(lng, layer), _layer_spec(lnb, layer),
                  full2(tri), full2(lvl)],
        out_specs=vblk,
        out_shape=jax.ShapeDtypeStruct((n, hv), BF16),
        scratch_shapes=[pltpu.VMEM((C_HEADS, C_DV, C_DK), F32), pltpu.VMEM((C_HEADS, c, c), F32)],
        compiler_params=pltpu.CompilerParams(dimension_semantics=("parallel", "arbitrary"),
                                             vmem_limit_bytes=VMEM_LIMIT),
        name="gla",
    )(q, k, v, la, sg, lng, lnb, tri, lvl)


def _post_ffn_kernel(x_ref, y1_ref, y2_ref, wo_ref, g1_ref, b1_ref, w1_ref, w2_ref, g2_ref, b2_ref,
                     o_ref, x1_ref, x1b_ref, acc_ref):
    i = pl.program_id(0)
    j = pl.program_id(1)
    nt = pl.num_programs(0) - 1
    half = y1_ref.shape[1]
    tm = x_ref.shape[0]
    blocks = [slice(r, r + FFN_ROW_BLOCK) for r in range(0, tm, FFN_ROW_BLOCK)]

    def mlp(x1b):
        h = _dot(x1b, w1_ref[...])
        return _dot(jnp.square(jnp.maximum(h, 0.0)).astype(BF16), w2_ref[...])

    def close_tile(rows):
        o_ref[rows, :] = _ln(ALPHA * x1_ref[rows, :] + acc_ref[rows, :], g2_ref[...], b2_ref[...])

    def out_proj(rows):
        return _dot(y1_ref[rows, :], wo_ref[0:half, :]) + _dot(y2_ref[rows, :], wo_ref[half:2 * half, :])

    def open_tile(rows, y):
        x1 = _ln(ALPHA * x_ref[rows, :] + y, g1_ref[...], b1_ref[...])
        x1_ref[rows, :] = x1
        x1b = x1.astype(BF16)
        x1b_ref[rows, :] = x1b
        acc_ref[rows, :] = mlp(x1b)

    def open_all(close_first):
        y = out_proj(blocks[0])
        for r, rows in enumerate(blocks):
            y_next = out_proj(blocks[r + 1]) if r + 1 < len(blocks) else None
            if close_first:
                close_tile(rows)
            open_tile(rows, y)
            y = y_next

    @pl.when((j == 0) & (i == 0))
    def _():
        open_all(False)

    @pl.when((j == 0) & (i > 0) & (i < nt))
    def _():
        open_all(True)

    @pl.when((j == 0) & (i == nt))
    def _():
        for rows in blocks:
            close_tile(rows)

    @pl.when((j > 0) & (i < nt))
    def _():
        acc_ref[...] += mlp(x1b_ref[...])


def _post_ffn(x, y1, y2, y2_col, wo, g1, b1, w1, w2, g2, b2, *, layer):
    n = x.shape[0]
    tm, tf = TILES["ffn_rows"], TILES["ffn_cols"]
    nt = n // tm
    half = D_MODEL // 2
    opened = lambda i: jnp.minimum(i, nt - 1)
    vec = pl.BlockSpec((None, 1, D_MODEL), lambda i, j: (layer, 0, 0))
    return pl.pallas_call(
        _post_ffn_kernel,
        grid=(nt + 1, D_FF // tf),
        in_specs=[pl.BlockSpec((tm, D_MODEL), lambda i, j: (opened(i), 0)),
                  pl.BlockSpec((tm, half), lambda i, j: (opened(i), 0)),
                  pl.BlockSpec((tm, half), lambda i, j: (opened(i), y2_col)),
                  pl.BlockSpec((None, D_MODEL, D_MODEL), lambda i, j: (layer, 0, 0)),
                  vec, vec,
                  pl.BlockSpec((None, D_MODEL, tf), lambda i, j: (layer, 0, j)),
                  pl.BlockSpec((None, tf, D_MODEL), lambda i, j: (layer, j, 0)),
                  vec, vec],
        out_specs=pl.BlockSpec((tm, D_MODEL), lambda i, j: (jnp.maximum(i - 1, 0), 0)),
        out_shape=jax.ShapeDtypeStruct((n, D_MODEL), F32),
        scratch_shapes=[pltpu.VMEM((tm, D_MODEL), F32), pltpu.VMEM((tm, D_MODEL), BF16),
                        pltpu.VMEM((tm, D_MODEL), F32)],
        compiler_params=pltpu.CompilerParams(dimension_semantics=("arbitrary", "arbitrary"),
                                             vmem_limit_bytes=VMEM_LIMIT),
        name="post_ffn",
    )(x, y1, y2, wo, g1, b1, w1, w2, g2, b2)


def _rope_inv_lanes():
    half = B_ROPE // 2
    inv = ROPE_THETA ** (-jnp.arange(half, dtype=F32) / half)
    lanes = jnp.zeros((1, HEAD_PAD), F32)
    lanes = lanes.at[0, B_NOPE:B_NOPE + half].set(inv)
    lanes = lanes.at[0, B_NOPE + half:B_NOPE + B_ROPE].set(inv)
    return lanes


def kernel(x, positions, ln1_g, ln1_b, ln2_g, ln2_b, w_in_even, a_w_s, a_b_s, a_ln_g, a_ln_b, b_q_norm, b_kv_norm, b_w_uq, b_w_ukv, w_out_even, w_in_odd, c_w_gate, c_b_gate, c_ln_g, c_ln_b, w_out_odd, w_ff1, w_ff2):
    bn, s, d = x.shape
    n = bn * s
    xf = x.reshape(n, d)
    pos = positions.reshape(n, 1).astype(jnp.int32)
    cos, sin = _rope_tables(pos, _rope_inv_lanes())
    n_even = w_in_even.shape[0]
    vec = lambda a: a.reshape(a.shape[0], 1, -1)

    o_kr = 2 * A_WIDTH + B_Q_RANK + B_KV_RANK
    pad = lambda width: jnp.zeros((n_even, d, width), F32)
    win_even = jnp.concatenate([w_in_even[:, :, :o_kr], pad(B_NOPE), w_in_even[:, :, o_kr:],
                                pad(HEAD_PAD - B_NOPE - B_ROPE)], axis=-1).astype(BF16)
    wuq = b_w_uq.reshape(n_even, B_Q_RANK, B_HEADS, B_NOPE + B_ROPE)
    wuq = jnp.pad(wuq, ((0, 0), (0, 0), (0, 0), (0, HEAD_PAD - B_NOPE - B_ROPE)))
    wuq = wuq.reshape(n_even, B_Q_RANK, B_HEADS * HEAD_PAD).astype(BF16)
    wukv = b_w_ukv.reshape(n_even, B_KV_RANK, B_HEADS, B_NOPE + B_VDIM)
    wuk = jnp.pad(wukv[..., :B_NOPE], ((0, 0), (0, 0), (0, 0), (0, HEAD_PAD - B_NOPE)))
    wuk = wuk.reshape(n_even, B_KV_RANK, B_HEADS * HEAD_PAD).astype(BF16)
    wuvv = wukv[..., B_NOPE:].reshape(n_even, B_KV_RANK, B_HEADS * B_VDIM).astype(BF16)
    causal = jnp.tril(jnp.ones((A_CHUNK, A_CHUNK), dtype=bool))
    ws = jnp.where(causal, a_w_s, 0.0).astype(BF16)
    bs = jnp.broadcast_to(a_b_s[..., None], a_b_s.shape + (A_GROUP_DIM,))
    even_params = (win_even, ws, bs, a_ln_g, a_ln_b, vec(b_q_norm), vec(b_kv_norm), wuq, wuk, wuvv)

    hk, hv = C_HEADS * C_DK, C_HEADS * C_DV
    win_odd = w_in_odd[:, :, :2 * hk + 2 * hv].astype(BF16)
    wzg = jnp.pad(w_in_odd[:, :, 2 * hk + 2 * hv:], ((0, 0), (0, 0), (0, 128 - C_GATE_RANK))).astype(BF16)
    wgate = jnp.pad(c_w_gate, ((0, 0), (0, 128 - C_GATE_RANK), (0, 0))).astype(BF16)
    odd_params = (win_odd, wzg, wgate, vec(c_b_gate))
    c_lng, c_lnb = vec(c_ln_g), vec(c_ln_b)

    w_out = jnp.stack([(w_out_even if layer % 2 == 0 else w_out_odd)[layer // 2]
                       for layer in range(DEPTH)]).astype(BF16)
    w1, w2 = w_ff1.astype(BF16), w_ff2.astype(BF16)
    g1, b1, g2, b2 = vec(ln1_g), vec(ln1_b), vec(ln2_g), vec(ln2_b)

    for layer in range(DEPTH):
        j = layer // 2
        if layer % 2 == 0:
            ya, qt, k, vt = _even_pre(xf, cos, sin, *even_params, layer=j)
            yb = _attn(qt, k, vt, batch=bn, seq=s)
            y1, y2, y2_col = ya, yb, 0
        else:
            q, k, v, sg, la = _odd_pre(xf, *odd_params, layer=j)
            y = _gla(q, k, v, la, sg, c_lng, c_lnb, batch=bn, seq=s, layer=j)
            y1, y2, y2_col = y, y, 1
        xf = _post_ffn(xf, y1, y2, y2_col, w_out, g1, b1, w1, w2, g2, b2, layer=layer)
    return xf.reshape(bn, s, d)
```

```python
import jax
import jax.numpy as jnp
from jax import lax
from jax.experimental import pallas as pl
from jax.experimental.pallas import tpu as pltpu

F32 = jnp.float32
BF16 = jnp.bfloat16

D_MODEL = 1024
DEPTH = 4

A_CHUNK = 128
A_GROUPS = 4
A_WIDTH = 512
A_GROUP_DIM = 128

B_HEADS = 8
B_NOPE = 64
B_ROPE = 32
B_VDIM = 64
B_Q_RANK = 384
B_KV_RANK = 256
ROPE_THETA = 10000.0
HEAD_PAD = 128
ATTN_SUM_ROWS = 16

C_HEADS = 4
C_DK = 128
C_DV = 256
C_GATE_RANK = 16
C_GATE_TAU = 16.0
GLA_CHUNK = 256
GLA_FINE_LEVELS = (1, 2, 4)
GLA_BLOCK_LEVELS = (128, 64)

D_FF = 4 * D_MODEL
ALPHA = (2.0 * DEPTH) ** 0.25
LN_EPS = 1e-5
LOG2_E = 1.4426950408889634

VMEM_LIMIT = 56 * 1024 * 1024

TILES = {
    "attn": 512,
    "odd_pre": 512,
    "rope": 2048,
    "gla": 512,
    "ffn_rows": 1024,
    "ffn_cols": 1024,
}
PRE_ROW_BLOCK = 256
ODD_ROW_BLOCK = 256
FFN_ROW_BLOCK = 256


def _dot(a, b):
    return jnp.dot(a, b, preferred_element_type=F32)


def _dot_nt(a, b):
    return lax.dot_general(a, b, (((1,), (1,)), ((), ())), preferred_element_type=F32)


def _dot_tn(a, b):
    return lax.dot_general(a, b, (((0,), (0,)), ((), ())), preferred_element_type=F32)


def _ln(x, g, b):
    mu = jnp.mean(x, axis=-1, keepdims=True)
    xc = x - mu
    var = jnp.mean(xc * xc, axis=-1, keepdims=True)
    return xc * lax.rsqrt(var + LN_EPS) * g + b


def _rms(x, g):
    ms = jnp.mean(x * x, axis=-1, keepdims=True)
    return x * lax.rsqrt(ms + LN_EPS) * g


def _gelu(x):
    return 0.5 * x * (1.0 + jnp.tanh(0.7978845608028654 * (x + 0.044715 * (x * x * x))))


def _layer_spec(a, layer):
    zeros = (0,) * (a.ndim - 1)
    return pl.BlockSpec((None,) + a.shape[1:], lambda *_: (layer,) + zeros)


def _rope_table_kernel(pos_ref, inv_ref, cos_ref, sin_ref):
    ang = pos_ref[...].astype(F32) * inv_ref[...]
    cos_ref[...] = jnp.cos(ang)
    sin_ref[...] = jnp.sin(ang)


def _rope_tables(pos, inv_lane):
    n = pos.shape[0]
    tm = TILES["rope"]
    row = lambda w: pl.BlockSpec((tm, w), lambda i: (i, 0))
    table = jax.ShapeDtypeStruct((n, HEAD_PAD), F32)
    return pl.pallas_call(
        _rope_table_kernel,
        grid=(n // tm,),
        in_specs=[row(1), pl.BlockSpec(inv_lane.shape, lambda i: (0, 0))],
        out_specs=[row(HEAD_PAD), row(HEAD_PAD)],
        out_shape=[table, table],
        compiler_params=pltpu.CompilerParams(dimension_semantics=("parallel",)),
        name="rope_tables",
    )(pos, inv_lane)


def _even_pre_kernel(x_ref, cos_ref, sin_ref, win_ref,
                     ws_ref, bs_ref, alng_ref, alnb_ref, gq_ref, gkv_ref,
                     wuqt_ref, wuk_ref, wuvvt_ref,
                     ya_ref, qt_ref, k_ref, vt_ref):
    tm = x_ref.shape[0]
    o_cq = 2 * A_WIDTH
    o_ckv = o_cq + B_Q_RANK
    o_kr = o_ckv + B_KV_RANK
    lane = lax.broadcasted_iota(jnp.int32, (1, HEAD_PAD), 1)
    half = B_ROPE // 2
    scale = (B_NOPE + B_ROPE) ** -0.5 * LOG2_E

    def in_proj(rows):
        return _dot(x_ref[rows, :].astype(BF16), win_ref[...])

    def finish(rows, z):
        for g in range(A_GROUPS):
            lo = g * A_GROUP_DIM
            u = _gelu(z[:, lo:lo + A_GROUP_DIM])
            vv = _gelu(z[:, A_WIDTH + lo:A_WIDTH + lo + A_GROUP_DIM])
            vn = _ln(vv, alng_ref[g:g + 1, :], alnb_ref[g:g + 1, :]).astype(BF16)
            for r in range(0, rows.stop - rows.start, A_CHUNK):
                mixed = _dot(ws_ref[g], vn[r:r + A_CHUNK, :]) + bs_ref[g]
                ya_ref[rows.start + r:rows.start + r + A_CHUNK, lo:lo + A_GROUP_DIM] = (
                    u[r:r + A_CHUNK, :] * mixed).astype(ya_ref.dtype)

        cos = cos_ref[rows, :]
        sin = sin_ref[rows, :]
        s1 = jnp.where((lane >= B_NOPE) & (lane < B_NOPE + half), -sin, 0.0)
        s2 = jnp.where((lane >= B_NOPE + half) & (lane < B_NOPE + B_ROPE), sin, 0.0)
        ckvn = _rms(z[:, o_ckv:o_kr], gkv_ref[...]).astype(BF16)
        kn = _dot(ckvn, wuk_ref[...])
        kr = z[:, o_kr:o_kr + HEAD_PAD]
        kr = kr * cos + pltpu.roll(kr, HEAD_PAD - half, 1) * s1 + pltpu.roll(kr, half, 1) * s2
        for h in range(B_HEADS):
            lo = h * HEAD_PAD
            k_ref[rows, lo:lo + HEAD_PAD] = (kn[:, lo:lo + HEAD_PAD] + kr).astype(k_ref.dtype)

        cos_t = cos.T[B_NOPE:B_NOPE + half, :]
        sin_t = sin.T[B_NOPE:B_NOPE + half, :]
        cqn = _rms(z[:, o_cq:o_ckv], gq_ref[...]).astype(BF16)
        q_t = _dot_nt(wuqt_ref[...], cqn)
        for h in range(B_HEADS):
            lo = h * HEAD_PAD
            x1 = q_t[lo + B_NOPE:lo + B_NOPE + half, :]
            x2 = q_t[lo + B_NOPE + half:lo + B_NOPE + B_ROPE, :]
            q_h = jnp.concatenate([q_t[lo:lo + B_NOPE, :], x1 * cos_t - x2 * sin_t, x2 * cos_t + x1 * sin_t,
                                   q_t[lo + B_NOPE + B_ROPE:lo + HEAD_PAD, :]], axis=0)
            qt_ref[0, lo:lo + HEAD_PAD, rows] = (q_h * scale).astype(qt_ref.dtype)
        vt_ref[0, :, rows] = _dot_nt(wuvvt_ref[...], ckvn).astype(vt_ref.dtype)

    blocks = [slice(r, r + PRE_ROW_BLOCK) for r in range(0, tm, PRE_ROW_BLOCK)]
    z = in_proj(blocks[0])
    for r, rows in enumerate(blocks):
        z_next = in_proj(blocks[r + 1]) if r + 1 < len(blocks) else None
        finish(rows, z)
        z = z_next


def _even_pre(x, cos, sin, *params, layer):
    n = x.shape[0]
    tm = TILES["attn"]
    row = lambda w: pl.BlockSpec((tm, w), lambda i: (i, 0))
    hp = B_HEADS * HEAD_PAD
    return pl.pallas_call(
        _even_pre_kernel,
        grid=(n // tm,),
        in_specs=[row(D_MODEL), row(HEAD_PAD), row(HEAD_PAD)] + [_layer_spec(a, layer) for a in params],
        out_specs=[row(A_WIDTH),
                   pl.BlockSpec((1, hp, tm), lambda i: (i, 0, 0)),
                   row(hp),
                   pl.BlockSpec((1, B_HEADS * B_VDIM, tm), lambda i: (i, 0, 0))],
        out_shape=[jax.ShapeDtypeStruct((n, A_WIDTH), BF16),
                   jax.ShapeDtypeStruct((n // tm, hp, tm), BF16),
                   jax.ShapeDtypeStruct((n, hp), BF16),
                   jax.ShapeDtypeStruct((n // tm, B_HEADS * B_VDIM, tm), BF16)],
        compiler_params=pltpu.CompilerParams(dimension_semantics=("parallel",),
                                             vmem_limit_bytes=VMEM_LIMIT),
        name="even_pre",
    )(x, cos, sin, *params)


def _attn_kernel(qt_ref, k_ref, vt_ref, o_ref, sa_ref, sb_ref):
    t = o_ref.shape[0]
    qi = pl.program_id(2)

    def scores(j, dst):
        start = pl.multiple_of(j * t, t)
        for hh in range(2):
            lo = hh * HEAD_PAD
            dst[hh] = _dot(k_ref[pl.ds(start, t), lo:lo + HEAD_PAD], qt_ref[0, lo:lo + HEAD_PAD, :])

    ones = jnp.ones((ATTN_SUM_ROWS, t), BF16)

    def update(j, src, stats, masked):
        vt = vt_ref[j]
        new = []
        for hh in range(2):
            m, acc = stats[hh]
            s = src[hh]
            if masked:
                key_ids = lax.broadcasted_iota(jnp.int32, (t, t), 0)
                qry_ids = lax.broadcasted_iota(jnp.int32, (t, t), 1)
                s = jnp.where(key_ids <= qry_ids, s, -jnp.inf)
            m_new = jnp.maximum(m, jnp.max(s, axis=0, keepdims=True))
            p = jnp.exp2(s - m_new).astype(BF16)
            a = jnp.exp2(m - m_new)
            v_ext = jnp.concatenate([vt[hh * B_VDIM:(hh + 1) * B_VDIM, :], ones], axis=0)
            acc = a * acc + _dot(v_ext, p)
            new.append((m_new, acc))
        return tuple(new)

    def two_blocks(i, stats):
        j = 2 * i
        scores(j + 1, sb_ref)
        stats = update(j, sa_ref, stats, False)
        scores(j + 2, sa_ref)
        return update(j + 1, sb_ref, stats, False)

    def tail_odd(stats):
        scores(qi, sb_ref)
        stats = update(qi - 1, sa_ref, stats, False)
        return update(qi, sb_ref, stats, True)

    def tail_even(stats):
        return update(qi, sa_ref, stats, True)

    scores(0, sa_ref)
    init1 = (jnp.full((1, t), -jnp.inf, F32), jnp.zeros((B_VDIM + ATTN_SUM_ROWS, t), F32))
    def four_blocks(i, stats):
        return two_blocks(2 * i + 1, two_blocks(2 * i, stats))

    stats = lax.fori_loop(0, qi // 4, four_blocks, (init1, init1))
    stats = lax.fori_loop(2 * (qi // 4), qi // 2, two_blocks, stats)
    stats = lax.cond(qi % 2 == 1, tail_odd, tail_even, stats)
    out_t = jnp.concatenate([acc[:B_VDIM] / acc[B_VDIM:B_VDIM + 1] for (_, acc) in stats], axis=0)
    o_ref[...] = out_t.T.astype(o_ref.dtype)


def _attn(qt, k, vt, *, batch, seq):
    t = qt.shape[2]
    n = k.shape[0]
    nq = seq // t
    return pl.pallas_call(
        _attn_kernel,
        grid=(batch, B_HEADS // 2, nq),
        in_specs=[pl.BlockSpec((1, 2 * HEAD_PAD, t), lambda b, h, i: (b * nq + i, h, 0)),
                  pl.BlockSpec((seq, 2 * HEAD_PAD), lambda b, h, i: (b, h)),
                  pl.BlockSpec((nq, 2 * B_VDIM, t), lambda b, h, i: (b, h, 0))],
        out_specs=pl.BlockSpec((t, 2 * B_VDIM), lambda b, h, i: (b * nq + i, h)),
        out_shape=jax.ShapeDtypeStruct((n, B_HEADS * B_VDIM), BF16),
        scratch_shapes=[pltpu.VMEM((2, t, t), F32), pltpu.VMEM((2, t, t), F32)],
        compiler_params=pltpu.CompilerParams(dimension_semantics=("parallel", "parallel", "arbitrary"),
                                             vmem_limit_bytes=VMEM_LIMIT),
        name="attn",
    )(qt, k, vt)


def _odd_pre_kernel(x_ref, win_ref, wzg_ref, wgate_ref, bgate_ref,
                    q_ref, k_ref, v_ref, sg_ref, la_ref):
    hk, hv = C_HEADS * C_DK, C_HEADS * C_DV
    tm = x_ref.shape[0]

    def in_proj(rows):
        xb = x_ref[rows, :].astype(BF16)
        return _dot(xb, win_ref[...]), _dot(xb, wzg_ref[...])

    def finish(rows, z, zg):
        q_ref[rows, :] = z[:, :hk] * (C_DK ** -0.5)
        k_ref[rows, :] = z[:, hk:2 * hk]
        v_ref[rows, :] = z[:, 2 * hk:2 * hk + hv].astype(v_ref.dtype)
        g = z[:, 2 * hk + hv:]
        sg_ref[rows, :] = g * jax.nn.sigmoid(g)
        logits = _dot(zg.astype(BF16), wgate_ref[...]) + bgate_ref[...]
        la_ref[rows, :] = -(jnp.maximum(-logits, 0.0) + jnp.log1p(jnp.exp(-jnp.abs(logits)))) * (1.0 / C_GATE_TAU)

    blocks = [slice(r, r + ODD_ROW_BLOCK) for r in range(0, tm, ODD_ROW_BLOCK)]
    cur = in_proj(blocks[0])
    for r, rows in enumerate(blocks):
        nxt = in_proj(blocks[r + 1]) if r + 1 < len(blocks) else None
        finish(rows, *cur)
        cur = nxt


def _odd_pre(x, *params, layer):
    n = x.shape[0]
    tm = TILES["odd_pre"]
    row = lambda w: pl.BlockSpec((tm, w), lambda i: (i, 0))
    hk, hv = C_HEADS * C_DK, C_HEADS * C_DV
    return pl.pallas_call(
        _odd_pre_kernel,
        grid=(n // tm,),
        in_specs=[row(D_MODEL)] + [_layer_spec(a, layer) for a in params],
        out_specs=[row(hk), row(hk), row(hv), row(hv), row(hk)],
        out_shape=[jax.ShapeDtypeStruct((n, hk), F32),
                   jax.ShapeDtypeStruct((n, hk), F32),
                   jax.ShapeDtypeStruct((n, hv), BF16),
                   jax.ShapeDtypeStruct((n, hv), F32),
                   jax.ShapeDtypeStruct((n, hk), F32)],
        compiler_params=pltpu.CompilerParams(dimension_semantics=("parallel",),
                                             vmem_limit_bytes=VMEM_LIMIT),
        name="odd_pre",
    )(x, *params)


def _gla_level_matrix():
    c = GLA_CHUNK
    t = jnp.arange(c)[:, None]
    u = jnp.arange(c)[None, :]
    mats = []
    for m in GLA_FINE_LEVELS:
        r = t % (2 * m)
        p = t - r + m - 1
        upper = (r >= m) & (u > p) & (u <= t)
        lower = (r < m) & (u > t) & (u <= p)
        mats.append(upper | lower)
    return jnp.concatenate(mats, axis=0).astype(BF16)


def _split_bf16(x, pieces):
    out = []
    for _ in range(pieces - 1):
        hi = x.astype(BF16)
        out.append(hi)
        x = x - hi.astype(F32)
    out.append(x.astype(BF16))
    return out


def _gla_kernel(q_ref, k_ref, v_ref, la_ref, sg_ref, lng_ref, lnb_ref, tri_ref, lvl_ref,
                y_ref, st_ref, a_scr):
    c = GLA_CHUNK
    hk = C_HEADS * C_DK
    n_chunks = q_ref.shape[0] // c

    @pl.when(pl.program_id(1) == 0)
    def _():
        st_ref[...] = jnp.zeros_like(st_ref)

    row = lax.broadcasted_iota(jnp.int32, (c, 1), 0)
    pair = lax.broadcasted_iota(jnp.int32, (c, c), 0) ^ lax.broadcasted_iota(jnp.int32, (c, c), 1)

    def intra(ci):
        rows = slice(ci * c, (ci + 1) * c)
        q = q_ref[rows, :]
        k = k_ref[rows, :]
        g = _split_bf16(la_ref[rows, :], 3)
        tri = tri_ref[...]
        b = _dot(tri, g[0]) + _dot(tri, g[1]) + _dot(tri, g[2])
        fine = _dot(lvl_ref[...], g[0]) + _dot(lvl_ref[...], g[1])

        qb = q.astype(BF16)
        kb = k.astype(BF16)
        for h in range(C_HEADS):
            hs = slice(h * C_DK, (h + 1) * C_DK)
            a_scr[ci, h] = jnp.where(pair == 0, _dot_nt(qb[:, hs], kb[:, hs]), 0.0)

        for m in GLA_BLOCK_LEVELS:
            for base in range(0, c, 2 * m):
                lower, upper = slice(base, base + m), slice(base + m, base + 2 * m)
                b_ref_row = b[base + m - 1:base + m, :]
                qe = (q[upper, :] * jnp.exp(b[upper, :] - b_ref_row)).astype(BF16)
                ke = (k[lower, :] * jnp.exp(b_ref_row - b[lower, :])).astype(BF16)
                for h in range(C_HEADS):
                    hs = slice(h * C_DK, (h + 1) * C_DK)
                    a_scr[ci, h, upper, lower] = _dot_nt(qe[:, hs], ke[:, hs])

        m = 1
        while m < min(GLA_BLOCK_LEVELS):
            if m in GLA_FINE_LEVELS:
                i = GLA_FINE_LEVELS.index(m)
                d = fine[i * c:(i + 1) * c, :]
            else:
                b3 = b.reshape(c // (2 * m), 2 * m, hk)
                d3 = b3 - b3[:, m - 1:m, :]
                sub = lax.broadcasted_iota(jnp.int32, (1, 2 * m, 1), 1)
                d = jnp.where(sub >= m, d3, -d3).reshape(c, hk)
            e = jnp.exp(d)
            upper = (row & m) != 0
            qe = jnp.where(upper, q * e, 0.0).astype(BF16)
            ke = jnp.where(upper, 0.0, k * e).astype(BF16)
            level = (pair >= m) & (pair < 2 * m)
            for h in range(C_HEADS):
                hs = slice(h * C_DK, (h + 1) * C_DK)
                a_scr[ci, h] = jnp.where(level, _dot_nt(qe[:, hs], ke[:, hs]), a_scr[ci, h])
            m *= 2

        b_last = b[c - 1:c, :]
        qs = (q * jnp.exp(b)).astype(BF16)
        kd = (k * jnp.exp(b_last - b)).astype(BF16)
        return qs, kd, jnp.exp(b_last)

    def carry_state(ci, qs, kd, decay):
        rows = slice(ci * c, (ci + 1) * c)
        for h in range(C_HEADS):
            hs = slice(h * C_DK, (h + 1) * C_DK)
            vs = slice(h * C_DV, (h + 1) * C_DV)
            v = v_ref[rows, vs]
            st = st_ref[h]
            o = _dot(a_scr[ci, h].astype(BF16), v) + _dot_nt(qs[:, hs], st.astype(BF16))
            st_ref[h] = st * decay[:, hs] + _dot_tn(v, kd[:, hs])
            on = _ln(o, lng_ref[...], lnb_ref[...])
            y_ref[rows, vs] = (on * sg_ref[rows, vs]).astype(y_ref.dtype)

    parts = [intra(ci) for ci in range(n_chunks)]
    for ci in range(n_chunks):
        carry_state(ci, *parts[ci])


def _gla(q, k, v, la, sg, lng, lnb, *, batch, seq, layer):
    n = q.shape[0]
    c = GLA_CHUNK
    tm = TILES["gla"]
    nt = seq // tm
    hk, hv = C_HEADS * C_DK, C_HEADS * C_DV
    tri = jnp.tril(jnp.ones((c, c), F32)).astype(BF16)
    lvl = _gla_level_matrix()
    kblk = pl.BlockSpec((tm, hk), lambda b, i: (b * nt + i, 0))
    vblk = pl.BlockSpec((tm, hv), lambda b, i: (b * nt + i, 0))
    full2 = lambda a: pl.BlockSpec(a.shape, lambda b, i: (0, 0))
    return pl.pallas_call(
        _gla_kernel,
        grid=(batch, nt),
        in_specs=[kblk, kblk, vblk, kblk, vblk, _layer_spec(lng, layer), _layer_spec(lnb, layer),
                  full2(tri), full2(lvl)],
        out_specs=vblk,
        out_shape=jax.ShapeDtypeStruct((n, hv), BF16),
        scratch_shapes=[pltpu.VMEM((C_HEADS, C_DV, C_DK), F32), pltpu.VMEM((tm // c, C_HEADS, c, c), F32)],
        compiler_params=pltpu.CompilerParams(dimension_semantics=("parallel", "arbitrary"),
                                             vmem_limit_bytes=VMEM_LIMIT),
        name="gla",
    )(q, k, v, la, sg, lng, lnb, tri, lvl)


def _post_ffn_kernel(x_ref, y1_ref, y2_ref, wo_ref, g1_ref, b1_ref, w1_ref, w2_ref, g2_ref, b2_ref,
                     o_ref, x1_ref, x1b_ref, acc_ref):
    i = pl.program_id(0)
    j = pl.program_id(1)
    nt = pl.num_programs(0) - 1
    half = y1_ref.shape[1]
    tm = x_ref.shape[0]
    blocks = [slice(r, r + FFN_ROW_BLOCK) for r in range(0, tm, FFN_ROW_BLOCK)]

    def mlp(x1b):
        h = _dot(x1b, w1_ref[...])
        return _dot(jnp.square(jnp.maximum(h, 0.0)).astype(BF16), w2_ref[...])

    def close_tile(rows):
        o_ref[rows, :] = _ln(ALPHA * x1_ref[rows, :] + acc_ref[rows, :], g2_ref[...], b2_ref[...])

    def out_proj(rows):
        return _dot(y1_ref[rows, :], wo_ref[0:half, :]) + _dot(y2_ref[rows, :], wo_ref[half:2 * half, :])

    def open_tile(rows, y):
        x1 = _ln(ALPHA * x_ref[rows, :] + y, g1_ref[...], b1_ref[...])
        x1_ref[rows, :] = x1
        x1b = x1.astype(BF16)
        x1b_ref[rows, :] = x1b
        acc_ref[rows, :] = mlp(x1b)

    def open_all(close_first):
        y = out_proj(blocks[0])
        for r, rows in enumerate(blocks):
            y_next = out_proj(blocks[r + 1]) if r + 1 < len(blocks) else None
            if close_first:
                close_tile(rows)
            open_tile(rows, y)
            y = y_next

    @pl.when((j == 0) & (i == 0))
    def _():
        open_all(False)

    @pl.when((j == 0) & (i > 0) & (i < nt))
    def _():
        open_all(True)

    @pl.when((j == 0) & (i == nt))
    def _():
        for rows in blocks:
            close_tile(rows)

    @pl.when((j > 0) & (i < nt))
    def _():
        acc_ref[...] += mlp(x1b_ref[...])


def _post_ffn(x, y1, y2, y2_col, wo, g1, b1, w1, w2, g2, b2, *, layer):
    n = x.shape[0]
    tm, tf = TILES["ffn_rows"], TILES["ffn_cols"]
    nt = n // tm
    half = D_MODEL // 2
    opened = lambda i: jnp.minimum(i, nt - 1)
    vec = pl.BlockSpec((None, 1, D_MODEL), lambda i, j: (layer, 0, 0))
    return pl.pallas_call(
        _post_ffn_kernel,
        grid=(nt + 1, D_FF // tf),
        in_specs=[pl.BlockSpec((tm, D_MODEL), lambda i, j: (opened(i), 0)),
                  pl.BlockSpec((tm, half), lambda i, j: (opened(i), 0)),
                  pl.BlockSpec((tm, half), lambda i, j: (opened(i), y2_col)),
                  pl.BlockSpec((None, D_MODEL, D_MODEL), lambda i, j: (layer, 0, 0)),
                  vec, vec,
                  pl.BlockSpec((None, D_MODEL, tf), lambda i, j: (layer, 0, j)),
                  pl.BlockSpec((None, tf, D_MODEL), lambda i, j: (layer, j, 0)),
                  vec, vec],
        out_specs=pl.BlockSpec((tm, D_MODEL), lambda i, j: (jnp.maximum(i - 1, 0), 0)),
        out_shape=jax.ShapeDtypeStruct((n, D_MODEL), F32),
        scratch_shapes=[pltpu.VMEM((tm, D_MODEL), F32), pltpu.VMEM((tm, D_MODEL), BF16),
                        pltpu.VMEM((tm, D_MODEL), F32)],
        compiler_params=pltpu.CompilerParams(dimension_semantics=("arbitrary", "arbitrary"),
                                             vmem_limit_bytes=VMEM_LIMIT),
        name="post_ffn",
    )(x, y1, y2, wo, g1, b1, w1, w2, g2, b2)


def _rope_inv_lanes():
    half = B_ROPE // 2
    inv = ROPE_THETA ** (-jnp.arange(half, dtype=F32) / half)
    lanes = jnp.zeros((1, HEAD_PAD), F32)
    lanes = lanes.at[0, B_NOPE:B_NOPE + half].set(inv)
    lanes = lanes.at[0, B_NOPE + half:B_NOPE + B_ROPE].set(inv)
    return lanes


def kernel(x, positions, ln1_g, ln1_b, ln2_g, ln2_b, w_in_even, a_w_s, a_b_s, a_ln_g, a_ln_b, b_q_norm, b_kv_norm, b_w_uq, b_w_ukv, w_out_even, w_in_odd, c_w_gate, c_b_gate, c_ln_g, c_ln_b, w_out_odd, w_ff1, w_ff2):
    bn, s, d = x.shape
    n = bn * s
    xf = x.reshape(n, d)
    pos = positions.reshape(n, 1).astype(jnp.int32)
    cos, sin = _rope_tables(pos, _rope_inv_lanes())
    n_even = w_in_even.shape[0]
    vec = lambda a: a.reshape(a.shape[0], 1, -1)

    o_kr = 2 * A_WIDTH + B_Q_RANK + B_KV_RANK
    pad = lambda width: jnp.zeros((n_even, d, width), F32)
    win_even = jnp.concatenate([w_in_even[:, :, :o_kr], pad(B_NOPE), w_in_even[:, :, o_kr:],
                                pad(HEAD_PAD - B_NOPE - B_ROPE)], axis=-1).astype(BF16)
    wuq = b_w_uq.reshape(n_even, B_Q_RANK, B_HEADS, B_NOPE + B_ROPE)
    wuq = jnp.pad(wuq, ((0, 0), (0, 0), (0, 0), (0, HEAD_PAD - B_NOPE - B_ROPE)))
    wuqt = wuq.reshape(n_even, B_Q_RANK, B_HEADS * HEAD_PAD).transpose(0, 2, 1).astype(BF16)
    wukv = b_w_ukv.reshape(n_even, B_KV_RANK, B_HEADS, B_NOPE + B_VDIM)
    wuk = jnp.pad(wukv[..., :B_NOPE], ((0, 0), (0, 0), (0, 0), (0, HEAD_PAD - B_NOPE)))
    wuk = wuk.reshape(n_even, B_KV_RANK, B_HEADS * HEAD_PAD).astype(BF16)
    wuvvt = wukv[..., B_NOPE:].reshape(n_even, B_KV_RANK, B_HEADS * B_VDIM).transpose(0, 2, 1).astype(BF16)
    causal = jnp.tril(jnp.ones((A_CHUNK, A_CHUNK), dtype=bool))
    ws = jnp.where(causal, a_w_s, 0.0).astype(BF16)
    bs = jnp.broadcast_to(a_b_s[..., None], a_b_s.shape + (A_GROUP_DIM,))
    even_params = (win_even, ws, bs, a_ln_g, a_ln_b, vec(b_q_norm), vec(b_kv_norm), wuqt, wuk, wuvvt)

    hk, hv = C_HEADS * C_DK, C_HEADS * C_DV
    win_odd = w_in_odd[:, :, :2 * hk + 2 * hv].astype(BF16)
    wzg = jnp.pad(w_in_odd[:, :, 2 * hk + 2 * hv:], ((0, 0), (0, 0), (0, 128 - C_GATE_RANK))).astype(BF16)
    wgate = jnp.pad(c_w_gate, ((0, 0), (0, 128 - C_GATE_RANK), (0, 0))).astype(BF16)
    odd_params = (win_odd, wzg, wgate, vec(c_b_gate))
    c_lng, c_lnb = vec(c_ln_g), vec(c_ln_b)

    w_out = jnp.stack([(w_out_even if layer % 2 == 0 else w_out_odd)[layer // 2]
                       for layer in range(DEPTH)]).astype(BF16)
    w1, w2 = w_ff1.astype(BF16), w_ff2.astype(BF16)
    g1, b1, g2, b2 = vec(ln1_g), vec(ln1_b), vec(ln2_g), vec(ln2_b)

    for layer in range(DEPTH):
        j = layer // 2
        if layer % 2 == 0:
            ya, qt, k, vt = _even_pre(xf, cos, sin, *even_params, layer=j)
            yb = _attn(qt, k, vt, batch=bn, seq=s)
            y1, y2, y2_col = ya, yb, 0
        else:
            q, k, v, sg, la = _odd_pre(xf, *odd_params, layer=j)
            y = _gla(q, k, v, la, sg, c_lng, c_lnb, batch=bn, seq=s, layer=j)
            y1, y2, y2_col = y, y, 1
        xf = _post_ffn(xf, y1, y2, y2_col, w_out, g1, b1, w1, w2, g2, b2, layer=layer)
    return xf.reshape(bn, s, d)
```

```python
import jax
import jax.numpy as jnp
from jax import lax
from jax.experimental import pallas as pl
from jax.experimental.pallas import tpu as pltpu

F32 = jnp.float32
BF16 = jnp.bfloat16

D_MODEL = 1024
DEPTH = 4

A_CHUNK = 128
A_GROUPS = 4
A_WIDTH = 512
A_GROUP_DIM = 128

B_HEADS = 8
B_NOPE = 64
B_ROPE = 32
B_VDIM = 64
B_Q_RANK = 384
B_KV_RANK = 256
ROPE_THETA = 10000.0
HEAD_PAD = 128
ATTN_SUM_ROWS = 16
ATTN_TILES_PER_STEP = 2

C_HEADS = 4
C_DK = 128
C_DV = 256
C_GATE_RANK = 16
C_GATE_TAU = 16.0
GLA_CHUNK = 256
GLA_FINE_LEVELS = (1, 2, 4)
GLA_BLOCK_LEVELS = (128, 64)

D_FF = 4 * D_MODEL
ALPHA = (2.0 * DEPTH) ** 0.25
LN_EPS = 1e-5
LOG2_E = 1.4426950408889634

VMEM_LIMIT = 56 * 1024 * 1024

TILES = {
    "attn": 512,
    "odd_pre": 1024,
    "rope": 2048,
    "gla": 1024,
    "ffn_rows": 1024,
    "ffn_cols": 1024,
}
PRE_ROW_BLOCK = 256
ODD_ROW_BLOCK = 256
FFN_ROW_BLOCK = 256


def _dot(a, b):
    return jnp.dot(a, b, preferred_element_type=F32)


def _dot_nt(a, b):
    return lax.dot_general(a, b, (((1,), (1,)), ((), ())), preferred_element_type=F32)


def _dot_tn(a, b):
    return lax.dot_general(a, b, (((0,), (0,)), ((), ())), preferred_element_type=F32)


def _ln(x, g, b):
    mu = jnp.mean(x, axis=-1, keepdims=True)
    xc = x - mu
    var = jnp.mean(xc * xc, axis=-1, keepdims=True)
    return xc * lax.rsqrt(var + LN_EPS) * g + b


def _rms(x, g):
    ms = jnp.mean(x * x, axis=-1, keepdims=True)
    return x * lax.rsqrt(ms + LN_EPS) * g


def _gelu(x):
    return 0.5 * x * (1.0 + jnp.tanh(0.7978845608028654 * (x + 0.044715 * (x * x * x))))


def _layer_spec(a, layer):
    zeros = (0,) * (a.ndim - 1)
    return pl.BlockSpec((None,) + a.shape[1:], lambda *_: (layer,) + zeros)


def _rope_table_kernel(pos_ref, inv_ref, cos_ref, sin_ref):
    ang = pos_ref[...].astype(F32) * inv_ref[...]
    cos_ref[...] = jnp.cos(ang)
    sin_ref[...] = jnp.sin(ang)


def _rope_tables(pos, inv_lane):
    n = pos.shape[0]
    tm = TILES["rope"]
    row = lambda w: pl.BlockSpec((tm, w), lambda i: (i, 0))
    table = jax.ShapeDtypeStruct((n, HEAD_PAD), F32)
    return pl.pallas_call(
        _rope_table_kernel,
        grid=(n // tm,),
        in_specs=[row(1), pl.BlockSpec(inv_lane.shape, lambda i: (0, 0))],
        out_specs=[row(HEAD_PAD), row(HEAD_PAD)],
        out_shape=[table, table],
        compiler_params=pltpu.CompilerParams(dimension_semantics=("parallel",)),
        name="rope_tables",
    )(pos, inv_lane)


def _even_pre_kernel(x_ref, cos_ref, sin_ref, win_ref,
                     ws_ref, bs_ref, alng_ref, alnb_ref, gq_ref, gkv_ref,
                     wuqt_ref, wuk_ref, wuvvt_ref,
                     ya_ref, qt_ref, k_ref, vt_ref):
    tm = x_ref.shape[0]
    o_cq = 2 * A_WIDTH
    o_ckv = o_cq + B_Q_RANK
    o_kr = o_ckv + B_KV_RANK
    lane = lax.broadcasted_iota(jnp.int32, (1, HEAD_PAD), 1)
    half = B_ROPE // 2
    scale = (B_NOPE + B_ROPE) ** -0.5 * LOG2_E

    def in_proj(rows):
        return _dot(x_ref[rows, :].astype(BF16), win_ref[...])

    def finish(rows, z):
        for g in range(A_GROUPS):
            lo = g * A_GROUP_DIM
            u = _gelu(z[:, lo:lo + A_GROUP_DIM])
            vv = _gelu(z[:, A_WIDTH + lo:A_WIDTH + lo + A_GROUP_DIM])
            vn = _ln(vv, alng_ref[g:g + 1, :], alnb_ref[g:g + 1, :]).astype(BF16)
            for r in range(0, rows.stop - rows.start, A_CHUNK):
                mixed = _dot(ws_ref[g], vn[r:r + A_CHUNK, :]) + bs_ref[g]
                ya_ref[rows.start + r:rows.start + r + A_CHUNK, lo:lo + A_GROUP_DIM] = (
                    u[r:r + A_CHUNK, :] * mixed).astype(ya_ref.dtype)

        cos = cos_ref[rows, :]
        sin = sin_ref[rows, :]
        s1 = jnp.where((lane >= B_NOPE) & (lane < B_NOPE + half), -sin, 0.0)
        s2 = jnp.where((lane >= B_NOPE + half) & (lane < B_NOPE + B_ROPE), sin, 0.0)
        ckvn = _rms(z[:, o_ckv:o_kr], gkv_ref[...]).astype(BF16)
        kn = _dot(ckvn, wuk_ref[...])
        kr = z[:, o_kr:o_kr + HEAD_PAD]
        kr = kr * cos + pltpu.roll(kr, HEAD_PAD - half, 1) * s1 + pltpu.roll(kr, half, 1) * s2
        for h in range(B_HEADS):
            lo = h * HEAD_PAD
            k_ref[rows, lo:lo + HEAD_PAD] = (kn[:, lo:lo + HEAD_PAD] + kr).astype(k_ref.dtype)

        cos_t = cos.T[B_NOPE:B_NOPE + half, :]
        sin_t = sin.T[B_NOPE:B_NOPE + half, :]
        cqn = _rms(z[:, o_cq:o_ckv], gq_ref[...]).astype(BF16)
        q_t = _dot_nt(wuqt_ref[...], cqn)
        for h in range(B_HEADS):
            lo = h * HEAD_PAD
            x1 = q_t[lo + B_NOPE:lo + B_NOPE + half, :]
            x2 = q_t[lo + B_NOPE + half:lo + B_NOPE + B_ROPE, :]
            q_h = jnp.concatenate([q_t[lo:lo + B_NOPE, :], x1 * cos_t - x2 * sin_t, x2 * cos_t + x1 * sin_t,
                                   q_t[lo + B_NOPE + B_ROPE:lo + HEAD_PAD, :]], axis=0)
            qt_ref[0, lo:lo + HEAD_PAD, rows] = (q_h * scale).astype(qt_ref.dtype)
        vt_ref[0, :, rows] = _dot_nt(wuvvt_ref[...], ckvn).astype(vt_ref.dtype)

    blocks = [slice(r, r + PRE_ROW_BLOCK) for r in range(0, tm, PRE_ROW_BLOCK)]
    z = in_proj(blocks[0])
    for r, rows in enumerate(blocks):
        z_next = in_proj(blocks[r + 1]) if r + 1 < len(blocks) else None
        finish(rows, z)
        z = z_next


def _even_pre(x, cos, sin, *params, layer):
    n = x.shape[0]
    tm = TILES["attn"]
    row = lambda w: pl.BlockSpec((tm, w), lambda i: (i, 0))
    hp = B_HEADS * HEAD_PAD
    return pl.pallas_call(
        _even_pre_kernel,
        grid=(n // tm,),
        in_specs=[row(D_MODEL), row(HEAD_PAD), row(HEAD_PAD)] + [_layer_spec(a, layer) for a in params],
        out_specs=[row(A_WIDTH),
                   pl.BlockSpec((1, hp, tm), lambda i: (i, 0, 0)),
                   row(hp),
                   pl.BlockSpec((1, B_HEADS * B_VDIM, tm), lambda i: (i, 0, 0))],
        out_shape=[jax.ShapeDtypeStruct((n, A_WIDTH), BF16),
                   jax.ShapeDtypeStruct((n // tm, hp, tm), BF16),
                   jax.ShapeDtypeStruct((n, hp), BF16),
                   jax.ShapeDtypeStruct((n // tm, B_HEADS * B_VDIM, tm), BF16)],
        compiler_params=pltpu.CompilerParams(dimension_semantics=("parallel",),
                                             vmem_limit_bytes=VMEM_LIMIT),
        name="even_pre",
    )(x, cos, sin, *params)


def _attn_kernel(qt_ref, k_ref, vt_ref, o_ref, sa_ref, sb_ref):
    t = qt_ref.shape[2]
    ones = jnp.ones((ATTN_SUM_ROWS, t), BF16)

    def one_tile(sub):
        qi = pl.program_id(2) * ATTN_TILES_PER_STEP + sub

        def scores(j, dst):
            start = pl.multiple_of(j * t, t)
            for hh in range(2):
                lo = hh * HEAD_PAD
                dst[hh] = _dot(k_ref[pl.ds(start, t), lo:lo + HEAD_PAD], qt_ref[sub, lo:lo + HEAD_PAD, :])

        def update(j, src, stats, masked):
            vt = vt_ref[j]
            new = []
            for hh in range(2):
                m, acc = stats[hh]
                s = src[hh]
                if masked:
                    key_ids = lax.broadcasted_iota(jnp.int32, (t, t), 0)
                    qry_ids = lax.broadcasted_iota(jnp.int32, (t, t), 1)
                    s = jnp.where(key_ids <= qry_ids, s, -jnp.inf)
                m_new = jnp.maximum(m, jnp.max(s, axis=0, keepdims=True))
                p = jnp.exp2(s - m_new).astype(BF16)
                a = jnp.exp2(m - m_new)
                v_ext = jnp.concatenate([vt[hh * B_VDIM:(hh + 1) * B_VDIM, :], ones], axis=0)
                acc = a * acc + _dot(v_ext, p)
                new.append((m_new, acc))
            return tuple(new)

        def two_blocks(i, stats):
            j = 2 * i
            scores(j + 1, sb_ref)
            stats = update(j, sa_ref, stats, False)
            scores(j + 2, sa_ref)
            return update(j + 1, sb_ref, stats, False)

        def four_blocks(i, stats):
            return two_blocks(2 * i + 1, two_blocks(2 * i, stats))

        def tail_odd(stats):
            scores(qi, sb_ref)
            stats = update(qi - 1, sa_ref, stats, False)
            return update(qi, sb_ref, stats, True)

        def tail_even(stats):
            return update(qi, sa_ref, stats, True)

        scores(0, sa_ref)
        init1 = (jnp.full((1, t), -jnp.inf, F32), jnp.zeros((B_VDIM + ATTN_SUM_ROWS, t), F32))
        stats = lax.fori_loop(0, qi // 4, four_blocks, (init1, init1))
        stats = lax.fori_loop(2 * (qi // 4), qi // 2, two_blocks, stats)
        stats = lax.cond(qi % 2 == 1, tail_odd, tail_even, stats)
        out_t = jnp.concatenate([acc[:B_VDIM] / acc[B_VDIM:B_VDIM + 1] for (_, acc) in stats], axis=0)
        o_ref[sub * t:(sub + 1) * t, :] = out_t.T.astype(o_ref.dtype)

    for sub in range(ATTN_TILES_PER_STEP):
        one_tile(sub)


def _attn(qt, k, vt, *, batch, seq):
    t = qt.shape[2]
    n = k.shape[0]
    nq = seq // t
    per = ATTN_TILES_PER_STEP
    steps = nq // per
    return pl.pallas_call(
        _attn_kernel,
        grid=(batch, B_HEADS // 2, steps),
        in_specs=[pl.BlockSpec((per, 2 * HEAD_PAD, t), lambda b, h, i: (b * steps + i, h, 0)),
                  pl.BlockSpec((seq, 2 * HEAD_PAD), lambda b, h, i: (b, h)),
                  pl.BlockSpec((nq, 2 * B_VDIM, t), lambda b, h, i: (b, h, 0))],
        out_specs=pl.BlockSpec((per * t, 2 * B_VDIM), lambda b, h, i: (b * steps + i, h)),
        out_shape=jax.ShapeDtypeStruct((n, B_HEADS * B_VDIM), BF16),
        scratch_shapes=[pltpu.VMEM((2, t, t), F32), pltpu.VMEM((2, t, t), F32)],
        compiler_params=pltpu.CompilerParams(dimension_semantics=("parallel", "parallel", "arbitrary"),
                                             vmem_limit_bytes=VMEM_LIMIT),
        name="attn",
    )(qt, k, vt)


def _odd_pre_kernel(x_ref, win_ref, wzg_ref, wgate_ref, bgate_ref,
                    q_ref, k_ref, v_ref, sg_ref, la_ref):
    hk, hv = C_HEADS * C_DK, C_HEADS * C_DV
    tm = x_ref.shape[0]

    def in_proj(rows):
        xb = x_ref[rows, :].astype(BF16)
        return _dot(xb, win_ref[...]), _dot(xb, wzg_ref[...])

    def finish(rows, z, zg):
        q_ref[rows, :] = z[:, :hk] * (C_DK ** -0.5)
        k_ref[rows, :] = z[:, hk:2 * hk]
        v_ref[rows, :] = z[:, 2 * hk:2 * hk + hv].astype(v_ref.dtype)
        g = z[:, 2 * hk + hv:]
        sg_ref[rows, :] = g * jax.nn.sigmoid(g)
        logits = _dot(zg.astype(BF16), wgate_ref[...]) + bgate_ref[...]
        la_ref[rows, :] = -(jnp.maximum(-logits, 0.0) + jnp.log1p(jnp.exp(-jnp.abs(logits)))) * (1.0 / C_GATE_TAU)

    blocks = [slice(r, r + ODD_ROW_BLOCK) for r in range(0, tm, ODD_ROW_BLOCK)]
    cur = in_proj(blocks[0])
    for r, rows in enumerate(blocks):
        nxt = in_proj(blocks[r + 1]) if r + 1 < len(blocks) else None
        finish(rows, *cur)
        cur = nxt


def _odd_pre(x, *params, layer):
    n = x.shape[0]
    tm = TILES["odd_pre"]
    row = lambda w: pl.BlockSpec((tm, w), lambda i: (i, 0))
    hk, hv = C_HEADS * C_DK, C_HEADS * C_DV
    return pl.pallas_call(
        _odd_pre_kernel,
        grid=(n // tm,),
        in_specs=[row(D_MODEL)] + [_layer_spec(a, layer) for a in params],
        out_specs=[row(hk), row(hk), row(hv), row(hv), row(hk)],
        out_shape=[jax.ShapeDtypeStruct((n, hk), F32),
                   jax.ShapeDtypeStruct((n, hk), F32),
                   jax.ShapeDtypeStruct((n, hv), BF16),
                   jax.ShapeDtypeStruct((n, hv), F32),
                   jax.ShapeDtypeStruct((n, hk), F32)],
        compiler_params=pltpu.CompilerParams(dimension_semantics=("parallel",),
                                             vmem_limit_bytes=VMEM_LIMIT),
        name="odd_pre",
    )(x, *params)


def _gla_level_matrix():
    c = GLA_CHUNK
    t = jnp.arange(c)[:, None]
    u = jnp.arange(c)[None, :]
    mats = []
    for m in GLA_FINE_LEVELS:
        r = t % (2 * m)
        p = t - r + m - 1
        upper = (r >= m) & (u > p) & (u <= t)
        lower = (r < m) & (u > t) & (u <= p)
        mats.append(upper | lower)
    return jnp.concatenate(mats, axis=0).astype(BF16)


def _split_bf16(x, pieces):
    out = []
    for _ in range(pieces - 1):
        hi = x.astype(BF16)
        out.append(hi)
        x = x - hi.astype(F32)
    out.append(x.astype(BF16))
    return out


def _gla_kernel(q_ref, k_ref, v_ref, la_ref, sg_ref, lng_ref, lnb_ref, tri_ref, lvl_ref,
                y_ref, st_ref, a_scr):
    c = GLA_CHUNK
    hk = C_HEADS * C_DK
    n_chunks = q_ref.shape[0] // c

    @pl.when(pl.program_id(1) == 0)
    def _():
        st_ref[...] = jnp.zeros_like(st_ref)

    row = lax.broadcasted_iota(jnp.int32, (c, 1), 0)
    pair = lax.broadcasted_iota(jnp.int32, (c, c), 0) ^ lax.broadcasted_iota(jnp.int32, (c, c), 1)

    def intra(ci):
        rows = slice(ci * c, (ci + 1) * c)
        q = q_ref[rows, :]
        k = k_ref[rows, :]
        g = _split_bf16(la_ref[rows, :], 3)
        tri = tri_ref[...]
        b = _dot(tri, g[0]) + _dot(tri, g[1]) + _dot(tri, g[2])
        fine = _dot(lvl_ref[...], g[0]) + _dot(lvl_ref[...], g[1])

        qb = q.astype(BF16)
        kb = k.astype(BF16)
        for h in range(C_HEADS):
            hs = slice(h * C_DK, (h + 1) * C_DK)
            a_scr[ci, h] = jnp.where(pair == 0, _dot_nt(qb[:, hs], kb[:, hs]), 0.0)

        for m in GLA_BLOCK_LEVELS:
            for base in range(0, c, 2 * m):
                lower, upper = slice(base, base + m), slice(base + m, base + 2 * m)
                b_ref_row = b[base + m - 1:base + m, :]
                qe = (q[upper, :] * jnp.exp(b[upper, :] - b_ref_row)).astype(BF16)
                ke = (k[lower, :] * jnp.exp(b_ref_row - b[lower, :])).astype(BF16)
                for h in range(C_HEADS):
                    hs = slice(h * C_DK, (h + 1) * C_DK)
                    a_scr[ci, h, upper, lower] = _dot_nt(qe[:, hs], ke[:, hs])

        m = 1
        while m < min(GLA_BLOCK_LEVELS):
            if m in GLA_FINE_LEVELS:
                i = GLA_FINE_LEVELS.index(m)
                d = fine[i * c:(i + 1) * c, :]
            else:
                b3 = b.reshape(c // (2 * m), 2 * m, hk)
                d3 = b3 - b3[:, m - 1:m, :]
                sub = lax.broadcasted_iota(jnp.int32, (1, 2 * m, 1), 1)
                d = jnp.where(sub >= m, d3, -d3).reshape(c, hk)
            e = jnp.exp(d)
            upper = (row & m) != 0
            qe = jnp.where(upper, q * e, 0.0).astype(BF16)
            ke = jnp.where(upper, 0.0, k * e).astype(BF16)
            level = (pair >= m) & (pair < 2 * m)
            for h in range(C_HEADS):
                hs = slice(h * C_DK, (h + 1) * C_DK)
                a_scr[ci, h] = jnp.where(level, _dot_nt(qe[:, hs], ke[:, hs]), a_scr[ci, h])
            m *= 2

        b_last = b[c - 1:c, :]
        qs = (q * jnp.exp(b)).astype(BF16)
        kd = (k * jnp.exp(b_last - b)).astype(BF16)
        return qs, kd, jnp.exp(b_last)

    def carry_state(ci, qs, kd, decay):
        rows = slice(ci * c, (ci + 1) * c)
        for h in range(C_HEADS):
            hs = slice(h * C_DK, (h + 1) * C_DK)
            vs = slice(h * C_DV, (h + 1) * C_DV)
            v = v_ref[rows, vs]
            st = st_ref[h]
            o = _dot(a_scr[ci, h].astype(BF16), v) + _dot_nt(qs[:, hs], st.astype(BF16))
            st_ref[h] = st * decay[:, hs] + _dot_tn(v, kd[:, hs])
            on = _ln(o, lng_ref[...], lnb_ref[...])
            y_ref[rows, vs] = (on * sg_ref[rows, vs]).astype(y_ref.dtype)

    parts = [intra(ci) for ci in range(n_chunks)]
    for ci in range(n_chunks):
        carry_state(ci, *parts[ci])


def _gla(q, k, v, la, sg, lng, lnb, *, batch, seq, layer):
    n = q.shape[0]
    c = GLA_CHUNK
    tm = TILES["gla"]
    nt = seq // tm
    hk, hv = C_HEADS * C_DK, C_HEADS * C_DV
    tri = jnp.tril(jnp.ones((c, c), F32)).astype(BF16)
    lvl = _gla_level_matrix()
    kblk = pl.BlockSpec((tm, hk), lambda b, i: (b * nt + i, 0))
    vblk = pl.BlockSpec((tm, hv), lambda b, i: (b * nt + i, 0))
    full2 = lambda a: pl.BlockSpec(a.shape, lambda b, i: (0, 0))
    return pl.pallas_call(
        _gla_kernel,
        grid=(batch, nt),
        in_specs=[kblk, kblk, vblk, kblk, vblk, _layer_spec(lng, layer), _layer_spec(lnb, layer),
                  full2(tri), full2(lvl)],
        out_specs=vblk,
        out_shape=jax.ShapeDtypeStruct((n, hv), BF16),
        scratch_shapes=[pltpu.VMEM((C_HEADS, C_DV, C_DK), F32), pltpu.VMEM((tm // c, C_HEADS, c, c), F32)],
        compiler_params=pltpu.CompilerParams(dimension_semantics=("parallel", "arbitrary"),
                                             vmem_limit_bytes=VMEM_LIMIT),
        name="gla",
    )(q, k, v, la, sg, lng, lnb, tri, lvl)


def _post_ffn_kernel(x_ref, y1_ref, y2_ref, wo_ref, g1_ref, b1_ref, w1_ref, w2_ref, g2_ref, b2_ref,
                     o_ref, x1_ref, x1b_ref, acc_ref):
    i = pl.program_id(0)
    j = pl.program_id(1)
    nt = pl.num_programs(0) - 1
    half = y1_ref.shape[1]
    tm = x_ref.shape[0]
    blocks = [slice(r, r + FFN_ROW_BLOCK) for r in range(0, tm, FFN_ROW_BLOCK)]

    def mlp(x1b):
        h = _dot(x1b, w1_ref[...].astype(BF16))
        return _dot(jnp.square(jnp.maximum(h, 0.0)).astype(BF16), w2_ref[...].astype(BF16))

    def close_tile(rows):
        o_ref[rows, :] = _ln(ALPHA * x1_ref[rows, :] + acc_ref[rows, :], g2_ref[...], b2_ref[...])

    def out_proj(rows):
        return _dot(y1_ref[rows, :], wo_ref[0:half, :]) + _dot(y2_ref[rows, :], wo_ref[half:2 * half, :])

    def open_tile(rows, y):
        x1 = _ln(ALPHA * x_ref[rows, :] + y, g1_ref[...], b1_ref[...])
        x1_ref[rows, :] = x1
        x1b = x1.astype(BF16)
        x1b_ref[rows, :] = x1b
        acc_ref[rows, :] = mlp(x1b)

    def open_all(close_first):
        y = out_proj(blocks[0])
        for r, rows in enumerate(blocks):
            y_next = out_proj(blocks[r + 1]) if r + 1 < len(blocks) else None
            if close_first:
                close_tile(rows)
            open_tile(rows, y)
            y = y_next

    @pl.when((j == 0) & (i == 0))
    def _():
        open_all(False)

    @pl.when((j == 0) & (i > 0) & (i < nt))
    def _():
        open_all(True)

    @pl.when((j == 0) & (i == nt))
    def _():
        for rows in blocks:
            close_tile(rows)

    @pl.when((j > 0) & (i < nt))
    def _():
        acc_ref[...] += mlp(x1b_ref[...])


def _post_ffn(x, y1, y2, y2_col, wo, g1, b1, w1, w2, g2, b2, *, layer):
    n = x.shape[0]
    tm, tf = TILES["ffn_rows"], TILES["ffn_cols"]
    nt = n // tm
    half = D_MODEL // 2
    opened = lambda i: jnp.minimum(i, nt - 1)
    vec = pl.BlockSpec((None, 1, D_MODEL), lambda i, j: (layer, 0, 0))
    return pl.pallas_call(
        _post_ffn_kernel,
        grid=(nt + 1, D_FF // tf),
        in_specs=[pl.BlockSpec((tm, D_MODEL), lambda i, j: (opened(i), 0)),
                  pl.BlockSpec((tm, half), lambda i, j: (opened(i), 0)),
                  pl.BlockSpec((tm, half), lambda i, j: (opened(i), y2_col)),
                  pl.BlockSpec((None, D_MODEL, D_MODEL), lambda i, j: (layer, 0, 0)),
                  vec, vec,
                  pl.BlockSpec((None, D_MODEL, tf), lambda i, j: (layer, 0, j)),
                  pl.BlockSpec((None, tf, D_MODEL), lambda i, j: (layer, j, 0)),
                  vec, vec],
        out_specs=pl.BlockSpec((tm, D_MODEL), lambda i, j: (jnp.maximum(i - 1, 0), 0)),
        out_shape=jax.ShapeDtypeStruct((n, D_MODEL), F32),
        scratch_shapes=[pltpu.VMEM((tm, D_MODEL), F32), pltpu.VMEM((tm, D_MODEL), BF16),
                        pltpu.VMEM((tm, D_MODEL), F32)],
        compiler_params=pltpu.CompilerParams(dimension_semantics=("arbitrary", "arbitrary"),
                                             vmem_limit_bytes=VMEM_LIMIT),
        name="post_ffn",
    )(x, y1, y2, wo, g1, b1, w1, w2, g2, b2)


def _rope_inv_lanes():
    half = B_ROPE // 2
    inv = ROPE_THETA ** (-jnp.arange(half, dtype=F32) / half)
    lanes = jnp.zeros((1, HEAD_PAD), F32)
    lanes = lanes.at[0, B_NOPE:B_NOPE + half].set(inv)
    lanes = lanes.at[0, B_NOPE + half:B_NOPE + B_ROPE].set(inv)
    return lanes


def kernel(x, positions, ln1_g, ln1_b, ln2_g, ln2_b, w_in_even, a_w_s, a_b_s, a_ln_g, a_ln_b, b_q_norm, b_kv_norm, b_w_uq, b_w_ukv, w_out_even, w_in_odd, c_w_gate, c_b_gate, c_ln_g, c_ln_b, w_out_odd, w_ff1, w_ff2):
    bn, s, d = x.shape
    n = bn * s
    xf = x.reshape(n, d)
    pos = positions.reshape(n, 1).astype(jnp.int32)
    cos, sin = _rope_tables(pos, _rope_inv_lanes())
    n_even = w_in_even.shape[0]
    vec = lambda a: a.reshape(a.shape[0], 1, -1)

    o_kr = 2 * A_WIDTH + B_Q_RANK + B_KV_RANK
    pad = lambda width: jnp.zeros((n_even, d, width), F32)
    win_even = jnp.concatenate([w_in_even[:, :, :o_kr], pad(B_NOPE), w_in_even[:, :, o_kr:],
                                pad(HEAD_PAD - B_NOPE - B_ROPE)], axis=-1).astype(BF16)
    wuq = b_w_uq.reshape(n_even, B_Q_RANK, B_HEADS, B_NOPE + B_ROPE)
    wuq = jnp.pad(wuq, ((0, 0), (0, 0), (0, 0), (0, HEAD_PAD - B_NOPE - B_ROPE)))
    wuqt = wuq.reshape(n_even, B_Q_RANK, B_HEADS * HEAD_PAD).transpose(0, 2, 1).astype(BF16)
    wukv = b_w_ukv.reshape(n_even, B_KV_RANK, B_HEADS, B_NOPE + B_VDIM)
    wuk = jnp.pad(wukv[..., :B_NOPE], ((0, 0), (0, 0), (0, 0), (0, HEAD_PAD - B_NOPE)))
    wuk = wuk.reshape(n_even, B_KV_RANK, B_HEADS * HEAD_PAD).astype(BF16)
    wuvvt = wukv[..., B_NOPE:].reshape(n_even, B_KV_RANK, B_HEADS * B_VDIM).transpose(0, 2, 1).astype(BF16)
    causal = jnp.tril(jnp.ones((A_CHUNK, A_CHUNK), dtype=bool))
    ws = jnp.where(causal, a_w_s, 0.0).astype(BF16)
    bs = jnp.broadcast_to(a_b_s[..., None], a_b_s.shape + (A_GROUP_DIM,))
    even_params = (win_even, ws, bs, a_ln_g, a_ln_b, vec(b_q_norm), vec(b_kv_norm), wuqt, wuk, wuvvt)

    hk, hv = C_HEADS * C_DK, C_HEADS * C_DV
    win_odd = w_in_odd[:, :, :2 * hk + 2 * hv].astype(BF16)
    wzg = jnp.pad(w_in_odd[:, :, 2 * hk + 2 * hv:], ((0, 0), (0, 0), (0, 128 - C_GATE_RANK))).astype(BF16)
    wgate = jnp.pad(c_w_gate, ((0, 0), (0, 128 - C_GATE_RANK), (0, 0))).astype(BF16)
    odd_params = (win_odd, wzg, wgate, vec(c_b_gate))
    c_lng, c_lnb = vec(c_ln_g), vec(c_ln_b)

    w_out = jnp.stack([(w_out_even if layer % 2 == 0 else w_out_odd)[layer // 2]
                       for layer in range(DEPTH)]).astype(BF16)
    w1, w2 = w_ff1, w_ff2
    g1, b1, g2, b2 = vec(ln1_g), vec(ln1_b), vec(ln2_g), vec(ln2_b)

    for layer in range(DEPTH):
        j = layer // 2
        if layer % 2 == 0:
            ya, qt, k, vt = _even_pre(xf, cos, sin, *even_params, layer=j)
            yb = _attn(qt, k, vt, batch=bn, seq=s)
            y1, y2, y2_col = ya, yb, 0
        else:
            q, k, v, sg, la = _odd_pre(xf, *odd_params, layer=j)
            y = _gla(q, k, v, la, sg, c_lng, c_lnb, batch=bn, seq=s, layer=j)
            y1, y2, y2_col = y, y, 1
        xf = _post_ffn(xf, y1, y2, y2_col, w_out, g1, b1, w1, w2, g2, b2, layer=layer)
    return xf.reshape(bn, s, d)
```

```python
import jax
import jax.numpy as jnp
from jax import lax
from jax.experimental import pallas as pl
from jax.experimental.pallas import tpu as pltpu

F32 = jnp.float32
BF16 = jnp.bfloat16

D_MODEL = 1024
DEPTH = 4

A_CHUNK = 128
A_GROUPS = 4
A_WIDTH = 512
A_GROUP_DIM = 128

B_HEADS = 8
B_NOPE = 64
B_ROPE = 32
B_VDIM = 64
B_Q_RANK = 384
B_KV_RANK = 256
ROPE_THETA = 10000.0
HEAD_PAD = 128
ATTN_SUM_ROWS = 16

C_HEADS = 4
C_DK = 128
C_DV = 256
C_GATE_RANK = 16
C_GATE_TAU = 16.0
GLA_CHUNK = 256
GLA_FINE_LEVELS = (1, 2, 4)
GLA_BLOCK_LEVELS = (128, 64)

D_FF = 4 * D_MODEL
ALPHA = (2.0 * DEPTH) ** 0.25
LN_EPS = 1e-5
LOG2_E = 1.4426950408889634

VMEM_LIMIT = 56 * 1024 * 1024

TILES = {
    "attn": 512,
    "odd_pre": 1024,
    "rope": 2048,
    "gla": 1024,
    "ffn_rows": 1024,
    "ffn_cols": 1024,
}
PRE_ROW_BLOCK = 256
ODD_ROW_BLOCK = 256
FFN_ROW_BLOCK = 256


def _dot(a, b):
    return jnp.dot(a, b, preferred_element_type=F32)


def _dot_nt(a, b):
    return lax.dot_general(a, b, (((1,), (1,)), ((), ())), preferred_element_type=F32)


def _dot_tn(a, b):
    return lax.dot_general(a, b, (((0,), (0,)), ((), ())), preferred_element_type=F32)


def _ln(x, g, b):
    mu = jnp.mean(x, axis=-1, keepdims=True)
    xc = x - mu
    var = jnp.mean(xc * xc, axis=-1, keepdims=True)
    return xc * lax.rsqrt(var + LN_EPS) * g + b


def _rms(x, g):
    ms = jnp.mean(x * x, axis=-1, keepdims=True)
    return x * lax.rsqrt(ms + LN_EPS) * g


def _gelu(x):
    return 0.5 * x * (1.0 + jnp.tanh(0.7978845608028654 * (x + 0.044715 * (x * x * x))))


def _layer_spec(a, layer, cols=None):
    zeros = (0,) * (a.ndim - 1)
    shape = a.shape[1:] if cols is None else a.shape[1:-1] + (cols,)
    return pl.BlockSpec((None,) + shape, lambda *_: (layer,) + zeros)


def _rope_table_kernel(pos_ref, inv_ref, cos_ref, sin_ref):
    ang = pos_ref[...].astype(F32) * inv_ref[...]
    cos_ref[...] = jnp.cos(ang)
    sin_ref[...] = jnp.sin(ang)


def _rope_tables(pos, inv_lane):
    n = pos.shape[0]
    tm = TILES["rope"]
    row = lambda w: pl.BlockSpec((tm, w), lambda i: (i, 0))
    table = jax.ShapeDtypeStruct((n, HEAD_PAD), F32)
    return pl.pallas_call(
        _rope_table_kernel,
        grid=(n // tm,),
        in_specs=[row(1), pl.BlockSpec(inv_lane.shape, lambda i: (0, 0))],
        out_specs=[row(HEAD_PAD), row(HEAD_PAD)],
        out_shape=[table, table],
        compiler_params=pltpu.CompilerParams(dimension_semantics=("parallel",)),
        name="rope_tables",
    )(pos, inv_lane)


def _even_pre_kernel(x_ref, cos_ref, sin_ref, win_ref,
                     ws_ref, bs_ref, alng_ref, alnb_ref, gq_ref, gkv_ref,
                     wuqt_ref, wuk_ref, wuvvt_ref,
                     ya_ref, qt_ref, k_ref, vt_ref):
    tm = x_ref.shape[0]
    o_cq = 2 * A_WIDTH
    o_ckv = o_cq + B_Q_RANK
    o_kr = o_ckv + B_KV_RANK
    lane = lax.broadcasted_iota(jnp.int32, (1, HEAD_PAD), 1)
    half = B_ROPE // 2
    scale = (B_NOPE + B_ROPE) ** -0.5 * LOG2_E

    def in_proj(rows):
        return _dot(x_ref[rows, :].astype(BF16), win_ref[...])

    def finish(rows, z):
        for g in range(A_GROUPS):
            lo = g * A_GROUP_DIM
            u = _gelu(z[:, lo:lo + A_GROUP_DIM])
            vv = _gelu(z[:, A_WIDTH + lo:A_WIDTH + lo + A_GROUP_DIM])
            vn = _ln(vv, alng_ref[g:g + 1, :], alnb_ref[g:g + 1, :]).astype(BF16)
            for r in range(0, rows.stop - rows.start, A_CHUNK):
                mixed = _dot(ws_ref[g], vn[r:r + A_CHUNK, :]) + bs_ref[g]
                ya_ref[rows.start + r:rows.start + r + A_CHUNK, lo:lo + A_GROUP_DIM] = (
                    u[r:r + A_CHUNK, :] * mixed).astype(ya_ref.dtype)

        cos = cos_ref[rows, :]
        sin = sin_ref[rows, :]
        s1 = jnp.where((lane >= B_NOPE) & (lane < B_NOPE + half), -sin, 0.0)
        s2 = jnp.where((lane >= B_NOPE + half) & (lane < B_NOPE + B_ROPE), sin, 0.0)
        ckvn = _rms(z[:, o_ckv:o_kr], gkv_ref[...]).astype(BF16)
        kn = _dot(ckvn, wuk_ref[...])
        kr = z[:, o_kr:o_kr + HEAD_PAD]
        kr = kr * cos + pltpu.roll(kr, HEAD_PAD - half, 1) * s1 + pltpu.roll(kr, half, 1) * s2
        for h in range(B_HEADS):
            lo = h * HEAD_PAD
            k_ref[rows, lo:lo + HEAD_PAD] = (kn[:, lo:lo + HEAD_PAD] + kr).astype(k_ref.dtype)

        cos_t = cos.T[B_NOPE:B_NOPE + half, :]
        sin_t = sin.T[B_NOPE:B_NOPE + half, :]
        cqn = _rms(z[:, o_cq:o_ckv], gq_ref[...]).astype(BF16)
        q_t = _dot_nt(wuqt_ref[...], cqn)
        for h in range(B_HEADS):
            lo = h * HEAD_PAD
            x1 = q_t[lo + B_NOPE:lo + B_NOPE + half, :]
            x2 = q_t[lo + B_NOPE + half:lo + B_NOPE + B_ROPE, :]
            q_h = jnp.concatenate([q_t[lo:lo + B_NOPE, :], x1 * cos_t - x2 * sin_t, x2 * cos_t + x1 * sin_t,
                                   q_t[lo + B_NOPE + B_ROPE:lo + HEAD_PAD, :]], axis=0)
            qt_ref[0, lo:lo + HEAD_PAD, rows] = (q_h * scale).astype(qt_ref.dtype)
        vt_ref[0, :, rows] = _dot_nt(wuvvt_ref[...], ckvn).astype(vt_ref.dtype)

    blocks = [slice(r, r + PRE_ROW_BLOCK) for r in range(0, tm, PRE_ROW_BLOCK)]
    z = in_proj(blocks[0])
    for r, rows in enumerate(blocks):
        z_next = in_proj(blocks[r + 1]) if r + 1 < len(blocks) else None
        finish(rows, z)
        z = z_next


def _even_pre(x, cos, sin, *params, layer):
    n = x.shape[0]
    tm = TILES["attn"]
    row = lambda w: pl.BlockSpec((tm, w), lambda i: (i, 0))
    hp = B_HEADS * HEAD_PAD
    return pl.pallas_call(
        _even_pre_kernel,
        grid=(n // tm,),
        in_specs=[row(D_MODEL), row(HEAD_PAD), row(HEAD_PAD)] + [_layer_spec(a, layer) for a in params],
        out_specs=[row(A_WIDTH),
                   pl.BlockSpec((1, hp, tm), lambda i: (i, 0, 0)),
                   row(hp),
                   pl.BlockSpec((1, B_HEADS * B_VDIM, tm), lambda i: (i, 0, 0))],
        out_shape=[jax.ShapeDtypeStruct((n, A_WIDTH), BF16),
                   jax.ShapeDtypeStruct((n // tm, hp, tm), BF16),
                   jax.ShapeDtypeStruct((n, hp), BF16),
                   jax.ShapeDtypeStruct((n // tm, B_HEADS * B_VDIM, tm), BF16)],
        compiler_params=pltpu.CompilerParams(dimension_semantics=("parallel",),
                                             vmem_limit_bytes=VMEM_LIMIT),
        name="even_pre",
    )(x, cos, sin, *params)


def _attn_kernel(qt_ref, k_ref, vt_ref, o_ref, sa_ref, sb_ref):
    t = qt_ref.shape[2]
    ones = jnp.ones((ATTN_SUM_ROWS, t), BF16)

    def one_tile(qi, _):

        def scores(j, dst):
            start = pl.multiple_of(j * t, t)
            for hh in range(2):
                lo = hh * HEAD_PAD
                dst[hh] = _dot(k_ref[pl.ds(start, t), lo:lo + HEAD_PAD], qt_ref[qi, lo:lo + HEAD_PAD, :])

        def update(j, src, stats, masked):
            vt = vt_ref[j]
            new = []
            for hh in range(2):
                m, acc = stats[hh]
                s = src[hh]
                if masked:
                    key_ids = lax.broadcasted_iota(jnp.int32, (t, t), 0)
                    qry_ids = lax.broadcasted_iota(jnp.int32, (t, t), 1)
                    s = jnp.where(key_ids <= qry_ids, s, -jnp.inf)
                m_new = jnp.maximum(m, jnp.max(s, axis=0, keepdims=True))
                p = jnp.exp2(s - m_new).astype(BF16)
                a = jnp.exp2(m - m_new)
                v_ext = jnp.concatenate([vt[hh * B_VDIM:(hh + 1) * B_VDIM, :], ones], axis=0)
                acc = a * acc + _dot(v_ext, p)
                new.append((m_new, acc))
            return tuple(new)

        def two_blocks(i, stats):
            j = 2 * i
            scores(j + 1, sb_ref)
            stats = update(j, sa_ref, stats, False)
            scores(j + 2, sa_ref)
            return update(j + 1, sb_ref, stats, False)

        def four_blocks(i, stats):
            return two_blocks(2 * i + 1, two_blocks(2 * i, stats))

        def tail_odd(stats):
            scores(qi, sb_ref)
            stats = update(qi - 1, sa_ref, stats, False)
            return update(qi, sb_ref, stats, True)

        def tail_even(stats):
            return update(qi, sa_ref, stats, True)

        scores(0, sa_ref)
        init1 = (jnp.full((1, t), -jnp.inf, F32), jnp.zeros((B_VDIM + ATTN_SUM_ROWS, t), F32))
        def eight_blocks(i, stats):
            return four_blocks(2 * i + 1, four_blocks(2 * i, stats))

        stats = lax.fori_loop(0, qi // 8, eight_blocks, (init1, init1))
        stats = lax.fori_loop(2 * (qi // 8), qi // 4, four_blocks, stats)
        stats = lax.fori_loop(2 * (qi // 4), qi // 2, two_blocks, stats)
        stats = lax.cond(qi % 2 == 1, tail_odd, tail_even, stats)
        out_t = jnp.concatenate([acc[:B_VDIM] / acc[B_VDIM:B_VDIM + 1] for (_, acc) in stats], axis=0)
        o_ref[pl.ds(pl.multiple_of(qi * t, t), t), :] = out_t.T.astype(o_ref.dtype)
        return 0

    lax.fori_loop(0, qt_ref.shape[0], one_tile, 0)


def _attn(qt, k, vt, *, batch, seq):
    t = qt.shape[2]
    n = k.shape[0]
    nq = seq // t
    return pl.pallas_call(
        _attn_kernel,
        grid=(batch, B_HEADS // 2),
        in_specs=[pl.BlockSpec((nq, 2 * HEAD_PAD, t), lambda b, h: (b, h, 0)),
                  pl.BlockSpec((seq, 2 * HEAD_PAD), lambda b, h: (b, h)),
                  pl.BlockSpec((nq, 2 * B_VDIM, t), lambda b, h: (b, h, 0))],
        out_specs=pl.BlockSpec((seq, 2 * B_VDIM), lambda b, h: (b, h)),
        out_shape=jax.ShapeDtypeStruct((n, B_HEADS * B_VDIM), BF16),
        scratch_shapes=[pltpu.VMEM((2, t, t), F32), pltpu.VMEM((2, t, t), F32)],
        compiler_params=pltpu.CompilerParams(dimension_semantics=("parallel", "parallel"),
                                             vmem_limit_bytes=VMEM_LIMIT),
        name="attn",
    )(qt, k, vt)


def _odd_pre_kernel(x_ref, win_ref, wzg_ref, wgate_ref, bgate_ref,
                    q_ref, k_ref, v_ref, sg_ref, la_ref):
    hk, hv = C_HEADS * C_DK, C_HEADS * C_DV
    tm = x_ref.shape[0]

    def in_proj(rows):
        xb = x_ref[rows, :].astype(BF16)
        return _dot(xb, win_ref[...]), _dot(xb, wzg_ref[...])

    def finish(rows, z, zg):
        q_ref[rows, :] = z[:, :hk] * (C_DK ** -0.5)
        k_ref[rows, :] = z[:, hk:2 * hk]
        v_ref[rows, :] = z[:, 2 * hk:2 * hk + hv].astype(v_ref.dtype)
        g = z[:, 2 * hk + hv:]
        sg_ref[rows, :] = g * jax.nn.sigmoid(g)
        logits = _dot(zg.astype(BF16), wgate_ref[...]) + bgate_ref[...]
        la_ref[rows, :] = -(jnp.maximum(-logits, 0.0) + jnp.log1p(jnp.exp(-jnp.abs(logits)))) * (1.0 / C_GATE_TAU)

    blocks = [slice(r, r + ODD_ROW_BLOCK) for r in range(0, tm, ODD_ROW_BLOCK)]
    cur = in_proj(blocks[0])
    for r, rows in enumerate(blocks):
        nxt = in_proj(blocks[r + 1]) if r + 1 < len(blocks) else None
        finish(rows, *cur)
        cur = nxt


def _odd_pre(x, *params, layer):
    n = x.shape[0]
    tm = TILES["odd_pre"]
    row = lambda w: pl.BlockSpec((tm, w), lambda i: (i, 0))
    hk, hv = C_HEADS * C_DK, C_HEADS * C_DV
    return pl.pallas_call(
        _odd_pre_kernel,
        grid=(n // tm,),
        in_specs=[row(D_MODEL), _layer_spec(params[0], layer, cols=2 * hk + 2 * hv)]
                 + [_layer_spec(a, layer) for a in params[1:]],
        out_specs=[row(hk), row(hk), row(hv), row(hv), row(hk)],
        out_shape=[jax.ShapeDtypeStruct((n, hk), F32),
                   jax.ShapeDtypeStruct((n, hk), F32),
                   jax.ShapeDtypeStruct((n, hv), BF16),
                   jax.ShapeDtypeStruct((n, hv), F32),
                   jax.ShapeDtypeStruct((n, hk), F32)],
        compiler_params=pltpu.CompilerParams(dimension_semantics=("parallel",),
                                             vmem_limit_bytes=VMEM_LIMIT),
        name="odd_pre",
    )(x, *params)


def _gla_level_matrix():
    c = GLA_CHUNK
    t = jnp.arange(c)[:, None]
    u = jnp.arange(c)[None, :]
    mats = []
    for m in GLA_FINE_LEVELS:
        r = t % (2 * m)
        p = t - r + m - 1
        upper = (r >= m) & (u > p) & (u <= t)
        lower = (r < m) & (u > t) & (u <= p)
        mats.append(upper | lower)
    return jnp.concatenate(mats, axis=0).astype(BF16)


def _split_bf16(x, pieces):
    out = []
    for _ in range(pieces - 1):
        hi = x.astype(BF16)
        out.append(hi)
        x = x - hi.astype(F32)
    out.append(x.astype(BF16))
    return out


def _gla_kernel(q_ref, k_ref, v_ref, la_ref, sg_ref, lng_ref, lnb_ref, tri_ref, lvl_ref,
                y_ref, st_ref, a_scr):
    c = GLA_CHUNK
    hk = C_HEADS * C_DK
    n_chunks = q_ref.shape[0] // c

    @pl.when(pl.program_id(1) == 0)
    def _():
        st_ref[...] = jnp.zeros_like(st_ref)

    row = lax.broadcasted_iota(jnp.int32, (c, 1), 0)
    pair = lax.broadcasted_iota(jnp.int32, (c, c), 0) ^ lax.broadcasted_iota(jnp.int32, (c, c), 1)

    def intra(ci):
        rows = slice(ci * c, (ci + 1) * c)
        q = q_ref[rows, :]
        k = k_ref[rows, :]
        g = _split_bf16(la_ref[rows, :], 3)
        tri = tri_ref[...]
        b = _dot(tri, g[0]) + _dot(tri, g[1]) + _dot(tri, g[2])
        fine = _dot(lvl_ref[...], g[0]) + _dot(lvl_ref[...], g[1])

        qb = q.astype(BF16)
        kb = k.astype(BF16)
        for h in range(C_HEADS):
            hs = slice(h * C_DK, (h + 1) * C_DK)
            a_scr[ci, h] = jnp.where(pair == 0, _dot_nt(qb[:, hs], kb[:, hs]), 0.0)

        for m in GLA_BLOCK_LEVELS:
            for base in range(0, c, 2 * m):
                lower, upper = slice(base, base + m), slice(base + m, base + 2 * m)
                b_ref_row = b[base + m - 1:base + m, :]
                qe = (q[upper, :] * jnp.exp(b[upper, :] - b_ref_row)).astype(BF16)
                ke = (k[lower, :] * jnp.exp(b_ref_row - b[lower, :])).astype(BF16)
                for h in range(C_HEADS):
                    hs = slice(h * C_DK, (h + 1) * C_DK)
                    a_scr[ci, h, upper, lower] = _dot_nt(qe[:, hs], ke[:, hs])

        m = 1
        while m < min(GLA_BLOCK_LEVELS):
            if m in GLA_FINE_LEVELS:
                i = GLA_FINE_LEVELS.index(m)
                d = fine[i * c:(i + 1) * c, :]
            else:
                b3 = b.reshape(c // (2 * m), 2 * m, hk)
                d3 = b3 - b3[:, m - 1:m, :]
                sub = lax.broadcasted_iota(jnp.int32, (1, 2 * m, 1), 1)
                d = jnp.where(sub >= m, d3, -d3).reshape(c, hk)
            e = jnp.exp(d)
            upper = (row & m) != 0
            qe = jnp.where(upper, q * e, 0.0).astype(BF16)
            ke = jnp.where(upper, 0.0, k * e).astype(BF16)
            level = (pair >= m) & (pair < 2 * m)
            for h in range(C_HEADS):
                hs = slice(h * C_DK, (h + 1) * C_DK)
                a_scr[ci, h] = jnp.where(level, _dot_nt(qe[:, hs], ke[:, hs]), a_scr[ci, h])
            m *= 2

        b_last = b[c - 1:c, :]
        qs = (q * jnp.exp(b)).astype(BF16)
        kd = (k * jnp.exp(b_last - b)).astype(BF16)
        return qs, kd, jnp.exp(b_last)

    def carry_state(ci, qs, kd, decay):
        rows = slice(ci * c, (ci + 1) * c)
        for h in range(C_HEADS):
            hs = slice(h * C_DK, (h + 1) * C_DK)
            vs = slice(h * C_DV, (h + 1) * C_DV)
            v = v_ref[rows, vs]
            st = st_ref[h]
            o = _dot(a_scr[ci, h].astype(BF16), v) + _dot_nt(qs[:, hs], st.astype(BF16))
            st_ref[h] = st * decay[:, hs] + _dot_tn(v, kd[:, hs])
            on = _ln(o, lng_ref[...], lnb_ref[...])
            y_ref[rows, vs] = (on * sg_ref[rows, vs]).astype(y_ref.dtype)

    parts = [intra(ci) for ci in range(n_chunks)]
    for ci in range(n_chunks):
        carry_state(ci, *parts[ci])


def _gla(q, k, v, la, sg, lng, lnb, *, batch, seq, layer):
    n = q.shape[0]
    c = GLA_CHUNK
    tm = TILES["gla"]
    nt = seq // tm
    hk, hv = C_HEADS * C_DK, C_HEADS * C_DV
    tri = jnp.tril(jnp.ones((c, c), F32)).astype(BF16)
    lvl = _gla_level_matrix()
    kblk = pl.BlockSpec((tm, hk), lambda b, i: (b * nt + i, 0))
    vblk = pl.BlockSpec((tm, hv), lambda b, i: (b * nt + i, 0))
    full2 = lambda a: pl.BlockSpec(a.shape, lambda b, i: (0, 0))
    return pl.pallas_call(
        _gla_kernel,
        grid=(batch, nt),
        in_specs=[kblk, kblk, vblk, kblk, vblk, _layer_spec(lng, layer), _layer_spec(lnb, layer),
                  full2(tri), full2(lvl)],
        out_specs=vblk,
        out_shape=jax.ShapeDtypeStruct((n, hv), BF16),
        scratch_shapes=[pltpu.VMEM((C_HEADS, C_DV, C_DK), F32), pltpu.VMEM((tm // c, C_HEADS, c, c), F32)],
        compiler_params=pltpu.CompilerParams(dimension_semantics=("parallel", "arbitrary"),
                                             vmem_limit_bytes=VMEM_LIMIT),
        name="gla",
    )(q, k, v, la, sg, lng, lnb, tri, lvl)


def _post_ffn_kernel(x_ref, y1_ref, y2_ref, wo_ref, g1_ref, b1_ref, w1_ref, w2_ref, g2_ref, b2_ref,
                     o_ref, x1_ref, x1b_ref, acc_ref):
    i = pl.program_id(0)
    j = pl.program_id(1)
    nt = pl.num_programs(0) - 1
    half = y1_ref.shape[1]
    tm = x_ref.shape[0]
    blocks = [slice(r, r + FFN_ROW_BLOCK) for r in range(0, tm, FFN_ROW_BLOCK)]

    def mlp(x1b):
        h = _dot(x1b, w1_ref[...])
        return _dot(jnp.square(jnp.maximum(h, 0.0)).astype(BF16), w2_ref[...])

    def close_tile(rows):
        o_ref[rows, :] = _ln(ALPHA * x1_ref[rows, :] + acc_ref[rows, :], g2_ref[...], b2_ref[...])

    def out_proj(rows):
        return _dot(y1_ref[rows, :], wo_ref[0:half, :]) + _dot(y2_ref[rows, :], wo_ref[half:2 * half, :])

    def open_tile(rows, y):
        x1 = _ln(ALPHA * x_ref[rows, :] + y, g1_ref[...], b1_ref[...])
        x1_ref[rows, :] = x1
        x1b = x1.astype(BF16)
        x1b_ref[rows, :] = x1b
        acc_ref[rows, :] = mlp(x1b)

    def open_all(close_first):
        y = out_proj(blocks[0])
        for r, rows in enumerate(blocks):
            y_next = out_proj(blocks[r + 1]) if r + 1 < len(blocks) else None
            if close_first:
                close_tile(rows)
            open_tile(rows, y)
            y = y_next

    @pl.when((j == 0) & (i == 0))
    def _():
        open_all(False)

    @pl.when((j == 0) & (i > 0) & (i < nt))
    def _():
        open_all(True)

    @pl.when((j == 0) & (i == nt))
    def _():
        for rows in blocks:
            close_tile(rows)

    @pl.when((j > 0) & (i < nt))
    def _():
        acc_ref[...] += mlp(x1b_ref[...])


def _post_ffn(x, y1, y2, y2_col, wo, g1, b1, w1, w2, g2, b2, *, layer):
    n = x.shape[0]
    tm, tf = TILES["ffn_rows"], TILES["ffn_cols"]
    nt = n // tm
    half = D_MODEL // 2
    opened = lambda i: jnp.minimum(i, nt - 1)
    vec = pl.BlockSpec((None, 1, D_MODEL), lambda i, j: (layer, 0, 0))
    return pl.pallas_call(
        _post_ffn_kernel,
        grid=(nt + 1, D_FF // tf),
        in_specs=[pl.BlockSpec((tm, D_MODEL), lambda i, j: (opened(i), 0)),
                  pl.BlockSpec((tm, half), lambda i, j: (opened(i), 0)),
                  pl.BlockSpec((tm, half), lambda i, j: (opened(i), y2_col)),
                  pl.BlockSpec((None, D_MODEL, D_MODEL), lambda i, j: (layer, 0, 0)),
                  vec, vec,
                  pl.BlockSpec((None, D_MODEL, tf), lambda i, j: (layer, 0, j)),
                  pl.BlockSpec((None, tf, D_MODEL), lambda i, j: (layer, j, 0)),
                  vec, vec],
        out_specs=pl.BlockSpec((tm, D_MODEL), lambda i, j: (jnp.maximum(i - 1, 0), 0)),
        out_shape=jax.ShapeDtypeStruct((n, D_MODEL), F32),
        scratch_shapes=[pltpu.VMEM((tm, D_MODEL), F32), pltpu.VMEM((tm, D_MODEL), BF16),
                        pltpu.VMEM((tm, D_MODEL), F32)],
        compiler_params=pltpu.CompilerParams(dimension_semantics=("arbitrary", "arbitrary"),
                                             vmem_limit_bytes=VMEM_LIMIT),
        name="post_ffn",
    )(x, y1, y2, wo, g1, b1, w1, w2, g2, b2)


def _rope_inv_lanes():
    half = B_ROPE // 2
    inv = ROPE_THETA ** (-jnp.arange(half, dtype=F32) / half)
    lanes = jnp.zeros((1, HEAD_PAD), F32)
    lanes = lanes.at[0, B_NOPE:B_NOPE + half].set(inv)
    lanes = lanes.at[0, B_NOPE + half:B_NOPE + B_ROPE].set(inv)
    return lanes


def kernel(x, positions, ln1_g, ln1_b, ln2_g, ln2_b, w_in_even, a_w_s, a_b_s, a_ln_g, a_ln_b, b_q_norm, b_kv_norm, b_w_uq, b_w_ukv, w_out_even, w_in_odd, c_w_gate, c_b_gate, c_ln_g, c_ln_b, w_out_odd, w_ff1, w_ff2):
    bn, s, d = x.shape
    n = bn * s
    xf = x.reshape(n, d)
    pos = positions.reshape(n, 1).astype(jnp.int32)
    cos, sin = _rope_tables(pos, _rope_inv_lanes())
    n_even = w_in_even.shape[0]
    vec = lambda a: a.reshape(a.shape[0], 1, -1)

    o_kr = 2 * A_WIDTH + B_Q_RANK + B_KV_RANK
    pad = lambda width: jnp.zeros((n_even, d, width), F32)
    win_even = jnp.concatenate([w_in_even[:, :, :o_kr], pad(B_NOPE), w_in_even[:, :, o_kr:],
                                pad(HEAD_PAD - B_NOPE - B_ROPE)], axis=-1).astype(BF16)
    wuq = b_w_uq.reshape(n_even, B_Q_RANK, B_HEADS, B_NOPE + B_ROPE)
    wuq = jnp.pad(wuq, ((0, 0), (0, 0), (0, 0), (0, HEAD_PAD - B_NOPE - B_ROPE)))
    wuqt = wuq.reshape(n_even, B_Q_RANK, B_HEADS * HEAD_PAD).transpose(0, 2, 1).astype(BF16)
    wukv = b_w_ukv.reshape(n_even, B_KV_RANK, B_HEADS, B_NOPE + B_VDIM)
    wuk = jnp.pad(wukv[..., :B_NOPE], ((0, 0), (0, 0), (0, 0), (0, HEAD_PAD - B_NOPE)))
    wuk = wuk.reshape(n_even, B_KV_RANK, B_HEADS * HEAD_PAD).astype(BF16)
    wuvvt = wukv[..., B_NOPE:].reshape(n_even, B_KV_RANK, B_HEADS * B_VDIM).transpose(0, 2, 1).astype(BF16)
    causal = jnp.tril(jnp.ones((A_CHUNK, A_CHUNK), dtype=bool))
    ws = jnp.where(causal, a_w_s, 0.0).astype(BF16)
    bs = jnp.broadcast_to(a_b_s[..., None], a_b_s.shape + (A_GROUP_DIM,))
    even_params = (win_even, ws, bs, a_ln_g, a_ln_b, vec(b_q_norm), vec(b_kv_norm), wuqt, wuk, wuvvt)

    hk, hv = C_HEADS * C_DK, C_HEADS * C_DV
    win_odd = w_in_odd.astype(BF16)
    wzg = jnp.pad(w_in_odd[:, :, 2 * hk + 2 * hv:], ((0, 0), (0, 0), (0, 128 - C_GATE_RANK))).astype(BF16)
    wgate = jnp.pad(c_w_gate, ((0, 0), (0, 128 - C_GATE_RANK), (0, 0))).astype(BF16)
    odd_params = (win_odd, wzg, wgate, vec(c_b_gate))
    c_lng, c_lnb = vec(c_ln_g), vec(c_ln_b)

    w_out = jnp.stack([(w_out_even if layer % 2 == 0 else w_out_odd)[layer // 2]
                       for layer in range(DEPTH)]).astype(BF16)
    w1, w2 = w_ff1.astype(BF16), w_ff2.astype(BF16)
    g1, b1, g2, b2 = vec(ln1_g), vec(ln1_b), vec(ln2_g), vec(ln2_b)

    for layer in range(DEPTH):
        j = layer // 2
        if layer % 2 == 0:
            ya, qt, k, vt = _even_pre(xf, cos, sin, *even_params, layer=j)
            yb = _attn(qt, k, vt, batch=bn, seq=s)
            y1, y2, y2_col = ya, yb, 0
        else:
            q, k, v, sg, la = _odd_pre(xf, *odd_params, layer=j)
            y = _gla(q, k, v, la, sg, c_lng, c_lnb, batch=bn, seq=s, layer=j)
            y1, y2, y2_col = y, y, 1
        xf = _post_ffn(xf, y1, y2, y2_col, w_out, g1, b1, w1, w2, g2, b2, layer=layer)
    return xf.reshape(bn, s, d)
```

```python
import jax
import jax.numpy as jnp
from jax import lax
from jax.experimental import pallas as pl
from jax.experimental.pallas import tpu as pltpu

F32 = jnp.float32
BF16 = jnp.bfloat16

D_MODEL = 1024
DEPTH = 4

A_CHUNK = 128
A_GROUPS = 4
A_WIDTH = 512
A_GROUP_DIM = 128

B_HEADS = 8
B_NOPE = 64
B_ROPE = 32
B_VDIM = 64
B_Q_RANK = 384
B_KV_RANK = 256
ROPE_THETA = 10000.0
HEAD_PAD = 128
ATTN_SUM_ROWS = 16

C_HEADS = 4
C_DK = 128
C_DV = 256
C_GATE_RANK = 16
C_GATE_TAU = 16.0
GLA_CHUNK = 256
GLA_FINE_LEVELS = (1, 2, 4)
GLA_BLOCK_LEVELS = (128, 64)

D_FF = 4 * D_MODEL
ALPHA = (2.0 * DEPTH) ** 0.25
LN_EPS = 1e-5
LOG2_E = 1.4426950408889634

VMEM_LIMIT = 56 * 1024 * 1024

TILES = {
    "attn": 512,
    "odd_pre": 1024,
    "rope": 2048,
    "gla": 1024,
    "ffn_rows": 1024,
    "ffn_cols": 1024,
}
PRE_ROW_BLOCK = 256
ODD_ROW_BLOCK = 256
FFN_ROW_BLOCK = 256


def _dot(a, b):
    return jnp.dot(a, b, preferred_element_type=F32)


def _dot_nt(a, b):
    return lax.dot_general(a, b, (((1,), (1,)), ((), ())), preferred_element_type=F32)


def _dot_tn(a, b):
    return lax.dot_general(a, b, (((0,), (0,)), ((), ())), preferred_element_type=F32)


def _ln(x, g, b):
    mu = jnp.mean(x, axis=-1, keepdims=True)
    xc = x - mu
    var = jnp.mean(xc * xc, axis=-1, keepdims=True)
    return xc * lax.rsqrt(var + LN_EPS) * g + b


def _rms(x, g):
    ms = jnp.mean(x * x, axis=-1, keepdims=True)
    return x * lax.rsqrt(ms + LN_EPS) * g


def _gelu(x):
    return 0.5 * x * (1.0 + jnp.tanh(0.7978845608028654 * (x + 0.044715 * (x * x * x))))


def _layer_spec(a, layer, cols=None):
    zeros = (0,) * (a.ndim - 1)
    shape = a.shape[1:] if cols is None else a.shape[1:-1] + (cols,)
    return pl.BlockSpec((None,) + shape, lambda *_: (layer,) + zeros)


def _rope_table_kernel(pos_ref, inv_ref, cos_ref, sin_ref):
    ang = inv_ref[...] * pos_ref[...].astype(F32)
    cos_ref[...] = jnp.cos(ang)
    sin_ref[...] = jnp.sin(ang)


def _rope_tables(pos_row, inv_col):
    n = pos_row.shape[1]
    tm = TILES["rope"]
    half = inv_col.shape[0]
    table = jax.ShapeDtypeStruct((half, n), F32)
    return pl.pallas_call(
        _rope_table_kernel,
        grid=(n // tm,),
        in_specs=[pl.BlockSpec((1, tm), lambda i: (0, i)), pl.BlockSpec((half, 1), lambda i: (0, 0))],
        out_specs=[pl.BlockSpec((half, tm), lambda i: (0, i))] * 2,
        out_shape=[table, table],
        compiler_params=pltpu.CompilerParams(dimension_semantics=("parallel",)),
        name="rope_tables",
    )(pos_row, inv_col)


def _even_pre_kernel(x_ref, cos_ref, sin_ref, win_ref,
                     ws_ref, bs_ref, alng_ref, alnb_ref, gq_ref, gkv_ref,
                     wuqt_ref, wuk_ref, wuvvt_ref,
                     ya_ref, qt_ref, k_ref, vt_ref):
    tm = x_ref.shape[0]
    o_cq = 2 * A_WIDTH
    o_ckv = o_cq + B_Q_RANK
    o_kr = o_ckv + B_KV_RANK
    half = B_ROPE // 2
    scale = (B_NOPE + B_ROPE) ** -0.5 * LOG2_E

    def in_proj(rows):
        return (_dot(x_ref[rows, :].astype(BF16), win_ref[...]),)

    def finish(rows, z):
        for g in range(A_GROUPS):
            lo = g * A_GROUP_DIM
            u = _gelu(z[:, lo:lo + A_GROUP_DIM])
            vv = _gelu(z[:, A_WIDTH + lo:A_WIDTH + lo + A_GROUP_DIM])
            vn = _ln(vv, alng_ref[g:g + 1, :], alnb_ref[g:g + 1, :]).astype(BF16)
            for r in range(0, rows.stop - rows.start, A_CHUNK):
                mixed = _dot(ws_ref[g], vn[r:r + A_CHUNK, :]) + bs_ref[g]
                ya_ref[rows.start + r:rows.start + r + A_CHUNK, lo:lo + A_GROUP_DIM] = (
                    u[r:r + A_CHUNK, :] * mixed).astype(ya_ref.dtype)

        cos_t = cos_ref[:, rows]
        sin_t = sin_ref[:, rows]

        def rope_t(blk):
            x1 = blk[B_NOPE:B_NOPE + half, :]
            x2 = blk[B_NOPE + half:B_NOPE + B_ROPE, :]
            return jnp.concatenate([blk[:B_NOPE, :], x1 * cos_t - x2 * sin_t, x2 * cos_t + x1 * sin_t,
                                    blk[B_NOPE + B_ROPE:, :]], axis=0)

        ckvn = _rms(z[:, o_ckv:o_kr], gkv_ref[...]).astype(BF16)
        kn = _dot(ckvn, wuk_ref[...])
        kr_t = z[:, o_kr:o_kr + HEAD_PAD].T
        kr_t = jnp.concatenate([jnp.zeros((B_NOPE, kr_t.shape[1]), F32), kr_t[:B_ROPE, :],
                                jnp.zeros((HEAD_PAD - B_NOPE - B_ROPE, kr_t.shape[1]), F32)], axis=0)
        kr = rope_t(kr_t).T
        for h in range(B_HEADS):
            lo = h * HEAD_PAD
            k_ref[rows, lo:lo + HEAD_PAD] = (kn[:, lo:lo + HEAD_PAD] + kr).astype(k_ref.dtype)

        cqn = _rms(z[:, o_cq:o_ckv], gq_ref[...]).astype(BF16)
        q_t = _dot_nt(wuqt_ref[...], cqn)
        for h in range(B_HEADS):
            lo = h * HEAD_PAD
            qt_ref[0, lo:lo + HEAD_PAD, rows] = (rope_t(q_t[lo:lo + HEAD_PAD, :]) * scale).astype(qt_ref.dtype)
        vt_ref[0, :, rows] = _dot_nt(wuvvt_ref[...], ckvn).astype(vt_ref.dtype)

    blocks = [slice(r, r + PRE_ROW_BLOCK) for r in range(0, tm, PRE_ROW_BLOCK)]
    cur = in_proj(blocks[0])
    for r, rows in enumerate(blocks):
        nxt = in_proj(blocks[r + 1]) if r + 1 < len(blocks) else None
        finish(rows, *cur)
        cur = nxt


def _even_pre(x, cos, sin, *params, layer):
    n = x.shape[0]
    tm = TILES["attn"]
    row = lambda w: pl.BlockSpec((tm, w), lambda i: (i, 0))
    table = pl.BlockSpec((B_ROPE // 2, tm), lambda i: (0, i))
    hp = B_HEADS * HEAD_PAD
    return pl.pallas_call(
        _even_pre_kernel,
        grid=(n // tm,),
        in_specs=[row(D_MODEL), table, table] + [_layer_spec(a, layer) for a in params],
        out_specs=[row(A_WIDTH),
                   pl.BlockSpec((1, hp, tm), lambda i: (i, 0, 0)),
                   row(hp),
                   pl.BlockSpec((1, B_HEADS * B_VDIM, tm), lambda i: (i, 0, 0))],
        out_shape=[jax.ShapeDtypeStruct((n, A_WIDTH), BF16),
                   jax.ShapeDtypeStruct((n // tm, hp, tm), BF16),
                   jax.ShapeDtypeStruct((n, hp), BF16),
                   jax.ShapeDtypeStruct((n // tm, B_HEADS * B_VDIM, tm), BF16)],
        compiler_params=pltpu.CompilerParams(dimension_semantics=("parallel",),
                                             vmem_limit_bytes=VMEM_LIMIT),
        name="even_pre",
    )(x, cos, sin, *params)


def _attn_kernel(qt_ref, k_ref, vt_ref, o_ref, sa_ref, sb_ref):
    t = qt_ref.shape[2]
    ones = jnp.ones((ATTN_SUM_ROWS, t), BF16)

    def one_tile(qi, _):

        def scores(j, dst):
            start = pl.multiple_of(j * t, t)
            for hh in range(2):
                lo = hh * HEAD_PAD
                dst[hh] = _dot(k_ref[pl.ds(start, t), lo:lo + HEAD_PAD], qt_ref[qi, lo:lo + HEAD_PAD, :])

        def update(j, src, stats, masked):
            vt = vt_ref[j]
            new = []
            for hh in range(2):
                m, acc = stats[hh]
                s = src[hh]
                if masked:
                    key_ids = lax.broadcasted_iota(jnp.int32, (t, t), 0)
                    qry_ids = lax.broadcasted_iota(jnp.int32, (t, t), 1)
                    s = jnp.where(key_ids <= qry_ids, s, -jnp.inf)
                m_new = jnp.maximum(m, jnp.max(s, axis=0, keepdims=True))
                p = jnp.exp2(s - m_new).astype(BF16)
                a = jnp.exp2(m - m_new)
                v_ext = jnp.concatenate([vt[hh * B_VDIM:(hh + 1) * B_VDIM, :], ones], axis=0)
                acc = a * acc + _dot(v_ext, p)
                new.append((m_new, acc))
            return tuple(new)

        def two_blocks(i, stats):
            j = 2 * i
            scores(j + 1, sb_ref)
            stats = update(j, sa_ref, stats, False)
            scores(j + 2, sa_ref)
            return update(j + 1, sb_ref, stats, False)

        def four_blocks(i, stats):
            return two_blocks(2 * i + 1, two_blocks(2 * i, stats))

        def tail_odd(stats):
            scores(qi, sb_ref)
            stats = update(qi - 1, sa_ref, stats, False)
            return update(qi, sb_ref, stats, True)

        def tail_even(stats):
            return update(qi, sa_ref, stats, True)

        scores(0, sa_ref)
        init1 = (jnp.full((1, t), -jnp.inf, F32), jnp.zeros((B_VDIM + ATTN_SUM_ROWS, t), F32))
        def eight_blocks(i, stats):
            return four_blocks(2 * i + 1, four_blocks(2 * i, stats))

        stats = lax.fori_loop(0, qi // 8, eight_blocks, (init1, init1))
        stats = lax.fori_loop(2 * (qi // 8), qi // 4, four_blocks, stats)
        stats = lax.fori_loop(2 * (qi // 4), qi // 2, two_blocks, stats)
        stats = lax.cond(qi % 2 == 1, tail_odd, tail_even, stats)
        out_t = jnp.concatenate([acc[:B_VDIM] / acc[B_VDIM:B_VDIM + 1] for (_, acc) in stats], axis=0)
        o_ref[pl.ds(pl.multiple_of(qi * t, t), t), :] = out_t.T.astype(o_ref.dtype)
        return 0

    lax.fori_loop(0, qt_ref.shape[0], one_tile, 0)


def _attn(qt, k, vt, *, batch, seq):
    t = qt.shape[2]
    n = k.shape[0]
    nq = seq // t
    return pl.pallas_call(
        _attn_kernel,
        grid=(batch, B_HEADS // 2),
        in_specs=[pl.BlockSpec((nq, 2 * HEAD_PAD, t), lambda b, h: (b, h, 0)),
                  pl.BlockSpec((seq, 2 * HEAD_PAD), lambda b, h: (b, h)),
                  pl.BlockSpec((nq, 2 * B_VDIM, t), lambda b, h: (b, h, 0))],
        out_specs=pl.BlockSpec((seq, 2 * B_VDIM), lambda b, h: (b, h)),
        out_shape=jax.ShapeDtypeStruct((n, B_HEADS * B_VDIM), BF16),
        scratch_shapes=[pltpu.VMEM((2, t, t), F32), pltpu.VMEM((2, t, t), F32)],
        compiler_params=pltpu.CompilerParams(dimension_semantics=("parallel", "parallel"),
                                             vmem_limit_bytes=VMEM_LIMIT),
        name="attn",
    )(qt, k, vt)


def _odd_pre_kernel(x_ref, win_ref, wzg_ref, wgate_ref, bgate_ref,
                    q_ref, k_ref, v_ref, sg_ref, la_ref):
    hk, hv = C_HEADS * C_DK, C_HEADS * C_DV
    tm = x_ref.shape[0]

    def in_proj(rows):
        xb = x_ref[rows, :].astype(BF16)
        return _dot(xb, win_ref[...]), _dot(xb, wzg_ref[...])

    def finish(rows, z, zg):
        q_ref[rows, :] = z[:, :hk] * (C_DK ** -0.5)
        k_ref[rows, :] = z[:, hk:2 * hk]
        v_ref[rows, :] = z[:, 2 * hk:2 * hk + hv].astype(v_ref.dtype)
        g = z[:, 2 * hk + hv:]
        sg_ref[rows, :] = g * jax.nn.sigmoid(g)
        logits = _dot(zg.astype(BF16), wgate_ref[...]) + bgate_ref[...]
        la_ref[rows, :] = -(jnp.maximum(-logits, 0.0) + jnp.log1p(jnp.exp(-jnp.abs(logits)))) * (1.0 / C_GATE_TAU)

    blocks = [slice(r, r + ODD_ROW_BLOCK) for r in range(0, tm, ODD_ROW_BLOCK)]
    cur = in_proj(blocks[0])
    for r, rows in enumerate(blocks):
        nxt = in_proj(blocks[r + 1]) if r + 1 < len(blocks) else None
        finish(rows, *cur)
        cur = nxt


def _odd_pre(x, *params, layer):
    n = x.shape[0]
    tm = TILES["odd_pre"]
    row = lambda w: pl.BlockSpec((tm, w), lambda i: (i, 0))
    hk, hv = C_HEADS * C_DK, C_HEADS * C_DV
    return pl.pallas_call(
        _odd_pre_kernel,
        grid=(n // tm,),
        in_specs=[row(D_MODEL), _layer_spec(params[0], layer, cols=2 * hk + 2 * hv)]
                 + [_layer_spec(a, layer) for a in params[1:]],
        out_specs=[row(hk), row(hk), row(hv), row(hv), row(hk)],
        out_shape=[jax.ShapeDtypeStruct((n, hk), F32),
                   jax.ShapeDtypeStruct((n, hk), F32),
                   jax.ShapeDtypeStruct((n, hv), BF16),
                   jax.ShapeDtypeStruct((n, hv), F32),
                   jax.ShapeDtypeStruct((n, hk), F32)],
        compiler_params=pltpu.CompilerParams(dimension_semantics=("parallel",),
                                             vmem_limit_bytes=VMEM_LIMIT),
        name="odd_pre",
    )(x, *params)


def _gla_level_matrix():
    c = GLA_CHUNK
    t = jnp.arange(c)[:, None]
    u = jnp.arange(c)[None, :]
    mats = []
    for m in GLA_FINE_LEVELS:
        r = t % (2 * m)
        p = t - r + m - 1
        upper = (r >= m) & (u > p) & (u <= t)
        lower = (r < m) & (u > t) & (u <= p)
        mats.append(upper | lower)
    return jnp.concatenate(mats, axis=0).astype(BF16)


def _split_bf16(x, pieces):
    out = []
    for _ in range(pieces - 1):
        hi = x.astype(BF16)
        out.append(hi)
        x = x - hi.astype(F32)
    out.append(x.astype(BF16))
    return out


def _gla_kernel(q_ref, k_ref, v_ref, la_ref, sg_ref, lng_ref, lnb_ref, tri_ref, lvl_ref,
                y_ref, st_ref, a_scr):
    c = GLA_CHUNK
    hk = C_HEADS * C_DK
    n_chunks = q_ref.shape[0] // c

    @pl.when(pl.program_id(1) == 0)
    def _():
        st_ref[...] = jnp.zeros_like(st_ref)

    row = lax.broadcasted_iota(jnp.int32, (c, 1), 0)
    pair = lax.broadcasted_iota(jnp.int32, (c, c), 0) ^ lax.broadcasted_iota(jnp.int32, (c, c), 1)

    def intra(ci):
        rows = slice(ci * c, (ci + 1) * c)
        q = q_ref[rows, :]
        k = k_ref[rows, :]
        g = _split_bf16(la_ref[rows, :], 3)
        tri = tri_ref[...]
        b = _dot(tri, g[0]) + _dot(tri, g[1]) + _dot(tri, g[2])
        fine = _dot(lvl_ref[...], g[0]) + _dot(lvl_ref[...], g[1])

        qb = q.astype(BF16)
        kb = k.astype(BF16)
        for h in range(C_HEADS):
            hs = slice(h * C_DK, (h + 1) * C_DK)
            a_scr[ci, h] = jnp.where(pair == 0, _dot_nt(qb[:, hs], kb[:, hs]), 0.0)

        for m in GLA_BLOCK_LEVELS:
            for base in range(0, c, 2 * m):
                lower, upper = slice(base, base + m), slice(base + m, base + 2 * m)
                b_ref_row = b[base + m - 1:base + m, :]
                qe = (q[upper, :] * jnp.exp(b[upper, :] - b_ref_row)).astype(BF16)
                ke = (k[lower, :] * jnp.exp(b_ref_row - b[lower, :])).astype(BF16)
                for h in range(C_HEADS):
                    hs = slice(h * C_DK, (h + 1) * C_DK)
                    a_scr[ci, h, upper, lower] = _dot_nt(qe[:, hs], ke[:, hs])

        m = 1
        while m < min(GLA_BLOCK_LEVELS):
            if m in GLA_FINE_LEVELS:
                i = GLA_FINE_LEVELS.index(m)
                d = fine[i * c:(i + 1) * c, :]
            else:
                b3 = b.reshape(c // (2 * m), 2 * m, hk)
                d3 = b3 - b3[:, m - 1:m, :]
                sub = lax.broadcasted_iota(jnp.int32, (1, 2 * m, 1), 1)
                d = jnp.where(sub >= m, d3, -d3).reshape(c, hk)
            e = jnp.exp(d)
            upper = (row & m) != 0
            qe = jnp.where(upper, q * e, 0.0).astype(BF16)
            ke = jnp.where(upper, 0.0, k * e).astype(BF16)
            level = (pair >= m) & (pair < 2 * m)
            for h in range(C_HEADS):
                hs = slice(h * C_DK, (h + 1) * C_DK)
                a_scr[ci, h] = jnp.where(level, _dot_nt(qe[:, hs], ke[:, hs]), a_scr[ci, h])
            m *= 2

        b_last = b[c - 1:c, :]
        qs = (q * jnp.exp(b)).astype(BF16)
        kd = (k * jnp.exp(b_last - b)).astype(BF16)
        return qs, kd, jnp.exp(b_last)

    def carry_state(ci, qs, kd, decay):
        rows = slice(ci * c, (ci + 1) * c)
        for h in range(C_HEADS):
            hs = slice(h * C_DK, (h + 1) * C_DK)
            vs = slice(h * C_DV, (h + 1) * C_DV)
            v = v_ref[rows, vs]
            st = st_ref[h]
            o = _dot(a_scr[ci, h].astype(BF16), v) + _dot_nt(qs[:, hs], st.astype(BF16))
            st_ref[h] = st * decay[:, hs] + _dot_tn(v, kd[:, hs])
            on = _ln(o, lng_ref[...], lnb_ref[...])
            y_ref[rows, vs] = (on * sg_ref[rows, vs]).astype(y_ref.dtype)

    parts = [intra(ci) for ci in range(n_chunks)]
    for ci in range(n_chunks):
        carry_state(ci, *parts[ci])


def _gla(q, k, v, la, sg, lng, lnb, *, batch, seq, layer):
    n = q.shape[0]
    c = GLA_CHUNK
    tm = TILES["gla"]
    nt = seq // tm
    hk, hv = C_HEADS * C_DK, C_HEADS * C_DV
    tri = jnp.tril(jnp.ones((c, c), F32)).astype(BF16)
    lvl = _gla_level_matrix()
    kblk = pl.BlockSpec((tm, hk), lambda b, i: (b * nt + i, 0))
    vblk = pl.BlockSpec((tm, hv), lambda b, i: (b * nt + i, 0))
    full2 = lambda a: pl.BlockSpec(a.shape, lambda b, i: (0, 0))
    return pl.pallas_call(
        _gla_kernel,
        grid=(batch, nt),
        in_specs=[kblk, kblk, vblk, kblk, vblk, _layer_spec(lng, layer), _layer_spec(lnb, layer),
                  full2(tri), full2(lvl)],
        out_specs=vblk,
        out_shape=jax.ShapeDtypeStruct((n, hv), BF16),
        scratch_shapes=[pltpu.VMEM((C_HEADS, C_DV, C_DK), F32), pltpu.VMEM((tm // c, C_HEADS, c, c), F32)],
        compiler_params=pltpu.CompilerParams(dimension_semantics=("parallel", "arbitrary"),
                                             vmem_limit_bytes=VMEM_LIMIT),
        name="gla",
    )(q, k, v, la, sg, lng, lnb, tri, lvl)


def _post_ffn_kernel(x_ref, y1_ref, y2_ref, wo_ref, g1_ref, b1_ref, w1_ref, w2_ref, g2_ref, b2_ref,
                     o_ref, x1_ref, x1b_ref, acc_ref):
    i = pl.program_id(0)
    j = pl.program_id(1)
    nt = pl.num_programs(0) - 1
    half = y1_ref.shape[1]
    tm = x_ref.shape[0]
    blocks = [slice(r, r + FFN_ROW_BLOCK) for r in range(0, tm, FFN_ROW_BLOCK)]

    def mlp(x1b):
        h = _dot(x1b, w1_ref[...])
        return _dot(jnp.square(jnp.maximum(h, 0.0)).astype(BF16), w2_ref[...])

    def close_tile(rows):
        o_ref[rows, :] = _ln(ALPHA * x1_ref[rows, :] + acc_ref[rows, :], g2_ref[...], b2_ref[...])

    def out_proj(rows):
        return _dot(y1_ref[rows, :], wo_ref[0:half, :]) + _dot(y2_ref[rows, :], wo_ref[half:2 * half, :])

    def open_tile(rows, y):
        x1 = _ln(ALPHA * x_ref[rows, :] + y, g1_ref[...], b1_ref[...])
        x1_ref[rows, :] = x1
        x1b = x1.astype(BF16)
        x1b_ref[rows, :] = x1b
        acc_ref[rows, :] = mlp(x1b)

    def open_all(close_first):
        y = out_proj(blocks[0])
        for r, rows in enumerate(blocks):
            y_next = out_proj(blocks[r + 1]) if r + 1 < len(blocks) else None
            if close_first:
                close_tile(rows)
            open_tile(rows, y)
            y = y_next

    @pl.when((j == 0) & (i == 0))
    def _():
        open_all(False)

    @pl.when((j == 0) & (i > 0) & (i < nt))
    def _():
        open_all(True)

    @pl.when((j == 0) & (i == nt))
    def _():
        for rows in blocks:
            close_tile(rows)

    @pl.when((j > 0) & (i < nt))
    def _():
        acc_ref[...] += mlp(x1b_ref[...])


def _post_ffn(x, y1, y2, y2_col, wo, g1, b1, w1, w2, g2, b2, *, layer):
    n = x.shape[0]
    tm, tf = TILES["ffn_rows"], TILES["ffn_cols"]
    nt = n // tm
    half = D_MODEL // 2
    opened = lambda i: jnp.minimum(i, nt - 1)
    vec = pl.BlockSpec((None, 1, D_MODEL), lambda i, j: (layer, 0, 0))
    return pl.pallas_call(
        _post_ffn_kernel,
        grid=(nt + 1, D_FF // tf),
        in_specs=[pl.BlockSpec((tm, D_MODEL), lambda i, j: (opened(i), 0)),
                  pl.BlockSpec((tm, half), lambda i, j: (opened(i), 0)),
                  pl.BlockSpec((tm, half), lambda i, j: (opened(i), y2_col)),
                  pl.BlockSpec((None, D_MODEL, D_MODEL), lambda i, j: (layer, 0, 0)),
                  vec, vec,
                  pl.BlockSpec((None, D_MODEL, tf), lambda i, j: (layer, 0, j)),
                  pl.BlockSpec((None, tf, D_MODEL), lambda i, j: (layer, j, 0)),
                  vec, vec],
        out_specs=pl.BlockSpec((tm, D_MODEL), lambda i, j: (jnp.maximum(i - 1, 0), 0)),
        out_shape=jax.ShapeDtypeStruct((n, D_MODEL), F32),
        scratch_shapes=[pltpu.VMEM((tm, D_MODEL), F32), pltpu.VMEM((tm, D_MODEL), BF16),
                        pltpu.VMEM((tm, D_MODEL), F32)],
        compiler_params=pltpu.CompilerParams(dimension_semantics=("arbitrary", "arbitrary"),
                                             vmem_limit_bytes=VMEM_LIMIT),
        name="post_ffn",
    )(x, y1, y2, wo, g1, b1, w1, w2, g2, b2)


def kernel(x, positions, ln1_g, ln1_b, ln2_g, ln2_b, w_in_even, a_w_s, a_b_s, a_ln_g, a_ln_b, b_q_norm, b_kv_norm, b_w_uq, b_w_ukv, w_out_even, w_in_odd, c_w_gate, c_b_gate, c_ln_g, c_ln_b, w_out_odd, w_ff1, w_ff2):
    bn, s, d = x.shape
    n = bn * s
    xf = x.reshape(n, d)
    half = B_ROPE // 2
    inv = ROPE_THETA ** (-jnp.arange(half, dtype=F32) / half)
    cos, sin = _rope_tables(positions.reshape(1, n).astype(jnp.int32), inv.reshape(half, 1))
    n_even = w_in_even.shape[0]
    vec = lambda a: a.reshape(a.shape[0], 1, -1)

    win_even = jnp.pad(w_in_even, ((0, 0), (0, 0), (0, HEAD_PAD - B_ROPE))).astype(BF16)
    wuq = b_w_uq.reshape(n_even, B_Q_RANK, B_HEADS, B_NOPE + B_ROPE)
    wuq = jnp.pad(wuq, ((0, 0), (0, 0), (0, 0), (0, HEAD_PAD - B_NOPE - B_ROPE)))
    wuqt = wuq.reshape(n_even, B_Q_RANK, B_HEADS * HEAD_PAD).transpose(0, 2, 1).astype(BF16)
    wukv = b_w_ukv.reshape(n_even, B_KV_RANK, B_HEADS, B_NOPE + B_VDIM)
    wuk = jnp.pad(wukv[..., :B_NOPE], ((0, 0), (0, 0), (0, 0), (0, HEAD_PAD - B_NOPE)))
    wuk = wuk.reshape(n_even, B_KV_RANK, B_HEADS * HEAD_PAD).astype(BF16)
    wuvvt = wukv[..., B_NOPE:].reshape(n_even, B_KV_RANK, B_HEADS * B_VDIM).transpose(0, 2, 1).astype(BF16)
    causal = jnp.tril(jnp.ones((A_CHUNK, A_CHUNK), dtype=bool))
    ws = jnp.where(causal, a_w_s, 0.0).astype(BF16)
    bs = jnp.broadcast_to(a_b_s[..., None], a_b_s.shape + (A_GROUP_DIM,))
    even_params = (win_even, ws, bs, a_ln_g, a_ln_b, vec(b_q_norm), vec(b_kv_norm), wuqt, wuk, wuvvt)

    hk, hv = C_HEADS * C_DK, C_HEADS * C_DV
    win_odd = w_in_odd.astype(BF16)
    wzg = jnp.pad(w_in_odd[:, :, 2 * hk + 2 * hv:], ((0, 0), (0, 0), (0, 128 - C_GATE_RANK))).astype(BF16)
    wgate = jnp.pad(c_w_gate, ((0, 0), (0, 128 - C_GATE_RANK), (0, 0))).astype(BF16)
    odd_params = (win_odd, wzg, wgate, vec(c_b_gate))
    c_lng, c_lnb = vec(c_ln_g), vec(c_ln_b)

    w_out = jnp.stack([(w_out_even if layer % 2 == 0 else w_out_odd)[layer // 2]
                       for layer in range(DEPTH)]).astype(BF16)
    w1, w2 = w_ff1.astype(BF16), w_ff2.astype(BF16)
    g1, b1, g2, b2 = vec(ln1_g), vec(ln1_b), vec(ln2_g), vec(ln2_b)

    for layer in range(DEPTH):
        j = layer // 2
        if layer % 2 == 0:
            ya, qt, k, vt = _even_pre(xf, cos, sin, *even_params, layer=j)
            yb = _attn(qt, k, vt, batch=bn, seq=s)
            y1, y2, y2_col = ya, yb, 0
        else:
            q, k, v, sg, la = _odd_pre(xf, *odd_params, layer=j)
            y = _gla(q, k, v, la, sg, c_lng, c_lnb, batch=bn, seq=s, layer=j)
            y1, y2, y2_col = y, y, 1
        xf = _post_ffn(xf, y1, y2, y2_col, w_out, g1, b1, w1, w2, g2, b2, layer=layer)
    return xf.reshape(bn, s, d)
```

```python
import jax
import jax.numpy as jnp
from jax import lax
from jax.experimental import pallas as pl
from jax.experimental.pallas import tpu as pltpu

F32 = jnp.float32
BF16 = jnp.bfloat16

D_MODEL = 1024
DEPTH = 4

A_CHUNK = 128
A_GROUPS = 4
A_WIDTH = 512
A_GROUP_DIM = 128

B_HEADS = 8
B_NOPE = 64
B_ROPE = 32
B_VDIM = 64
B_Q_RANK = 384
B_KV_RANK = 256
ROPE_THETA = 10000.0
HEAD_PAD = 128
ATTN_SUM_ROWS = 16

C_HEADS = 4
C_DK = 128
C_DV = 256
C_GATE_RANK = 16
C_GATE_TAU = 16.0
GLA_CHUNK = 256
GLA_FINE_LEVELS = (1, 2, 4)
GLA_BLOCK_LEVELS = (128, 64)

D_FF = 4 * D_MODEL
ALPHA = (2.0 * DEPTH) ** 0.25
LN_EPS = 1e-5
LOG2_E = 1.4426950408889634

VMEM_LIMIT = 56 * 1024 * 1024

TILES = {
    "attn": 512,
    "odd_pre": 1024,
    "rope": 2048,
    "gla": 1024,
    "ffn_rows": 1024,
    "ffn_cols": 1024,
}
PRE_ROW_BLOCK = 256
ODD_ROW_BLOCK = 256
FFN_ROW_BLOCK = 256


def _dot(a, b):
    return jnp.dot(a, b, preferred_element_type=F32)


def _dot_nt(a, b):
    return lax.dot_general(a, b, (((1,), (1,)), ((), ())), preferred_element_type=F32)


def _dot_tn(a, b):
    return lax.dot_general(a, b, (((0,), (0,)), ((), ())), preferred_element_type=F32)


def _ln(x, g, b):
    mu = jnp.mean(x, axis=-1, keepdims=True)
    xc = x - mu
    var = jnp.mean(xc * xc, axis=-1, keepdims=True)
    return xc * lax.rsqrt(var + LN_EPS) * g + b


def _rms(x, g):
    ms = jnp.mean(x * x, axis=-1, keepdims=True)
    return x * lax.rsqrt(ms + LN_EPS) * g


def _gelu(x):
    return 0.5 * x * (1.0 + jnp.tanh(0.7978845608028654 * (x + 0.044715 * (x * x * x))))


def _layer_spec(a, layer, cols=None):
    zeros = (0,) * (a.ndim - 1)
    shape = a.shape[1:] if cols is None else a.shape[1:-1] + (cols,)
    return pl.BlockSpec((None,) + shape, lambda *_: (layer,) + zeros)


def _rope_table_kernel(pos_ref, inv_ref, cos_ref, sin_ref):
    ang = inv_ref[...] * pos_ref[...].astype(F32)
    cos_ref[...] = jnp.cos(ang)
    sin_ref[...] = jnp.sin(ang)


def _rope_tables(pos_row, inv_col):
    n = pos_row.shape[1]
    tm = TILES["rope"]
    half = inv_col.shape[0]
    table = jax.ShapeDtypeStruct((half, n), F32)
    return pl.pallas_call(
        _rope_table_kernel,
        grid=(n // tm,),
        in_specs=[pl.BlockSpec((1, tm), lambda i: (0, i)), pl.BlockSpec((half, 1), lambda i: (0, 0))],
        out_specs=[pl.BlockSpec((half, tm), lambda i: (0, i))] * 2,
        out_shape=[table, table],
        compiler_params=pltpu.CompilerParams(dimension_semantics=("parallel",)),
        name="rope_tables",
    )(pos_row, inv_col)


def _even_pre_kernel(x_ref, cos_ref, sin_ref, win_ref,
                     ws_ref, bs_ref, alng_ref, alnb_ref, gq_ref, gkv_ref,
                     wuqt_ref, wuk_ref, wuvvt_ref,
                     ya_ref, qt_ref, k_ref, vt_ref):
    tm = x_ref.shape[0]
    o_cq = 2 * A_WIDTH
    o_ckv = o_cq + B_Q_RANK
    o_kr = o_ckv + B_KV_RANK
    half = B_ROPE // 2
    scale = (B_NOPE + B_ROPE) ** -0.5 * LOG2_E

    def in_proj(rows):
        return (_dot(x_ref[rows, :].astype(BF16), win_ref[...]),)

    def finish(rows, z):
        for g in range(A_GROUPS):
            lo = g * A_GROUP_DIM
            u = _gelu(z[:, lo:lo + A_GROUP_DIM])
            vv = _gelu(z[:, A_WIDTH + lo:A_WIDTH + lo + A_GROUP_DIM])
            vn = _ln(vv, alng_ref[g:g + 1, :], alnb_ref[g:g + 1, :]).astype(BF16)
            for r in range(0, rows.stop - rows.start, A_CHUNK):
                mixed = _dot(ws_ref[g], vn[r:r + A_CHUNK, :]) + bs_ref[g]
                ya_ref[rows.start + r:rows.start + r + A_CHUNK, lo:lo + A_GROUP_DIM] = (
                    u[r:r + A_CHUNK, :] * mixed).astype(ya_ref.dtype)

        cos_t = cos_ref[:, rows]
        sin_t = sin_ref[:, rows]

        def rope_t(blk):
            x1 = blk[B_NOPE:B_NOPE + half, :]
            x2 = blk[B_NOPE + half:B_NOPE + B_ROPE, :]
            return jnp.concatenate([blk[:B_NOPE, :], x1 * cos_t - x2 * sin_t, x2 * cos_t + x1 * sin_t,
                                    blk[B_NOPE + B_ROPE:, :]], axis=0)

        ckvn = _rms(z[:, o_ckv:o_kr], gkv_ref[...]).astype(BF16)
        kn = _dot(ckvn, wuk_ref[...])
        kr_t = z[:, o_kr:o_kr + HEAD_PAD].T
        kr_t = jnp.concatenate([jnp.zeros((B_NOPE, kr_t.shape[1]), F32), kr_t[:B_ROPE, :],
                                jnp.zeros((HEAD_PAD - B_NOPE - B_ROPE, kr_t.shape[1]), F32)], axis=0)
        kr = rope_t(kr_t).T
        for h in range(B_HEADS):
            lo = h * HEAD_PAD
            k_ref[rows, lo:lo + HEAD_PAD] = (kn[:, lo:lo + HEAD_PAD] + kr).astype(k_ref.dtype)

        cqn = _rms(z[:, o_cq:o_ckv], gq_ref[...]).astype(BF16)
        q_t = _dot_nt(wuqt_ref[...], cqn)
        for h in range(B_HEADS):
            lo = h * HEAD_PAD
            qt_ref[0, lo:lo + HEAD_PAD, rows] = (rope_t(q_t[lo:lo + HEAD_PAD, :]) * scale).astype(qt_ref.dtype)
        vt_ref[0, :, rows] = _dot_nt(wuvvt_ref[...], ckvn).astype(vt_ref.dtype)

    blocks = [slice(r, r + PRE_ROW_BLOCK) for r in range(0, tm, PRE_ROW_BLOCK)]
    cur = in_proj(blocks[0])
    for r, rows in enumerate(blocks):
        nxt = in_proj(blocks[r + 1]) if r + 1 < len(blocks) else None
        finish(rows, *cur)
        cur = nxt


def _even_pre(x, cos, sin, *params, layer):
    n = x.shape[0]
    tm = TILES["attn"]
    row = lambda w: pl.BlockSpec((tm, w), lambda i: (i, 0))
    table = pl.BlockSpec((B_ROPE // 2, tm), lambda i: (0, i))
    hp = B_HEADS * HEAD_PAD
    return pl.pallas_call(
        _even_pre_kernel,
        grid=(n // tm,),
        in_specs=[row(D_MODEL), table, table] + [_layer_spec(a, layer) for a in params],
        out_specs=[row(A_WIDTH),
                   pl.BlockSpec((1, hp, tm), lambda i: (i, 0, 0)),
                   row(hp),
                   pl.BlockSpec((1, B_HEADS * B_VDIM, tm), lambda i: (i, 0, 0))],
        out_shape=[jax.ShapeDtypeStruct((n, A_WIDTH), BF16),
                   jax.ShapeDtypeStruct((n // tm, hp, tm), BF16),
                   jax.ShapeDtypeStruct((n, hp), BF16),
                   jax.ShapeDtypeStruct((n // tm, B_HEADS * B_VDIM, tm), BF16)],
        compiler_params=pltpu.CompilerParams(dimension_semantics=("parallel",),
                                             vmem_limit_bytes=VMEM_LIMIT),
        name="even_pre",
    )(x, cos, sin, *params)


def _attn_kernel(qt_ref, k_ref, vt_ref, o_ref, sa_ref, sb_ref):
    t = qt_ref.shape[2]
    ones = jnp.ones((ATTN_SUM_ROWS, t), BF16)

    def one_tile(qi, _):
        outs = [head_pass(qi, hh) for hh in range(2)]
        out_t = jnp.concatenate(outs, axis=0)
        o_ref[pl.ds(pl.multiple_of(qi * t, t), t), :] = out_t.T.astype(o_ref.dtype)
        return 0

    def head_pass(qi, hh):
        lo = hh * HEAD_PAD

        def scores(j, dst):
            start = pl.multiple_of(j * t, t)
            dst[hh] = _dot(k_ref[pl.ds(start, t), lo:lo + HEAD_PAD], qt_ref[qi, lo:lo + HEAD_PAD, :])

        def update(j, src, stats, masked):
            m, acc = stats
            s = src[hh]
            if masked:
                key_ids = lax.broadcasted_iota(jnp.int32, (t, t), 0)
                qry_ids = lax.broadcasted_iota(jnp.int32, (t, t), 1)
                s = jnp.where(key_ids <= qry_ids, s, -jnp.inf)
            m_new = jnp.maximum(m, jnp.max(s, axis=0, keepdims=True))
            p = jnp.exp2(s - m_new).astype(BF16)
            a = jnp.exp2(m - m_new)
            v_ext = jnp.concatenate([vt_ref[j, hh * B_VDIM:(hh + 1) * B_VDIM, :], ones], axis=0)
            return m_new, a * acc + _dot(v_ext, p)

        def two_blocks(i, stats):
            j = 2 * i
            scores(j + 1, sb_ref)
            stats = update(j, sa_ref, stats, False)
            scores(j + 2, sa_ref)
            return update(j + 1, sb_ref, stats, False)

        def four_blocks(i, stats):
            return two_blocks(2 * i + 1, two_blocks(2 * i, stats))

        def eight_blocks(i, stats):
            return four_blocks(2 * i + 1, four_blocks(2 * i, stats))

        def tail_odd(stats):
            scores(qi, sb_ref)
            stats = update(qi - 1, sa_ref, stats, False)
            return update(qi, sb_ref, stats, True)

        def tail_even(stats):
            return update(qi, sa_ref, stats, True)

        scores(0, sa_ref)
        stats = (jnp.full((1, t), -jnp.inf, F32), jnp.zeros((B_VDIM + ATTN_SUM_ROWS, t), F32))
        stats = lax.fori_loop(0, qi // 8, eight_blocks, stats)
        stats = lax.fori_loop(2 * (qi // 8), qi // 4, four_blocks, stats)
        stats = lax.fori_loop(2 * (qi // 4), qi // 2, two_blocks, stats)
        _, acc = lax.cond(qi % 2 == 1, tail_odd, tail_even, stats)
        return acc[:B_VDIM] / acc[B_VDIM:B_VDIM + 1]

    lax.fori_loop(0, qt_ref.shape[0], one_tile, 0)


def _attn(qt, k, vt, *, batch, seq):
    t = qt.shape[2]
    n = k.shape[0]
    nq = seq // t
    return pl.pallas_call(
        _attn_kernel,
        grid=(batch, B_HEADS // 2),
        in_specs=[pl.BlockSpec((nq, 2 * HEAD_PAD, t), lambda b, h: (b, h, 0)),
                  pl.BlockSpec((seq, 2 * HEAD_PAD), lambda b, h: (b, h)),
                  pl.BlockSpec((nq, 2 * B_VDIM, t), lambda b, h: (b, h, 0))],
        out_specs=pl.BlockSpec((seq, 2 * B_VDIM), lambda b, h: (b, h)),
        out_shape=jax.ShapeDtypeStruct((n, B_HEADS * B_VDIM), BF16),
        scratch_shapes=[pltpu.VMEM((2, t, t), F32), pltpu.VMEM((2, t, t), F32)],
        compiler_params=pltpu.CompilerParams(dimension_semantics=("parallel", "parallel"),
                                             vmem_limit_bytes=VMEM_LIMIT),
        name="attn",
    )(qt, k, vt)


def _odd_pre_kernel(x_ref, win_ref, wzg_ref, wgate_ref, bgate_ref,
                    q_ref, k_ref, v_ref, sg_ref, la_ref):
    hk, hv = C_HEADS * C_DK, C_HEADS * C_DV
    tm = x_ref.shape[0]

    def in_proj(rows):
        xb = x_ref[rows, :].astype(BF16)
        return _dot(xb, win_ref[...]), _dot(xb, wzg_ref[...])

    def finish(rows, z, zg):
        q_ref[rows, :] = z[:, :hk] * (C_DK ** -0.5)
        k_ref[rows, :] = z[:, hk:2 * hk]
        v_ref[rows, :] = z[:, 2 * hk:2 * hk + hv].astype(v_ref.dtype)
        g = z[:, 2 * hk + hv:]
        sg_ref[rows, :] = g * jax.nn.sigmoid(g)
        logits = _dot(zg.astype(BF16), wgate_ref[...]) + bgate_ref[...]
        la_ref[rows, :] = -(jnp.maximum(-logits, 0.0) + jnp.log1p(jnp.exp(-jnp.abs(logits)))) * (1.0 / C_GATE_TAU)

    blocks = [slice(r, r + ODD_ROW_BLOCK) for r in range(0, tm, ODD_ROW_BLOCK)]
    cur = in_proj(blocks[0])
    for r, rows in enumerate(blocks):
        nxt = in_proj(blocks[r + 1]) if r + 1 < len(blocks) else None
        finish(rows, *cur)
        cur = nxt


def _odd_pre(x, *params, layer):
    n = x.shape[0]
    tm = TILES["odd_pre"]
    row = lambda w: pl.BlockSpec((tm, w), lambda i: (i, 0))
    hk, hv = C_HEADS * C_DK, C_HEADS * C_DV
    return pl.pallas_call(
        _odd_pre_kernel,
        grid=(n // tm,),
        in_specs=[row(D_MODEL), _layer_spec(params[0], layer, cols=2 * hk + 2 * hv)]
                 + [_layer_spec(a, layer) for a in params[1:]],
        out_specs=[row(hk), row(hk), row(hv), row(hv), row(hk)],
        out_shape=[jax.ShapeDtypeStruct((n, hk), F32),
                   jax.ShapeDtypeStruct((n, hk), F32),
                   jax.ShapeDtypeStruct((n, hv), BF16),
                   jax.ShapeDtypeStruct((n, hv), F32),
                   jax.ShapeDtypeStruct((n, hk), F32)],
        compiler_params=pltpu.CompilerParams(dimension_semantics=("parallel",),
                                             vmem_limit_bytes=VMEM_LIMIT),
        name="odd_pre",
    )(x, *params)


def _gla_level_matrix():
    c = GLA_CHUNK
    t = jnp.arange(c)[:, None]
    u = jnp.arange(c)[None, :]
    mats = []
    for m in GLA_FINE_LEVELS:
        r = t % (2 * m)
        p = t - r + m - 1
        upper = (r >= m) & (u > p) & (u <= t)
        lower = (r < m) & (u > t) & (u <= p)
        mats.append(upper | lower)
    return jnp.concatenate(mats, axis=0).astype(BF16)


def _split_bf16(x, pieces):
    out = []
    for _ in range(pieces - 1):
        hi = x.astype(BF16)
        out.append(hi)
        x = x - hi.astype(F32)
    out.append(x.astype(BF16))
    return out


def _gla_kernel(q_ref, k_ref, v_ref, la_ref, sg_ref, lng_ref, lnb_ref, tri_ref, lvl_ref,
                y_ref, st_ref, a_scr):
    c = GLA_CHUNK
    hk = C_HEADS * C_DK
    n_chunks = q_ref.shape[0] // c

    @pl.when(pl.program_id(1) == 0)
    def _():
        st_ref[...] = jnp.zeros_like(st_ref)

    row = lax.broadcasted_iota(jnp.int32, (c, 1), 0)
    pair = lax.broadcasted_iota(jnp.int32, (c, c), 0) ^ lax.broadcasted_iota(jnp.int32, (c, c), 1)

    def intra(ci):
        rows = slice(ci * c, (ci + 1) * c)
        q = q_ref[rows, :]
        k = k_ref[rows, :]
        g = _split_bf16(la_ref[rows, :], 3)
        tri = tri_ref[...]
        b = _dot(tri, g[0]) + _dot(tri, g[1]) + _dot(tri, g[2])
        fine = _dot(lvl_ref[...], g[0]) + _dot(lvl_ref[...], g[1])

        qb = q.astype(BF16)
        kb = k.astype(BF16)
        for h in range(C_HEADS):
            hs = slice(h * C_DK, (h + 1) * C_DK)
            a_scr[ci, h] = jnp.where(pair == 0, _dot_nt(qb[:, hs], kb[:, hs]), 0.0)

        for m in GLA_BLOCK_LEVELS:
            for base in range(0, c, 2 * m):
                lower, upper = slice(base, base + m), slice(base + m, base + 2 * m)
                b_ref_row = b[base + m - 1:base + m, :]
                qe = (q[upper, :] * jnp.exp(b[upper, :] - b_ref_row)).astype(BF16)
                ke = (k[lower, :] * jnp.exp(b_ref_row - b[lower, :])).astype(BF16)
                for h in range(C_HEADS):
                    hs = slice(h * C_DK, (h + 1) * C_DK)
                    a_scr[ci, h, upper, lower] = _dot_nt(qe[:, hs], ke[:, hs])

        m = 1
        while m < min(GLA_BLOCK_LEVELS):
            if m in GLA_FINE_LEVELS:
                i = GLA_FINE_LEVELS.index(m)
                d = fine[i * c:(i + 1) * c, :]
            else:
                b3 = b.reshape(c // (2 * m), 2 * m, hk)
                d3 = b3 - b3[:, m - 1:m, :]
                sub = lax.broadcasted_iota(jnp.int32, (1, 2 * m, 1), 1)
                d = jnp.where(sub >= m, d3, -d3).reshape(c, hk)
            e = jnp.exp(d)
            upper = (row & m) != 0
            qe = jnp.where(upper, q * e, 0.0).astype(BF16)
            ke = jnp.where(upper, 0.0, k * e).astype(BF16)
            level = (pair >= m) & (pair < 2 * m)
            for h in range(C_HEADS):
                hs = slice(h * C_DK, (h + 1) * C_DK)
                a_scr[ci, h] = jnp.where(level, _dot_nt(qe[:, hs], ke[:, hs]), a_scr[ci, h])
            m *= 2

        b_last = b[c - 1:c, :]
        qs = (q * jnp.exp(b)).astype(BF16)
        kd = (k * jnp.exp(b_last - b)).astype(BF16)
        return qs, kd, jnp.exp(b_last)

    def carry_state(ci, qs, kd, decay):
        rows = slice(ci * c, (ci + 1) * c)
        for h in range(C_HEADS):
            hs = slice(h * C_DK, (h + 1) * C_DK)
            vs = slice(h * C_DV, (h + 1) * C_DV)
            v = v_ref[rows, vs]
            st = st_ref[h]
            o = _dot(a_scr[ci, h].astype(BF16), v) + _dot_nt(qs[:, hs], st.astype(BF16))
            st_ref[h] = st * decay[:, hs] + _dot_tn(v, kd[:, hs])
            on = _ln(o, lng_ref[...], lnb_ref[...])
            y_ref[rows, vs] = (on * sg_ref[rows, vs]).astype(y_ref.dtype)

    parts = [intra(ci) for ci in range(n_chunks)]
    for ci in range(n_chunks):
        carry_state(ci, *parts[ci])


def _gla(q, k, v, la, sg, lng, lnb, *, batch, seq, layer):
    n = q.shape[0]
    c = GLA_CHUNK
    tm = TILES["gla"]
    nt = seq // tm
    hk, hv = C_HEADS * C_DK, C_HEADS * C_DV
    tri = jnp.tril(jnp.ones((c, c), F32)).astype(BF16)
    lvl = _gla_level_matrix()
    kblk = pl.BlockSpec((tm, hk), lambda b, i: (b * nt + i, 0))
    vblk = pl.BlockSpec((tm, hv), lambda b, i: (b * nt + i, 0))
    full2 = lambda a: pl.BlockSpec(a.shape, lambda b, i: (0, 0))
    return pl.pallas_call(
        _gla_kernel,
        grid=(batch, nt),
        in_specs=[kblk, kblk, vblk, kblk, vblk, _layer_spec(lng, layer), _layer_spec(lnb, layer),
                  full2(tri), full2(lvl)],
        out_specs=vblk,
        out_shape=jax.ShapeDtypeStruct((n, hv), BF16),
        scratch_shapes=[pltpu.VMEM((C_HEADS, C_DV, C_DK), F32), pltpu.VMEM((tm // c, C_HEADS, c, c), F32)],
        compiler_params=pltpu.CompilerParams(dimension_semantics=("parallel", "arbitrary"),
                                             vmem_limit_bytes=VMEM_LIMIT),
        name="gla",
    )(q, k, v, la, sg, lng, lnb, tri, lvl)


def _post_ffn_kernel(x_ref, y1_ref, y2_ref, wo_ref, g1_ref, b1_ref, w1_ref, w2_ref, g2_ref, b2_ref,
                     o_ref, x1_ref, x1b_ref, acc_ref):
    i = pl.program_id(0)
    j = pl.program_id(1)
    nt = pl.num_programs(0) - 1
    half = y1_ref.shape[1]
    tm = x_ref.shape[0]
    blocks = [slice(r, r + FFN_ROW_BLOCK) for r in range(0, tm, FFN_ROW_BLOCK)]

    def mlp(x1b):
        h = _dot(x1b, w1_ref[...])
        return _dot(jnp.square(jnp.maximum(h, 0.0)).astype(BF16), w2_ref[...])

    def close_tile(rows):
        o_ref[rows, :] = _ln(ALPHA * x1_ref[rows, :] + acc_ref[rows, :], g2_ref[...], b2_ref[...])

    def out_proj(rows):
        return _dot(y1_ref[rows, :], wo_ref[0:half, :]) + _dot(y2_ref[rows, :], wo_ref[half:2 * half, :])

    def open_tile(rows, y):
        x1 = _ln(ALPHA * x_ref[rows, :] + y, g1_ref[...], b1_ref[...])
        x1_ref[rows, :] = x1
        x1b = x1.astype(BF16)
        x1b_ref[rows, :] = x1b
        acc_ref[rows, :] = mlp(x1b)

    def open_all(close_first):
        y = out_proj(blocks[0])
        for r, rows in enumerate(blocks):
            y_next = out_proj(blocks[r + 1]) if r + 1 < len(blocks) else None
            if close_first:
                close_tile(rows)
            open_tile(rows, y)
            y = y_next

    @pl.when((j == 0) & (i == 0))
    def _():
        open_all(False)

    @pl.when((j == 0) & (i > 0) & (i < nt))
    def _():
        open_all(True)

    @pl.when((j == 0) & (i == nt))
    def _():
        for rows in blocks:
            close_tile(rows)

    @pl.when((j > 0) & (i < nt))
    def _():
        acc_ref[...] += mlp(x1b_ref[...])


def _post_ffn(x, y1, y2, y2_col, wo, g1, b1, w1, w2, g2, b2, *, layer):
    n = x.shape[0]
    tm, tf = TILES["ffn_rows"], TILES["ffn_cols"]
    nt = n // tm
    half = D_MODEL // 2
    opened = lambda i: jnp.minimum(i, nt - 1)
    vec = pl.BlockSpec((None, 1, D_MODEL), lambda i, j: (layer, 0, 0))
    return pl.pallas_call(
        _post_ffn_kernel,
        grid=(nt + 1, D_FF // tf),
        in_specs=[pl.BlockSpec((tm, D_MODEL), lambda i, j: (opened(i), 0)),
                  pl.BlockSpec((tm, half), lambda i, j: (opened(i), 0)),
                  pl.BlockSpec((tm, half), lambda i, j: (opened(i), y2_col)),
                  pl.BlockSpec((None, D_MODEL, D_MODEL), lambda i, j: (layer, 0, 0)),
                  vec, vec,
                  pl.BlockSpec((None, D_MODEL, tf), lambda i, j: (layer, 0, j)),
                  pl.BlockSpec((None, tf, D_MODEL), lambda i, j: (layer, j, 0)),
                  vec, vec],
        out_specs=pl.BlockSpec((tm, D_MODEL), lambda i, j: (jnp.maximum(i - 1, 0), 0)),
        out_shape=jax.ShapeDtypeStruct((n, D_MODEL), F32),
        scratch_shapes=[pltpu.VMEM((tm, D_MODEL), F32), pltpu.VMEM((tm, D_MODEL), BF16),
                        pltpu.VMEM((tm, D_MODEL), F32)],
        compiler_params=pltpu.CompilerParams(dimension_semantics=("arbitrary", "arbitrary"),
                                             vmem_limit_bytes=VMEM_LIMIT),
        name="post_ffn",
    )(x, y1, y2, wo, g1, b1, w1, w2, g2, b2)


def kernel(x, positions, ln1_g, ln1_b, ln2_g, ln2_b, w_in_even, a_w_s, a_b_s, a_ln_g, a_ln_b, b_q_norm, b_kv_norm, b_w_uq, b_w_ukv, w_out_even, w_in_odd, c_w_gate, c_b_gate, c_ln_g, c_ln_b, w_out_odd, w_ff1, w_ff2):
    bn, s, d = x.shape
    n = bn * s
    xf = x.reshape(n, d)
    half = B_ROPE // 2
    inv = ROPE_THETA ** (-jnp.arange(half, dtype=F32) / half)
    cos, sin = _rope_tables(positions.reshape(1, n).astype(jnp.int32), inv.reshape(half, 1))
    n_even = w_in_even.shape[0]
    vec = lambda a: a.reshape(a.shape[0], 1, -1)

    win_even = jnp.pad(w_in_even, ((0, 0), (0, 0), (0, HEAD_PAD - B_ROPE))).astype(BF16)
    wuq = b_w_uq.reshape(n_even, B_Q_RANK, B_HEADS, B_NOPE + B_ROPE)
    wuq = jnp.pad(wuq, ((0, 0), (0, 0), (0, 0), (0, HEAD_PAD - B_NOPE - B_ROPE)))
    wuqt = wuq.reshape(n_even, B_Q_RANK, B_HEADS * HEAD_PAD).transpose(0, 2, 1).astype(BF16)
    wukv = b_w_ukv.reshape(n_even, B_KV_RANK, B_HEADS, B_NOPE + B_VDIM)
    wuk = jnp.pad(wukv[..., :B_NOPE], ((0, 0), (0, 0), (0, 0), (0, HEAD_PAD - B_NOPE)))
    wuk = wuk.reshape(n_even, B_KV_RANK, B_HEADS * HEAD_PAD).astype(BF16)
    wuvvt = wukv[..., B_NOPE:].reshape(n_even, B_KV_RANK, B_HEADS * B_VDIM).transpose(0, 2, 1).astype(BF16)
    causal = jnp.tril(jnp.ones((A_CHUNK, A_CHUNK), dtype=bool))
    ws = jnp.where(causal, a_w_s, 0.0).astype(BF16)
    bs = jnp.broadcast_to(a_b_s[..., None], a_b_s.shape + (A_GROUP_DIM,))
    even_params = (win_even, ws, bs, a_ln_g, a_ln_b, vec(b_q_norm), vec(b_kv_norm), wuqt, wuk, wuvvt)

    hk, hv = C_HEADS * C_DK, C_HEADS * C_DV
    win_odd = w_in_odd.astype(BF16)
    wzg = jnp.pad(w_in_odd[:, :, 2 * hk + 2 * hv:], ((0, 0), (0, 0), (0, 128 - C_GATE_RANK))).astype(BF16)
    wgate = jnp.pad(c_w_gate, ((0, 0), (0, 128 - C_GATE_RANK), (0, 0))).astype(BF16)
    odd_params = (win_odd, wzg, wgate, vec(c_b_gate))
    c_lng, c_lnb = vec(c_ln_g), vec(c_ln_b)

    w_out = jnp.stack([(w_out_even if layer % 2 == 0 else w_out_odd)[layer // 2]
                       for layer in range(DEPTH)]).astype(BF16)
    w1, w2 = w_ff1.astype(BF16), w_ff2.astype(BF16)
    g1, b1, g2, b2 = vec(ln1_g), vec(ln1_b), vec(ln2_g), vec(ln2_b)

    for layer in range(DEPTH):
        j = layer // 2
        if layer % 2 == 0:
            ya, qt, k, vt = _even_pre(xf, cos, sin, *even_params, layer=j)
            yb = _attn(qt, k, vt, batch=bn, seq=s)
            y1, y2, y2_col = ya, yb, 0
        else:
            q, k, v, sg, la = _odd_pre(xf, *odd_params, layer=j)
            y = _gla(q, k, v, la, sg, c_lng, c_lnb, batch=bn, seq=s, layer=j)
            y1, y2, y2_col = y, y, 1
        xf = _post_ffn(xf, y1, y2, y2_col, w_out, g1, b1, w1, w2, g2, b2, layer=layer)
    return xf.reshape(bn, s, d)
```

```python
import jax
import jax.numpy as jnp
from jax import lax
from jax.experimental import pallas as pl
from jax.experimental.pallas import tpu as pltpu

F32 = jnp.float32
BF16 = jnp.bfloat16

D_MODEL = 1024
DEPTH = 4

A_CHUNK = 128
A_GROUPS = 4
A_WIDTH = 512
A_GROUP_DIM = 128

B_HEADS = 8
B_NOPE = 64
B_ROPE = 32
B_VDIM = 64
B_Q_RANK = 384
B_KV_RANK = 256
ROPE_THETA = 10000.0
HEAD_PAD = 128
ATTN_SUM_ROWS = 16
ATTN_HEADS = 4

C_HEADS = 4
C_DK = 128
C_DV = 256
C_GATE_RANK = 16
C_GATE_TAU = 16.0
GLA_CHUNK = 256
GLA_FINE_LEVELS = (1, 2, 4)
GLA_BLOCK_LEVELS = (128, 64)

D_FF = 4 * D_MODEL
ALPHA = (2.0 * DEPTH) ** 0.25
LN_EPS = 1e-5
LOG2_E = 1.4426950408889634

VMEM_LIMIT = 56 * 1024 * 1024

TILES = {
    "attn": 512,
    "odd_pre": 1024,
    "rope": 2048,
    "gla": 1024,
    "ffn_rows": 1024,
    "ffn_cols": 1024,
}
PRE_ROW_BLOCK = 256
ODD_ROW_BLOCK = 256
FFN_ROW_BLOCK = 256


def _dot(a, b):
    return jnp.dot(a, b, preferred_element_type=F32)


def _dot_nt(a, b):
    return lax.dot_general(a, b, (((1,), (1,)), ((), ())), preferred_element_type=F32)


def _dot_tn(a, b):
    return lax.dot_general(a, b, (((0,), (0,)), ((), ())), preferred_element_type=F32)


def _ln(x, g, b):
    mu = jnp.mean(x, axis=-1, keepdims=True)
    xc = x - mu
    var = jnp.mean(xc * xc, axis=-1, keepdims=True)
    return xc * lax.rsqrt(var + LN_EPS) * g + b


def _rms(x, g):
    ms = jnp.mean(x * x, axis=-1, keepdims=True)
    return x * lax.rsqrt(ms + LN_EPS) * g


def _gelu(x):
    return 0.5 * x * (1.0 + jnp.tanh(0.7978845608028654 * (x + 0.044715 * (x * x * x))))


def _layer_spec(a, layer, cols=None):
    zeros = (0,) * (a.ndim - 1)
    shape = a.shape[1:] if cols is None else a.shape[1:-1] + (cols,)
    return pl.BlockSpec((None,) + shape, lambda *_: (layer,) + zeros)


def _rope_table_kernel(pos_ref, inv_ref, cos_ref, sin_ref):
    ang = inv_ref[...] * pos_ref[...].astype(F32)
    cos_ref[...] = jnp.cos(ang)
    sin_ref[...] = jnp.sin(ang)


def _rope_tables(pos_row, inv_col):
    n = pos_row.shape[1]
    tm = TILES["rope"]
    half = inv_col.shape[0]
    table = jax.ShapeDtypeStruct((half, n), F32)
    return pl.pallas_call(
        _rope_table_kernel,
        grid=(n // tm,),
        in_specs=[pl.BlockSpec((1, tm), lambda i: (0, i)), pl.BlockSpec((half, 1), lambda i: (0, 0))],
        out_specs=[pl.BlockSpec((half, tm), lambda i: (0, i))] * 2,
        out_shape=[table, table],
        compiler_params=pltpu.CompilerParams(dimension_semantics=("parallel",)),
        name="rope_tables",
    )(pos_row, inv_col)


def _even_pre_kernel(x_ref, cos_ref, sin_ref, win_ref,
                     ws_ref, bs_ref, alng_ref, alnb_ref, gq_ref, gkv_ref,
                     wuqt_ref, wuk_ref, wuvvt_ref,
                     ya_ref, qt_ref, k_ref, vt_ref):
    tm = x_ref.shape[0]
    o_cq = 2 * A_WIDTH
    o_ckv = o_cq + B_Q_RANK
    o_kr = o_ckv + B_KV_RANK
    half = B_ROPE // 2
    scale = (B_NOPE + B_ROPE) ** -0.5 * LOG2_E

    def in_proj(rows):
        return (_dot(x_ref[rows, :].astype(BF16), win_ref[...]),)

    def finish(rows, z):
        for g in range(A_GROUPS):
            lo = g * A_GROUP_DIM
            u = _gelu(z[:, lo:lo + A_GROUP_DIM])
            vv = _gelu(z[:, A_WIDTH + lo:A_WIDTH + lo + A_GROUP_DIM])
            vn = _ln(vv, alng_ref[g:g + 1, :], alnb_ref[g:g + 1, :]).astype(BF16)
            for r in range(0, rows.stop - rows.start, A_CHUNK):
                mixed = _dot(ws_ref[g], vn[r:r + A_CHUNK, :]) + bs_ref[g]
                ya_ref[rows.start + r:rows.start + r + A_CHUNK, lo:lo + A_GROUP_DIM] = (
                    u[r:r + A_CHUNK, :] * mixed).astype(ya_ref.dtype)

        cos_t = cos_ref[:, rows]
        sin_t = sin_ref[:, rows]

        def rope_t(blk):
            x1 = blk[B_NOPE:B_NOPE + half, :]
            x2 = blk[B_NOPE + half:B_NOPE + B_ROPE, :]
            return jnp.concatenate([blk[:B_NOPE, :], x1 * cos_t - x2 * sin_t, x2 * cos_t + x1 * sin_t,
                                    blk[B_NOPE + B_ROPE:, :]], axis=0)

        ckvn = _rms(z[:, o_ckv:o_kr], gkv_ref[...]).astype(BF16)
        kn = _dot(ckvn, wuk_ref[...])
        kr_t = z[:, o_kr:o_kr + HEAD_PAD].T
        kr_t = jnp.concatenate([jnp.zeros((B_NOPE, kr_t.shape[1]), F32), kr_t[:B_ROPE, :],
                                jnp.zeros((HEAD_PAD - B_NOPE - B_ROPE, kr_t.shape[1]), F32)], axis=0)
        kr = rope_t(kr_t).T
        for h in range(B_HEADS):
            lo = h * HEAD_PAD
            k_ref[rows, lo:lo + HEAD_PAD] = (kn[:, lo:lo + HEAD_PAD] + kr).astype(k_ref.dtype)

        cqn = _rms(z[:, o_cq:o_ckv], gq_ref[...]).astype(BF16)
        q_t = _dot_nt(wuqt_ref[...], cqn)
        for h in range(B_HEADS):
            lo = h * HEAD_PAD
            qt_ref[0, lo:lo + HEAD_PAD, rows] = (rope_t(q_t[lo:lo + HEAD_PAD, :]) * scale).astype(qt_ref.dtype)
        vt_ref[0, :, rows] = _dot_nt(wuvvt_ref[...], ckvn).astype(vt_ref.dtype)

    blocks = [slice(r, r + PRE_ROW_BLOCK) for r in range(0, tm, PRE_ROW_BLOCK)]
    cur = in_proj(blocks[0])
    for r, rows in enumerate(blocks):
        nxt = in_proj(blocks[r + 1]) if r + 1 < len(blocks) else None
        finish(rows, *cur)
        cur = nxt


def _even_pre(x, cos, sin, *params, layer):
    n = x.shape[0]
    tm = TILES["attn"]
    row = lambda w: pl.BlockSpec((tm, w), lambda i: (i, 0))
    table = pl.BlockSpec((B_ROPE // 2, tm), lambda i: (0, i))
    hp = B_HEADS * HEAD_PAD
    return pl.pallas_call(
        _even_pre_kernel,
        grid=(n // tm,),
        in_specs=[row(D_MODEL), table, table] + [_layer_spec(a, layer) for a in params],
        out_specs=[row(A_WIDTH),
                   pl.BlockSpec((1, hp, tm), lambda i: (i, 0, 0)),
                   row(hp),
                   pl.BlockSpec((1, B_HEADS * B_VDIM, tm), lambda i: (i, 0, 0))],
        out_shape=[jax.ShapeDtypeStruct((n, A_WIDTH), BF16),
                   jax.ShapeDtypeStruct((n // tm, hp, tm), BF16),
                   jax.ShapeDtypeStruct((n, hp), BF16),
                   jax.ShapeDtypeStruct((n // tm, B_HEADS * B_VDIM, tm), BF16)],
        compiler_params=pltpu.CompilerParams(dimension_semantics=("parallel",),
                                             vmem_limit_bytes=VMEM_LIMIT),
        name="even_pre",
    )(x, cos, sin, *params)


def _attn_kernel(qt_ref, k_ref, vt_ref, o_ref, sa_ref, sb_ref):
    t = qt_ref.shape[2]
    ones = jnp.ones((ATTN_SUM_ROWS, t), BF16)

    def one_tile(qi, _):

        def scores(j, dst):
            start = pl.multiple_of(j * t, t)
            for hh in range(ATTN_HEADS):
                lo = hh * HEAD_PAD
                dst[hh] = _dot(k_ref[pl.ds(start, t), lo:lo + HEAD_PAD], qt_ref[0, lo:lo + HEAD_PAD, :])

        def update(j, src, stats, masked):
            vt = vt_ref[j]
            new = []
            for hh in range(ATTN_HEADS):
                m, acc = stats[hh]
                s = src[hh]
                if masked:
                    key_ids = lax.broadcasted_iota(jnp.int32, (t, t), 0)
                    qry_ids = lax.broadcasted_iota(jnp.int32, (t, t), 1)
                    s = jnp.where(key_ids <= qry_ids, s, -jnp.inf)
                m_new = jnp.maximum(m, jnp.max(s, axis=0, keepdims=True))
                p = jnp.exp2(s - m_new).astype(BF16)
                a = jnp.exp2(m - m_new)
                v_ext = jnp.concatenate([vt[hh * B_VDIM:(hh + 1) * B_VDIM, :], ones], axis=0)
                acc = a * acc + _dot(v_ext, p)
                new.append((m_new, acc))
            return tuple(new)

        def two_blocks(i, stats):
            j = 2 * i
            scores(j + 1, sb_ref)
            stats = update(j, sa_ref, stats, False)
            scores(j + 2, sa_ref)
            return update(j + 1, sb_ref, stats, False)

        def four_blocks(i, stats):
            return two_blocks(2 * i + 1, two_blocks(2 * i, stats))

        def tail_odd(stats):
            scores(qi, sb_ref)
            stats = update(qi - 1, sa_ref, stats, False)
            return update(qi, sb_ref, stats, True)

        def tail_even(stats):
            return update(qi, sa_ref, stats, True)

        scores(0, sa_ref)
        init1 = (jnp.full((1, t), -jnp.inf, F32), jnp.zeros((B_VDIM + ATTN_SUM_ROWS, t), F32))
        def eight_blocks(i, stats):
            return four_blocks(2 * i + 1, four_blocks(2 * i, stats))

        stats = lax.fori_loop(0, qi // 8, eight_blocks, (init1,) * ATTN_HEADS)
        stats = lax.fori_loop(2 * (qi // 8), qi // 4, four_blocks, stats)
        stats = lax.fori_loop(2 * (qi // 4), qi // 2, two_blocks, stats)
        stats = lax.cond(qi % 2 == 1, tail_odd, tail_even, stats)
        out_t = jnp.concatenate([acc[:B_VDIM] / acc[B_VDIM:B_VDIM + 1] for (_, acc) in stats], axis=0)
        o_ref[...] = out_t.T.astype(o_ref.dtype)

    one_tile(pl.program_id(2), None)


def _attn(qt, k, vt, *, batch, seq):
    t = qt.shape[2]
    n = k.shape[0]
    nq = seq // t
    return pl.pallas_call(
        _attn_kernel,
        grid=(batch, B_HEADS // ATTN_HEADS, nq),
        in_specs=[pl.BlockSpec((1, ATTN_HEADS * HEAD_PAD, t), lambda b, h, i: (b * nq + i, h, 0)),
                  pl.BlockSpec((seq, ATTN_HEADS * HEAD_PAD), lambda b, h, i: (b, h)),
                  pl.BlockSpec((nq, ATTN_HEADS * B_VDIM, t), lambda b, h, i: (b, h, 0))],
        out_specs=pl.BlockSpec((t, ATTN_HEADS * B_VDIM), lambda b, h, i: (b * nq + i, h)),
        out_shape=jax.ShapeDtypeStruct((n, B_HEADS * B_VDIM), BF16),
        scratch_shapes=[pltpu.VMEM((ATTN_HEADS, t, t), F32), pltpu.VMEM((ATTN_HEADS, t, t), F32)],
        compiler_params=pltpu.CompilerParams(dimension_semantics=("parallel", "parallel", "arbitrary"),
                                             vmem_limit_bytes=VMEM_LIMIT),
        name="attn",
    )(qt, k, vt)


def _odd_pre_kernel(x_ref, win_ref, wzg_ref, wgate_ref, bgate_ref,
                    q_ref, k_ref, v_ref, sg_ref, la_ref):
    hk, hv = C_HEADS * C_DK, C_HEADS * C_DV
    tm = x_ref.shape[0]

    def in_proj(rows):
        xb = x_ref[rows, :].astype(BF16)
        return _dot(xb, win_ref[...]), _dot(xb, wzg_ref[...])

    def finish(rows, z, zg):
        q_ref[rows, :] = z[:, :hk] * (C_DK ** -0.5)
        k_ref[rows, :] = z[:, hk:2 * hk]
        v_ref[rows, :] = z[:, 2 * hk:2 * hk + hv].astype(v_ref.dtype)
        g = z[:, 2 * hk + hv:]
        sg_ref[rows, :] = g * jax.nn.sigmoid(g)
        logits = _dot(zg.astype(BF16), wgate_ref[...]) + bgate_ref[...]
        la_ref[rows, :] = -(jnp.maximum(-logits, 0.0) + jnp.log1p(jnp.exp(-jnp.abs(logits)))) * (1.0 / C_GATE_TAU)

    blocks = [slice(r, r + ODD_ROW_BLOCK) for r in range(0, tm, ODD_ROW_BLOCK)]
    cur = in_proj(blocks[0])
    for r, rows in enumerate(blocks):
        nxt = in_proj(blocks[r + 1]) if r + 1 < len(blocks) else None
        finish(rows, *cur)
        cur = nxt


def _odd_pre(x, *params, layer):
    n = x.shape[0]
    tm = TILES["odd_pre"]
    row = lambda w: pl.BlockSpec((tm, w), lambda i: (i, 0))
    hk, hv = C_HEADS * C_DK, C_HEADS * C_DV
    return pl.pallas_call(
        _odd_pre_kernel,
        grid=(n // tm,),
        in_specs=[row(D_MODEL), _layer_spec(params[0], layer, cols=2 * hk + 2 * hv)]
                 + [_layer_spec(a, layer) for a in params[1:]],
        out_specs=[row(hk), row(hk), row(hv), row(hv), row(hk)],
        out_shape=[jax.ShapeDtypeStruct((n, hk), F32),
                   jax.ShapeDtypeStruct((n, hk), F32),
                   jax.ShapeDtypeStruct((n, hv), BF16),
                   jax.ShapeDtypeStruct((n, hv), F32),
                   jax.ShapeDtypeStruct((n, hk), F32)],
        compiler_params=pltpu.CompilerParams(dimension_semantics=("parallel",),
                                             vmem_limit_bytes=VMEM_LIMIT),
        name="odd_pre",
    )(x, *params)


def _gla_level_matrix():
    c = GLA_CHUNK
    t = jnp.arange(c)[:, None]
    u = jnp.arange(c)[None, :]
    mats = []
    for m in GLA_FINE_LEVELS:
        r = t % (2 * m)
        p = t - r + m - 1
        upper = (r >= m) & (u > p) & (u <= t)
        lower = (r < m) & (u > t) & (u <= p)
        mats.append(upper | lower)
    return jnp.concatenate(mats, axis=0).astype(BF16)


def _split_bf16(x, pieces):
    out = []
    for _ in range(pieces - 1):
        hi = x.astype(BF16)
        out.append(hi)
        x = x - hi.astype(F32)
    out.append(x.astype(BF16))
    return out


def _gla_kernel(q_ref, k_ref, v_ref, la_ref, sg_ref, lng_ref, lnb_ref, tri_ref, lvl_ref,
                y_ref, st_ref, a_scr):
    c = GLA_CHUNK
    hk = C_HEADS * C_DK
    n_chunks = q_ref.shape[0] // c

    @pl.when(pl.program_id(1) == 0)
    def _():
        st_ref[...] = jnp.zeros_like(st_ref)

    row = lax.broadcasted_iota(jnp.int32, (c, 1), 0)
    pair = lax.broadcasted_iota(jnp.int32, (c, c), 0) ^ lax.broadcasted_iota(jnp.int32, (c, c), 1)

    def intra(ci):
        rows = slice(ci * c, (ci + 1) * c)
        q = q_ref[rows, :]
        k = k_ref[rows, :]
        g = _split_bf16(la_ref[rows, :], 3)
        tri = tri_ref[...]
        b = _dot(tri, g[0]) + _dot(tri, g[1]) + _dot(tri, g[2])
        fine = _dot(lvl_ref[...], g[0]) + _dot(lvl_ref[...], g[1])

        qb = q.astype(BF16)
        kb = k.astype(BF16)
        for h in range(C_HEADS):
            hs = slice(h * C_DK, (h + 1) * C_DK)
            a_scr[ci, h] = jnp.where(pair == 0, _dot_nt(qb[:, hs], kb[:, hs]), 0.0)

        for m in GLA_BLOCK_LEVELS:
            for base in range(0, c, 2 * m):
                lower, upper = slice(base, base + m), slice(base + m, base + 2 * m)
                b_ref_row = b[base + m - 1:base + m, :]
                qe = (q[upper, :] * jnp.exp(b[upper, :] - b_ref_row)).astype(BF16)
                ke = (k[lower, :] * jnp.exp(b_ref_row - b[lower, :])).astype(BF16)
                for h in range(C_HEADS):
                    hs = slice(h * C_DK, (h + 1) * C_DK)
                    a_scr[ci, h, upper, lower] = _dot_nt(qe[:, hs], ke[:, hs])

        m = 1
        while m < min(GLA_BLOCK_LEVELS):
            if m in GLA_FINE_LEVELS:
                i = GLA_FINE_LEVELS.index(m)
                d = fine[i * c:(i + 1) * c, :]
            else:
                b3 = b.reshape(c // (2 * m), 2 * m, hk)
                d3 = b3 - b3[:, m - 1:m, :]
                sub = lax.broadcasted_iota(jnp.int32, (1, 2 * m, 1), 1)
                d = jnp.where(sub >= m, d3, -d3).reshape(c, hk)
            e = jnp.exp(d)
            upper = (row & m) != 0
            qe = jnp.where(upper, q * e, 0.0).astype(BF16)
            ke = jnp.where(upper, 0.0, k * e).astype(BF16)
            level = (pair >= m) & (pair < 2 * m)
            for h in range(C_HEADS):
                hs = slice(h * C_DK, (h + 1) * C_DK)
                a_scr[ci, h] = jnp.where(level, _dot_nt(qe[:, hs], ke[:, hs]), a_scr[ci, h])
            m *= 2

        b_last = b[c - 1:c, :]
        qs = (q * jnp.exp(b)).astype(BF16)
        kd = (k * jnp.exp(b_last - b)).astype(BF16)
        return qs, kd, jnp.exp(b_last)

    def carry_state(ci, qs, kd, decay):
        rows = slice(ci * c, (ci + 1) * c)
        for h in range(C_HEADS):
            hs = slice(h * C_DK, (h + 1) * C_DK)
            vs = slice(h * C_DV, (h + 1) * C_DV)
            v = v_ref[rows, vs]
            st = st_ref[h]
            o = _dot(a_scr[ci, h].astype(BF16), v) + _dot_nt(qs[:, hs], st.astype(BF16))
            st_ref[h] = st * decay[:, hs] + _dot_tn(v, kd[:, hs])
            on = _ln(o, lng_ref[...], lnb_ref[...])
            y_ref[rows, vs] = (on * sg_ref[rows, vs]).astype(y_ref.dtype)

    parts = [intra(ci) for ci in range(n_chunks)]
    for ci in range(n_chunks):
        carry_state(ci, *parts[ci])


def _gla(q, k, v, la, sg, lng, lnb, *, batch, seq, layer):
    n = q.shape[0]
    c = GLA_CHUNK
    tm = TILES["gla"]
    nt = seq // tm
    hk, hv = C_HEADS * C_DK, C_HEADS * C_DV
    tri = jnp.tril(jnp.ones((c, c), F32)).astype(BF16)
    lvl = _gla_level_matrix()
    kblk = pl.BlockSpec((tm, hk), lambda b, i: (b * nt + i, 0))
    vblk = pl.BlockSpec((tm, hv), lambda b, i: (b * nt + i, 0))
    full2 = lambda a: pl.BlockSpec(a.shape, lambda b, i: (0, 0))
    return pl.pallas_call(
        _gla_kernel,
        grid=(batch, nt),
        in_specs=[kblk, kblk, vblk, kblk, vblk, _layer_spec(lng, layer), _layer_spec(lnb, layer),
                  full2(tri), full2(lvl)],
        out_specs=vblk,
        out_shape=jax.ShapeDtypeStruct((n, hv), BF16),
        scratch_shapes=[pltpu.VMEM((C_HEADS, C_DV, C_DK), F32), pltpu.VMEM((tm // c, C_HEADS, c, c), F32)],
        compiler_params=pltpu.CompilerParams(dimension_semantics=("parallel", "arbitrary"),
                                             vmem_limit_bytes=VMEM_LIMIT),
        name="gla",
    )(q, k, v, la, sg, lng, lnb, tri, lvl)


def _post_ffn_kernel(x_ref, y1_ref, y2_ref, wo_ref, g1_ref, b1_ref, w1_ref, w2_ref, g2_ref, b2_ref,
                     o_ref, x1_ref, x1b_ref, acc_ref):
    i = pl.program_id(0)
    j = pl.program_id(1)
    nt = pl.num_programs(0) - 1
    half = y1_ref.shape[1]
    tm = x_ref.shape[0]
    blocks = [slice(r, r + FFN_ROW_BLOCK) for r in range(0, tm, FFN_ROW_BLOCK)]

    def mlp(x1b):
        h = _dot(x1b, w1_ref[...])
        return _dot(jnp.square(jnp.maximum(h, 0.0)).astype(BF16), w2_ref[...])

    def close_tile(rows):
        o_ref[rows, :] = _ln(ALPHA * x1_ref[rows, :] + acc_ref[rows, :], g2_ref[...], b2_ref[...])

    def out_proj(rows):
        return _dot(y1_ref[rows, :], wo_ref[0:half, :]) + _dot(y2_ref[rows, :], wo_ref[half:2 * half, :])

    def open_tile(rows, y):
        x1 = _ln(ALPHA * x_ref[rows, :] + y, g1_ref[...], b1_ref[...])
        x1_ref[rows, :] = x1
        x1b = x1.astype(BF16)
        x1b_ref[rows, :] = x1b
        acc_ref[rows, :] = mlp(x1b)

    def open_all(close_first):
        y = out_proj(blocks[0])
        for r, rows in enumerate(blocks):
            y_next = out_proj(blocks[r + 1]) if r + 1 < len(blocks) else None
            if close_first:
                close_tile(rows)
            open_tile(rows, y)
            y = y_next

    @pl.when((j == 0) & (i == 0))
    def _():
        open_all(False)

    @pl.when((j == 0) & (i > 0) & (i < nt))
    def _():
        open_all(True)

    @pl.when((j == 0) & (i == nt))
    def _():
        for rows in blocks:
            close_tile(rows)

    @pl.when((j > 0) & (i < nt))
    def _():
        acc_ref[...] += mlp(x1b_ref[...])


def _post_ffn(x, y1, y2, y2_col, wo, g1, b1, w1, w2, g2, b2, *, layer):
    n = x.shape[0]
    tm, tf = TILES["ffn_rows"], TILES["ffn_cols"]
    nt = n // tm
    half = D_MODEL // 2
    opened = lambda i: jnp.minimum(i, nt - 1)
    vec = pl.BlockSpec((None, 1, D_MODEL), lambda i, j: (layer, 0, 0))
    return pl.pallas_call(
        _post_ffn_kernel,
        grid=(nt + 1, D_FF // tf),
        in_specs=[pl.BlockSpec((tm, D_MODEL), lambda i, j: (opened(i), 0)),
                  pl.BlockSpec((tm, half), lambda i, j: (opened(i), 0)),
                  pl.BlockSpec((tm, half), lambda i, j: (opened(i), y2_col)),
                  pl.BlockSpec((None, D_MODEL, D_MODEL), lambda i, j: (layer, 0, 0)),
                  vec, vec,
                  pl.BlockSpec((None, D_MODEL, tf), lambda i, j: (layer, 0, j)),
                  pl.BlockSpec((None, tf, D_MODEL), lambda i, j: (layer, j, 0)),
                  vec, vec],
        out_specs=pl.BlockSpec((tm, D_MODEL), lambda i, j: (jnp.maximum(i - 1, 0), 0)),
        out_shape=jax.ShapeDtypeStruct((n, D_MODEL), F32),
        scratch_shapes=[pltpu.VMEM((tm, D_MODEL), F32), pltpu.VMEM((tm, D_MODEL), BF16),
                        pltpu.VMEM((tm, D_MODEL), F32)],
        compiler_params=pltpu.CompilerParams(dimension_semantics=("arbitrary", "arbitrary"),
                                             vmem_limit_bytes=VMEM_LIMIT),
        name="post_ffn",
    )(x, y1, y2, wo, g1, b1, w1, w2, g2, b2)


def kernel(x, positions, ln1_g, ln1_b, ln2_g, ln2_b, w_in_even, a_w_s, a_b_s, a_ln_g, a_ln_b, b_q_norm, b_kv_norm, b_w_uq, b_w_ukv, w_out_even, w_in_odd, c_w_gate, c_b_gate, c_ln_g, c_ln_b, w_out_odd, w_ff1, w_ff2):
    bn, s, d = x.shape
    n = bn * s
    xf = x.reshape(n, d)
    half = B_ROPE // 2
    inv = ROPE_THETA ** (-jnp.arange(half, dtype=F32) / half)
    cos, sin = _rope_tables(positions.reshape(1, n).astype(jnp.int32), inv.reshape(half, 1))
    n_even = w_in_even.shape[0]
    vec = lambda a: a.reshape(a.shape[0], 1, -1)

    win_even = jnp.pad(w_in_even, ((0, 0), (0, 0), (0, HEAD_PAD - B_ROPE))).astype(BF16)
    wuq = b_w_uq.reshape(n_even, B_Q_RANK, B_HEADS, B_NOPE + B_ROPE)
    wuq = jnp.pad(wuq, ((0, 0), (0, 0), (0, 0), (0, HEAD_PAD - B_NOPE - B_ROPE)))
    wuqt = wuq.reshape(n_even, B_Q_RANK, B_HEADS * HEAD_PAD).transpose(0, 2, 1).astype(BF16)
    wukv = b_w_ukv.reshape(n_even, B_KV_RANK, B_HEADS, B_NOPE + B_VDIM)
    wuk = jnp.pad(wukv[..., :B_NOPE], ((0, 0), (0, 0), (0, 0), (0, HEAD_PAD - B_NOPE)))
    wuk = wuk.reshape(n_even, B_KV_RANK, B_HEADS * HEAD_PAD).astype(BF16)
    wuvvt = wukv[..., B_NOPE:].reshape(n_even, B_KV_RANK, B_HEADS * B_VDIM).transpose(0, 2, 1).astype(BF16)
    causal = jnp.tril(jnp.ones((A_CHUNK, A_CHUNK), dtype=bool))
    ws = jnp.where(causal, a_w_s, 0.0).astype(BF16)
    bs = jnp.broadcast_to(a_b_s[..., None], a_b_s.shape + (A_GROUP_DIM,))
    even_params = (win_even, ws, bs, a_ln_g, a_ln_b, vec(b_q_norm), vec(b_kv_norm), wuqt, wuk, wuvvt)

    hk, hv = C_HEADS * C_DK, C_HEADS * C_DV
    win_odd = w_in_odd.astype(BF16)
    wzg = jnp.pad(w_in_odd[:, :, 2 * hk + 2 * hv:], ((0, 0), (0, 0), (0, 128 - C_GATE_RANK))).astype(BF16)
    wgate = jnp.pad(c_w_gate, ((0, 0), (0, 128 - C_GATE_RANK), (0, 0))).astype(BF16)
    odd_params = (win_odd, wzg, wgate, vec(c_b_gate))
    c_lng, c_lnb = vec(c_ln_g), vec(c_ln_b)

    w_out = jnp.stack([(w_out_even if layer % 2 == 0 else w_out_odd)[layer // 2]
                       for layer in range(DEPTH)]).astype(BF16)
    w1, w2 = w_ff1.astype(BF16), w_ff2.astype(BF16)
    g1, b1, g2, b2 = vec(ln1_g), vec(ln1_b), vec(ln2_g), vec(ln2_b)

    for layer in range(DEPTH):
        j = layer // 2
        if layer % 2 == 0:
            ya, qt, k, vt = _even_pre(xf, cos, sin, *even_params, layer=j)
            yb = _attn(qt, k, vt, batch=bn, seq=s)
            y1, y2, y2_col = ya, yb, 0
        else:
            q, k, v, sg, la = _odd_pre(xf, *odd_params, layer=j)
            y = _gla(q, k, v, la, sg, c_lng, c_lnb, batch=bn, seq=s, layer=j)
            y1, y2, y2_col = y, y, 1
        xf = _post_ffn(xf, y1, y2, y2_col, w_out, g1, b1, w1, w2, g2, b2, layer=layer)
    return xf.reshape(bn, s, d)
```

```python
import jax
import jax.numpy as jnp
from jax import lax
from jax.experimental import pallas as pl
from jax.experimental.pallas import tpu as pltpu

F32 = jnp.float32
BF16 = jnp.bfloat16

D_MODEL = 1024
DEPTH = 4

A_CHUNK = 128
A_GROUPS = 4
A_WIDTH = 512
A_GROUP_DIM = 128

B_HEADS = 8
B_NOPE = 64
B_ROPE = 32
B_VDIM = 64
B_Q_RANK = 384
B_KV_RANK = 256
ROPE_THETA = 10000.0
HEAD_PAD = 128
ATTN_SUM_ROWS = 16
ATTN_HEADS = 4

C_HEADS = 4
C_DK = 128
C_DV = 256
C_GATE_RANK = 16
C_GATE_TAU = 16.0
GLA_CHUNK = 256
GLA_FINE_LEVELS = (1, 2, 4)
GLA_BLOCK_LEVELS = (128, 64)

D_FF = 4 * D_MODEL
ALPHA = (2.0 * DEPTH) ** 0.25
LN_EPS = 1e-5
LOG2_E = 1.4426950408889634

VMEM_LIMIT = 56 * 1024 * 1024

TILES = {
    "attn": 512,
    "rope": 2048,
    "gla": 1024,
    "ffn_rows": 1024,
    "ffn_cols": 1024,
}
PRE_ROW_BLOCK = 256
ODD_ROW_BLOCK = 256
FFN_ROW_BLOCK = 256


def _dot(a, b):
    return jnp.dot(a, b, preferred_element_type=F32)


def _dot_nt(a, b):
    return lax.dot_general(a, b, (((1,), (1,)), ((), ())), preferred_element_type=F32)


def _dot_tn(a, b):
    return lax.dot_general(a, b, (((0,), (0,)), ((), ())), preferred_element_type=F32)


def _ln(x, g, b):
    mu = jnp.mean(x, axis=-1, keepdims=True)
    xc = x - mu
    var = jnp.mean(xc * xc, axis=-1, keepdims=True)
    return xc * lax.rsqrt(var + LN_EPS) * g + b


def _rms(x, g):
    ms = jnp.mean(x * x, axis=-1, keepdims=True)
    return x * lax.rsqrt(ms + LN_EPS) * g


def _gelu(x):
    return 0.5 * x * (1.0 + jnp.tanh(0.7978845608028654 * (x + 0.044715 * (x * x * x))))


def _layer_spec(a, layer, cols=None):
    zeros = (0,) * (a.ndim - 1)
    shape = a.shape[1:] if cols is None else a.shape[1:-1] + (cols,)
    return pl.BlockSpec((None,) + shape, lambda *_: (layer,) + zeros)


def _rope_table_kernel(pos_ref, inv_ref, cos_ref, sin_ref):
    ang = inv_ref[...] * pos_ref[...].astype(F32)
    cos_ref[...] = jnp.cos(ang)
    sin_ref[...] = jnp.sin(ang)


def _rope_tables(pos_row, inv_col):
    n = pos_row.shape[1]
    tm = TILES["rope"]
    half = inv_col.shape[0]
    table = jax.ShapeDtypeStruct((half, n), F32)
    return pl.pallas_call(
        _rope_table_kernel,
        grid=(n // tm,),
        in_specs=[pl.BlockSpec((1, tm), lambda i: (0, i)), pl.BlockSpec((half, 1), lambda i: (0, 0))],
        out_specs=[pl.BlockSpec((half, tm), lambda i: (0, i))] * 2,
        out_shape=[table, table],
        compiler_params=pltpu.CompilerParams(dimension_semantics=("parallel",)),
        name="rope_tables",
    )(pos_row, inv_col)


def _even_pre_kernel(x_ref, cos_ref, sin_ref, win_ref,
                     ws_ref, bs_ref, alng_ref, alnb_ref, gq_ref, gkv_ref,
                     wuqt_ref, wuk_ref, wuvvt_ref,
                     ya_ref, qt_ref, k_ref, vt_ref):
    tm = x_ref.shape[0]
    o_cq = 2 * A_WIDTH
    o_ckv = o_cq + B_Q_RANK
    o_kr = o_ckv + B_KV_RANK
    half = B_ROPE // 2
    scale = (B_NOPE + B_ROPE) ** -0.5 * LOG2_E

    def in_proj(rows):
        return (_dot(x_ref[rows, :].astype(BF16), win_ref[...]),)

    def finish(rows, z):
        for g in range(A_GROUPS):
            lo = g * A_GROUP_DIM
            u = _gelu(z[:, lo:lo + A_GROUP_DIM])
            vv = _gelu(z[:, A_WIDTH + lo:A_WIDTH + lo + A_GROUP_DIM])
            vn = _ln(vv, alng_ref[g:g + 1, :], alnb_ref[g:g + 1, :]).astype(BF16)
            for r in range(0, rows.stop - rows.start, A_CHUNK):
                mixed = _dot(ws_ref[g], vn[r:r + A_CHUNK, :]) + bs_ref[g]
                ya_ref[rows.start + r:rows.start + r + A_CHUNK, lo:lo + A_GROUP_DIM] = (
                    u[r:r + A_CHUNK, :] * mixed).astype(ya_ref.dtype)

        cos_t = cos_ref[:, rows]
        sin_t = sin_ref[:, rows]

        def rope_t(blk):
            x1 = blk[B_NOPE:B_NOPE + half, :]
            x2 = blk[B_NOPE + half:B_NOPE + B_ROPE, :]
            return jnp.concatenate([blk[:B_NOPE, :], x1 * cos_t - x2 * sin_t, x2 * cos_t + x1 * sin_t,
                                    blk[B_NOPE + B_ROPE:, :]], axis=0)

        ckvn = _rms(z[:, o_ckv:o_kr], gkv_ref[...]).astype(BF16)
        kn = _dot(ckvn, wuk_ref[...])
        kr_t = z[:, o_kr:o_kr + HEAD_PAD].T
        kr_t = jnp.concatenate([jnp.zeros((B_NOPE, kr_t.shape[1]), F32), kr_t[:B_ROPE, :],
                                jnp.zeros((HEAD_PAD - B_NOPE - B_ROPE, kr_t.shape[1]), F32)], axis=0)
        kr = rope_t(kr_t).T
        for h in range(B_HEADS):
            lo = h * HEAD_PAD
            k_ref[rows, lo:lo + HEAD_PAD] = (kn[:, lo:lo + HEAD_PAD] + kr).astype(k_ref.dtype)

        cqn = _rms(z[:, o_cq:o_ckv], gq_ref[...]).astype(BF16)
        q_t = _dot_nt(wuqt_ref[...], cqn)
        for h in range(B_HEADS):
            lo = h * HEAD_PAD
            qt_ref[0, lo:lo + HEAD_PAD, rows] = (rope_t(q_t[lo:lo + HEAD_PAD, :]) * scale).astype(qt_ref.dtype)
        vt_ref[0, :, rows] = _dot_nt(wuvvt_ref[...], ckvn).astype(vt_ref.dtype)

    blocks = [slice(r, r + PRE_ROW_BLOCK) for r in range(0, tm, PRE_ROW_BLOCK)]
    cur = in_proj(blocks[0])
    for r, rows in enumerate(blocks):
        nxt = in_proj(blocks[r + 1]) if r + 1 < len(blocks) else None
        finish(rows, *cur)
        cur = nxt


def _even_pre(x, cos, sin, *params, layer):
    n = x.shape[0]
    tm = TILES["attn"]
    row = lambda w: pl.BlockSpec((tm, w), lambda i: (i, 0))
    table = pl.BlockSpec((B_ROPE // 2, tm), lambda i: (0, i))
    hp = B_HEADS * HEAD_PAD
    return pl.pallas_call(
        _even_pre_kernel,
        grid=(n // tm,),
        in_specs=[row(D_MODEL), table, table] + [_layer_spec(a, layer) for a in params],
        out_specs=[row(A_WIDTH),
                   pl.BlockSpec((1, hp, tm), lambda i: (i, 0, 0)),
                   row(hp),
                   pl.BlockSpec((1, B_HEADS * B_VDIM, tm), lambda i: (i, 0, 0))],
        out_shape=[jax.ShapeDtypeStruct((n, A_WIDTH), BF16),
                   jax.ShapeDtypeStruct((n // tm, hp, tm), BF16),
                   jax.ShapeDtypeStruct((n, hp), BF16),
                   jax.ShapeDtypeStruct((n // tm, B_HEADS * B_VDIM, tm), BF16)],
        compiler_params=pltpu.CompilerParams(dimension_semantics=("parallel",),
                                             vmem_limit_bytes=VMEM_LIMIT),
        name="even_pre",
    )(x, cos, sin, *params)


def _attn_kernel(qt_ref, k_ref, vt_ref, o_ref, sa_ref, sb_ref):
    t = qt_ref.shape[2]
    ones = jnp.ones((ATTN_SUM_ROWS, t), BF16)

    def one_tile(qi, _):

        def scores(j, dst):
            start = pl.multiple_of(j * t, t)
            for hh in range(ATTN_HEADS):
                lo = hh * HEAD_PAD
                dst[hh] = _dot(k_ref[pl.ds(start, t), lo:lo + HEAD_PAD], qt_ref[0, lo:lo + HEAD_PAD, :])

        def update(j, src, stats, masked):
            vt = vt_ref[j]
            new = []
            for hh in range(ATTN_HEADS):
                m, acc = stats[hh]
                s = src[hh]
                if masked:
                    key_ids = lax.broadcasted_iota(jnp.int32, (t, t), 0)
                    qry_ids = lax.broadcasted_iota(jnp.int32, (t, t), 1)
                    s = jnp.where(key_ids <= qry_ids, s, -jnp.inf)
                m_new = jnp.maximum(m, jnp.max(s, axis=0, keepdims=True))
                p = jnp.exp2(s - m_new).astype(BF16)
                a = jnp.exp2(m - m_new)
                v_ext = jnp.concatenate([vt[hh * B_VDIM:(hh + 1) * B_VDIM, :], ones], axis=0)
                acc = a * acc + _dot(v_ext, p)
                new.append((m_new, acc))
            return tuple(new)

        def two_blocks(i, stats):
            j = 2 * i
            scores(j + 1, sb_ref)
            stats = update(j, sa_ref, stats, False)
            scores(j + 2, sa_ref)
            return update(j + 1, sb_ref, stats, False)

        def four_blocks(i, stats):
            return two_blocks(2 * i + 1, two_blocks(2 * i, stats))

        def tail_odd(stats):
            scores(qi, sb_ref)
            stats = update(qi - 1, sa_ref, stats, False)
            return update(qi, sb_ref, stats, True)

        def tail_even(stats):
            return update(qi, sa_ref, stats, True)

        scores(0, sa_ref)
        init1 = (jnp.full((1, t), -jnp.inf, F32), jnp.zeros((B_VDIM + ATTN_SUM_ROWS, t), F32))
        def eight_blocks(i, stats):
            return four_blocks(2 * i + 1, four_blocks(2 * i, stats))

        stats = lax.fori_loop(0, qi // 8, eight_blocks, (init1,) * ATTN_HEADS)
        stats = lax.fori_loop(2 * (qi // 8), qi // 4, four_blocks, stats)
        stats = lax.fori_loop(2 * (qi // 4), qi // 2, two_blocks, stats)
        stats = lax.cond(qi % 2 == 1, tail_odd, tail_even, stats)
        out_t = jnp.concatenate([acc[:B_VDIM] / acc[B_VDIM:B_VDIM + 1] for (_, acc) in stats], axis=0)
        o_ref[...] = out_t.T.astype(o_ref.dtype)

    one_tile(pl.program_id(2), None)


def _attn(qt, k, vt, *, batch, seq):
    t = qt.shape[2]
    n = k.shape[0]
    nq = seq // t
    return pl.pallas_call(
        _attn_kernel,
        grid=(batch, B_HEADS // ATTN_HEADS, nq),
        in_specs=[pl.BlockSpec((1, ATTN_HEADS * HEAD_PAD, t), lambda b, h, i: (b * nq + i, h, 0)),
                  pl.BlockSpec((seq, ATTN_HEADS * HEAD_PAD), lambda b, h, i: (b, h)),
                  pl.BlockSpec((nq, ATTN_HEADS * B_VDIM, t), lambda b, h, i: (b, h, 0))],
        out_specs=pl.BlockSpec((t, ATTN_HEADS * B_VDIM), lambda b, h, i: (b * nq + i, h)),
        out_shape=jax.ShapeDtypeStruct((n, B_HEADS * B_VDIM), BF16),
        scratch_shapes=[pltpu.VMEM((ATTN_HEADS, t, t), F32), pltpu.VMEM((ATTN_HEADS, t, t), F32)],
        compiler_params=pltpu.CompilerParams(dimension_semantics=("parallel", "parallel", "arbitrary"),
                                             vmem_limit_bytes=VMEM_LIMIT),
        name="attn",
    )(qt, k, vt)


def _odd_pre_kernel(x_ref, win_ref, wzg_ref, wgate_ref, bgate_ref,
                    q_ref, k_ref, v_ref, sg_ref, la_ref):
    hk, hv = C_HEADS * C_DK, C_HEADS * C_DV
    tm = x_ref.shape[0]

    def in_proj(rows):
        xb = x_ref[rows, :].astype(BF16)
        return _dot(xb, win_ref[...]), _dot(xb, wzg_ref[...])

    def finish(rows, z, zg):
        q_ref[rows, :] = z[:, :hk] * (C_DK ** -0.5)
        k_ref[rows, :] = z[:, hk:2 * hk]
        v_ref[rows, :] = z[:, 2 * hk:2 * hk + hv].astype(v_ref.dtype)
        g = z[:, 2 * hk + hv:]
        sg_ref[rows, :] = g * jax.nn.sigmoid(g)
        logits = _dot(zg.astype(BF16), wgate_ref[...]) + bgate_ref[...]
        la_ref[rows, :] = -(jnp.maximum(-logits, 0.0) + jnp.log1p(jnp.exp(-jnp.abs(logits)))) * (1.0 / C_GATE_TAU)

    blocks = [slice(r, r + ODD_ROW_BLOCK) for r in range(0, tm, ODD_ROW_BLOCK)]
    cur = in_proj(blocks[0])
    for r, rows in enumerate(blocks):
        nxt = in_proj(blocks[r + 1]) if r + 1 < len(blocks) else None
        finish(rows, *cur)
        cur = nxt


def _gla_level_matrix():
    c = GLA_CHUNK
    t = jnp.arange(c)[:, None]
    u = jnp.arange(c)[None, :]
    mats = []
    for m in GLA_FINE_LEVELS:
        r = t % (2 * m)
        p = t - r + m - 1
        upper = (r >= m) & (u > p) & (u <= t)
        lower = (r < m) & (u > t) & (u <= p)
        mats.append(upper | lower)
    return jnp.concatenate(mats, axis=0).astype(BF16)


def _split_bf16(x, pieces):
    out = []
    for _ in range(pieces - 1):
        hi = x.astype(BF16)
        out.append(hi)
        x = x - hi.astype(F32)
    out.append(x.astype(BF16))
    return out


def _gla_kernel(q_ref, k_ref, v_ref, la_ref, sg_ref, lng_ref, lnb_ref, tri_ref, lvl_ref,
                y_ref, st_ref, a_scr):
    c = GLA_CHUNK
    hk = C_HEADS * C_DK
    n_chunks = q_ref.shape[0] // c

    @pl.when(pl.program_id(1) == 0)
    def _():
        st_ref[...] = jnp.zeros_like(st_ref)

    row = lax.broadcasted_iota(jnp.int32, (c, 1), 0)
    pair = lax.broadcasted_iota(jnp.int32, (c, c), 0) ^ lax.broadcasted_iota(jnp.int32, (c, c), 1)

    def intra(ci):
        rows = slice(ci * c, (ci + 1) * c)
        q = q_ref[rows, :]
        k = k_ref[rows, :]
        g = _split_bf16(la_ref[rows, :], 3)
        tri = tri_ref[...]
        b = _dot(tri, g[0]) + _dot(tri, g[1]) + _dot(tri, g[2])
        fine = _dot(lvl_ref[...], g[0]) + _dot(lvl_ref[...], g[1])

        qb = q.astype(BF16)
        kb = k.astype(BF16)
        for h in range(C_HEADS):
            hs = slice(h * C_DK, (h + 1) * C_DK)
            a_scr[ci, h] = jnp.where(pair == 0, _dot_nt(qb[:, hs], kb[:, hs]), 0.0)

        for m in GLA_BLOCK_LEVELS:
            for base in range(0, c, 2 * m):
                lower, upper = slice(base, base + m), slice(base + m, base + 2 * m)
                b_ref_row = b[base + m - 1:base + m, :]
                qe = (q[upper, :] * jnp.exp(b[upper, :] - b_ref_row)).astype(BF16)
                ke = (k[lower, :] * jnp.exp(b_ref_row - b[lower, :])).astype(BF16)
                for h in range(C_HEADS):
                    hs = slice(h * C_DK, (h + 1) * C_DK)
                    a_scr[ci, h, upper, lower] = _dot_nt(qe[:, hs], ke[:, hs])

        m = 1
        while m < min(GLA_BLOCK_LEVELS):
            if m in GLA_FINE_LEVELS:
                i = GLA_FINE_LEVELS.index(m)
                d = fine[i * c:(i + 1) * c, :]
            else:
                b3 = b.reshape(c // (2 * m), 2 * m, hk)
                d3 = b3 - b3[:, m - 1:m, :]
                sub = lax.broadcasted_iota(jnp.int32, (1, 2 * m, 1), 1)
                d = jnp.where(sub >= m, d3, -d3).reshape(c, hk)
            e = jnp.exp(d)
            upper = (row & m) != 0
            qe = jnp.where(upper, q * e, 0.0).astype(BF16)
            ke = jnp.where(upper, 0.0, k * e).astype(BF16)
            level = (pair >= m) & (pair < 2 * m)
            for h in range(C_HEADS):
                hs = slice(h * C_DK, (h + 1) * C_DK)
                a_scr[ci, h] = jnp.where(level, _dot_nt(qe[:, hs], ke[:, hs]), a_scr[ci, h])
            m *= 2

        b_last = b[c - 1:c, :]
        qs = (q * jnp.exp(b)).astype(BF16)
        kd = (k * jnp.exp(b_last - b)).astype(BF16)
        return qs, kd, jnp.exp(b_last)

    def carry_state(ci, qs, kd, decay):
        rows = slice(ci * c, (ci + 1) * c)
        for h in range(C_HEADS):
            hs = slice(h * C_DK, (h + 1) * C_DK)
            vs = slice(h * C_DV, (h + 1) * C_DV)
            v = v_ref[rows, vs]
            st = st_ref[h]
            o = _dot(a_scr[ci, h].astype(BF16), v) + _dot_nt(qs[:, hs], st.astype(BF16))
            st_ref[h] = st * decay[:, hs] + _dot_tn(v, kd[:, hs])
            on = _ln(o, lng_ref[...], lnb_ref[...])
            y_ref[rows, vs] = (on * sg_ref[rows, vs]).astype(y_ref.dtype)

    parts = [intra(ci) for ci in range(n_chunks)]
    for ci in range(n_chunks):
        carry_state(ci, *parts[ci])


def _odd_mixer_kernel(x_ref, win_ref, wzg_ref, wgate_ref, bgate_ref, lng_ref, lnb_ref, tri_ref, lvl_ref,
                      y_ref, st_ref, a_scr, q_scr, k_scr, v_scr, sg_scr, la_scr):
    _odd_pre_kernel(x_ref, win_ref, wzg_ref, wgate_ref, bgate_ref, q_scr, k_scr, v_scr, sg_scr, la_scr)
    _gla_kernel(q_scr, k_scr, v_scr, la_scr, sg_scr, lng_ref, lnb_ref, tri_ref, lvl_ref, y_ref, st_ref, a_scr)


def _odd_mixer(x, win, wzg, wgate, bgate, lng, lnb, *, batch, seq, layer):
    n = x.shape[0]
    c = GLA_CHUNK
    tm = TILES["gla"]
    nt = seq // tm
    hk, hv = C_HEADS * C_DK, C_HEADS * C_DV
    tri = jnp.tril(jnp.ones((c, c), F32)).astype(BF16)
    lvl = _gla_level_matrix()
    full2 = lambda a: pl.BlockSpec(a.shape, lambda b, i: (0, 0))
    return pl.pallas_call(
        _odd_mixer_kernel,
        grid=(batch, nt),
        in_specs=[pl.BlockSpec((tm, D_MODEL), lambda b, i: (b * nt + i, 0)),
                  _layer_spec(win, layer, cols=2 * hk + 2 * hv), _layer_spec(wzg, layer),
                  _layer_spec(wgate, layer), _layer_spec(bgate, layer),
                  _layer_spec(lng, layer), _layer_spec(lnb, layer), full2(tri), full2(lvl)],
        out_specs=pl.BlockSpec((tm, hv), lambda b, i: (b * nt + i, 0)),
        out_shape=jax.ShapeDtypeStruct((n, hv), BF16),
        scratch_shapes=[pltpu.VMEM((C_HEADS, C_DV, C_DK), F32), pltpu.VMEM((tm // c, C_HEADS, c, c), F32),
                        pltpu.VMEM((tm, hk), F32), pltpu.VMEM((tm, hk), F32), pltpu.VMEM((tm, hv), BF16),
                        pltpu.VMEM((tm, hv), F32), pltpu.VMEM((tm, hk), F32)],
        compiler_params=pltpu.CompilerParams(dimension_semantics=("parallel", "arbitrary"),
                                             vmem_limit_bytes=VMEM_LIMIT),
        name="odd_mixer",
    )(x, win, wzg, wgate, bgate, lng, lnb, tri, lvl)


def _post_ffn_kernel(x_ref, y1_ref, y2_ref, wo_ref, g1_ref, b1_ref, w1_ref, w2_ref, g2_ref, b2_ref,
                     o_ref, x1_ref, x1b_ref, acc_ref):
    i = pl.program_id(0)
    j = pl.program_id(1)
    nt = pl.num_programs(0) - 1
    half = y1_ref.shape[1]
    tm = x_ref.shape[0]
    blocks = [slice(r, r + FFN_ROW_BLOCK) for r in range(0, tm, FFN_ROW_BLOCK)]

    def mlp(x1b):
        h = _dot(x1b, w1_ref[...])
        return _dot(jnp.square(jnp.maximum(h, 0.0)).astype(BF16), w2_ref[...])

    def close_tile(rows):
        o_ref[rows, :] = _ln(ALPHA * x1_ref[rows, :] + acc_ref[rows, :], g2_ref[...], b2_ref[...])

    def out_proj(rows):
        return _dot(y1_ref[rows, :], wo_ref[0:half, :]) + _dot(y2_ref[rows, :], wo_ref[half:2 * half, :])

    def open_tile(rows, y):
        x1 = _ln(ALPHA * x_ref[rows, :] + y, g1_ref[...], b1_ref[...])
        x1_ref[rows, :] = x1
        x1b = x1.astype(BF16)
        x1b_ref[rows, :] = x1b
        acc_ref[rows, :] = mlp(x1b)

    def open_all(close_first):
        y = out_proj(blocks[0])
        for r, rows in enumerate(blocks):
            y_next = out_proj(blocks[r + 1]) if r + 1 < len(blocks) else None
            if close_first:
                close_tile(rows)
            open_tile(rows, y)
            y = y_next

    @pl.when((j == 0) & (i == 0))
    def _():
        open_all(False)

    @pl.when((j == 0) & (i > 0) & (i < nt))
    def _():
        open_all(True)

    @pl.when((j == 0) & (i == nt))
    def _():
        for rows in blocks:
            close_tile(rows)

    @pl.when((j > 0) & (i < nt))
    def _():
        acc_ref[...] += mlp(x1b_ref[...])


def _post_ffn(x, y1, y2, y2_col, wo, g1, b1, w1, w2, g2, b2, *, layer):
    n = x.shape[0]
    tm, tf = TILES["ffn_rows"], TILES["ffn_cols"]
    nt = n // tm
    half = D_MODEL // 2
    opened = lambda i: jnp.minimum(i, nt - 1)
    vec = pl.BlockSpec((None, 1, D_MODEL), lambda i, j: (layer, 0, 0))
    return pl.pallas_call(
        _post_ffn_kernel,
        grid=(nt + 1, D_FF // tf),
        in_specs=[pl.BlockSpec((tm, D_MODEL), lambda i, j: (opened(i), 0)),
                  pl.BlockSpec((tm, half), lambda i, j: (opened(i), 0)),
                  pl.BlockSpec((tm, half), lambda i, j: (opened(i), y2_col)),
                  pl.BlockSpec((None, D_MODEL, D_MODEL), lambda i, j: (layer, 0, 0)),
                  vec, vec,
                  pl.BlockSpec((None, D_MODEL, tf), lambda i, j: (layer, 0, j)),
                  pl.BlockSpec((None, tf, D_MODEL), lambda i, j: (layer, j, 0)),
                  vec, vec],
        out_specs=pl.BlockSpec((tm, D_MODEL), lambda i, j: (jnp.maximum(i - 1, 0), 0)),
        out_shape=jax.ShapeDtypeStruct((n, D_MODEL), F32),
        scratch_shapes=[pltpu.VMEM((tm, D_MODEL), F32), pltpu.VMEM((tm, D_MODEL), BF16),
                        pltpu.VMEM((tm, D_MODEL), F32)],
        compiler_params=pltpu.CompilerParams(dimension_semantics=("arbitrary", "arbitrary"),
                                             vmem_limit_bytes=VMEM_LIMIT),
        name="post_ffn",
    )(x, y1, y2, wo, g1, b1, w1, w2, g2, b2)


def kernel(x, positions, ln1_g, ln1_b, ln2_g, ln2_b, w_in_even, a_w_s, a_b_s, a_ln_g, a_ln_b, b_q_norm, b_kv_norm, b_w_uq, b_w_ukv, w_out_even, w_in_odd, c_w_gate, c_b_gate, c_ln_g, c_ln_b, w_out_odd, w_ff1, w_ff2):
    bn, s, d = x.shape
    n = bn * s
    xf = x.reshape(n, d)
    half = B_ROPE // 2
    inv = ROPE_THETA ** (-jnp.arange(half, dtype=F32) / half)
    cos, sin = _rope_tables(positions.reshape(1, n).astype(jnp.int32), inv.reshape(half, 1))
    n_even = w_in_even.shape[0]
    vec = lambda a: a.reshape(a.shape[0], 1, -1)

    win_even = jnp.pad(w_in_even, ((0, 0), (0, 0), (0, HEAD_PAD - B_ROPE))).astype(BF16)
    wuq = b_w_uq.reshape(n_even, B_Q_RANK, B_HEADS, B_NOPE + B_ROPE)
    wuq = jnp.pad(wuq, ((0, 0), (0, 0), (0, 0), (0, HEAD_PAD - B_NOPE - B_ROPE)))
    wuqt = wuq.reshape(n_even, B_Q_RANK, B_HEADS * HEAD_PAD).transpose(0, 2, 1).astype(BF16)
    wukv = b_w_ukv.reshape(n_even, B_KV_RANK, B_HEADS, B_NOPE + B_VDIM)
    wuk = jnp.pad(wukv[..., :B_NOPE], ((0, 0), (0, 0), (0, 0), (0, HEAD_PAD - B_NOPE)))
    wuk = wuk.reshape(n_even, B_KV_RANK, B_HEADS * HEAD_PAD).astype(BF16)
    wuvvt = wukv[..., B_NOPE:].reshape(n_even, B_KV_RANK, B_HEADS * B_VDIM).transpose(0, 2, 1).astype(BF16)
    causal = jnp.tril(jnp.ones((A_CHUNK, A_CHUNK), dtype=bool))
    ws = jnp.where(causal, a_w_s, 0.0).astype(BF16)
    bs = jnp.broadcast_to(a_b_s[..., None], a_b_s.shape + (A_GROUP_DIM,))
    even_params = (win_even, ws, bs, a_ln_g, a_ln_b, vec(b_q_norm), vec(b_kv_norm), wuqt, wuk, wuvvt)

    hk, hv = C_HEADS * C_DK, C_HEADS * C_DV
    win_odd = w_in_odd.astype(BF16)
    wzg = jnp.pad(w_in_odd[:, :, 2 * hk + 2 * hv:], ((0, 0), (0, 0), (0, 128 - C_GATE_RANK))).astype(BF16)
    wgate = jnp.pad(c_w_gate, ((0, 0), (0, 128 - C_GATE_RANK), (0, 0))).astype(BF16)
    odd_params = (win_odd, wzg, wgate, vec(c_b_gate))
    c_lng, c_lnb = vec(c_ln_g), vec(c_ln_b)

    w_out = jnp.stack([(w_out_even if layer % 2 == 0 else w_out_odd)[layer // 2]
                       for layer in range(DEPTH)]).astype(BF16)
    w1, w2 = w_ff1.astype(BF16), w_ff2.astype(BF16)
    g1, b1, g2, b2 = vec(ln1_g), vec(ln1_b), vec(ln2_g), vec(ln2_b)

    for layer in range(DEPTH):
        j = layer // 2
        if layer % 2 == 0:
            ya, qt, k, vt = _even_pre(xf, cos, sin, *even_params, layer=j)
            yb = _attn(qt, k, vt, batch=bn, seq=s)
            y1, y2, y2_col = ya, yb, 0
        else:
            y = _odd_mixer(xf, *odd_params, c_lng, c_lnb, batch=bn, seq=s, layer=j)
            y1, y2, y2_col = y, y, 1
        xf = _post_ffn(xf, y1, y2, y2_col, w_out, g1, b1, w1, w2, g2, b2, layer=layer)
    return xf.reshape(bn, s, d)
```

```python
import jax
import jax.numpy as jnp
from jax import lax
from jax.experimental import pallas as pl
from jax.experimental.pallas import tpu as pltpu

F32 = jnp.float32
BF16 = jnp.bfloat16

D_MODEL = 1024
DEPTH = 4

A_CHUNK = 128
A_GROUPS = 4
A_WIDTH = 512
A_GROUP_DIM = 128

B_HEADS = 8
B_NOPE = 64
B_ROPE = 32
B_VDIM = 64
B_Q_RANK = 384
B_KV_RANK = 256
ROPE_THETA = 10000.0
HEAD_PAD = 128
ATTN_SUM_ROWS = 16
ATTN_HEADS = 4

C_HEADS = 4
C_DK = 128
C_DV = 256
C_GATE_RANK = 16
C_GATE_TAU = 16.0
GLA_CHUNK = 256
GLA_FINE_LEVELS = (1, 2, 4)
GLA_BLOCK_LEVELS = (128, 64)

D_FF = 4 * D_MODEL
ALPHA = (2.0 * DEPTH) ** 0.25
LN_EPS = 1e-5
LOG2_E = 1.4426950408889634

VMEM_LIMIT = 56 * 1024 * 1024

TILES = {
    "attn": 512,
    "gla": 1024,
    "ffn_rows": 1024,
    "ffn_cols": 1024,
}
PRE_ROW_BLOCK = 256
ODD_ROW_BLOCK = 256
FFN_ROW_BLOCK = 256


def _dot(a, b):
    return jnp.dot(a, b, preferred_element_type=F32)


def _dot_nt(a, b):
    return lax.dot_general(a, b, (((1,), (1,)), ((), ())), preferred_element_type=F32)


def _dot_tn(a, b):
    return lax.dot_general(a, b, (((0,), (0,)), ((), ())), preferred_element_type=F32)


def _ln(x, g, b):
    mu = jnp.mean(x, axis=-1, keepdims=True)
    xc = x - mu
    var = jnp.mean(xc * xc, axis=-1, keepdims=True)
    return xc * lax.rsqrt(var + LN_EPS) * g + b


def _rms(x, g):
    ms = jnp.mean(x * x, axis=-1, keepdims=True)
    return x * lax.rsqrt(ms + LN_EPS) * g


def _gelu(x):
    return 0.5 * x * (1.0 + jnp.tanh(0.7978845608028654 * (x + 0.044715 * (x * x * x))))


def _layer_spec(a, layer, cols=None):
    zeros = (0,) * (a.ndim - 1)
    shape = a.shape[1:] if cols is None else a.shape[1:-1] + (cols,)
    return pl.BlockSpec((None,) + shape, lambda *_: (layer,) + zeros)


def _even_pre_kernel(x_ref, pos_ref, inv_ref, win_ref,
                     ws_ref, bs_ref, alng_ref, alnb_ref, gq_ref, gkv_ref,
                     wuqt_ref, wuk_ref, wuvvt_ref,
                     ya_ref, qt_ref, k_ref, vt_ref):
    tm = x_ref.shape[0]
    o_cq = 2 * A_WIDTH
    o_ckv = o_cq + B_Q_RANK
    o_kr = o_ckv + B_KV_RANK
    half = B_ROPE // 2
    scale = (B_NOPE + B_ROPE) ** -0.5 * LOG2_E

    def in_proj(rows):
        return (_dot(x_ref[rows, :].astype(BF16), win_ref[...]),)

    def finish(rows, z):
        for g in range(A_GROUPS):
            lo = g * A_GROUP_DIM
            u = _gelu(z[:, lo:lo + A_GROUP_DIM])
            vv = _gelu(z[:, A_WIDTH + lo:A_WIDTH + lo + A_GROUP_DIM])
            vn = _ln(vv, alng_ref[g:g + 1, :], alnb_ref[g:g + 1, :]).astype(BF16)
            for r in range(0, rows.stop - rows.start, A_CHUNK):
                mixed = _dot(ws_ref[g], vn[r:r + A_CHUNK, :]) + bs_ref[g]
                ya_ref[rows.start + r:rows.start + r + A_CHUNK, lo:lo + A_GROUP_DIM] = (
                    u[r:r + A_CHUNK, :] * mixed).astype(ya_ref.dtype)

        ang = inv_ref[...] * pos_ref[:, rows].astype(F32)
        cos_t = jnp.cos(ang)
        sin_t = jnp.sin(ang)

        def rope_t(blk):
            x1 = blk[B_NOPE:B_NOPE + half, :]
            x2 = blk[B_NOPE + half:B_NOPE + B_ROPE, :]
            return jnp.concatenate([blk[:B_NOPE, :], x1 * cos_t - x2 * sin_t, x2 * cos_t + x1 * sin_t,
                                    blk[B_NOPE + B_ROPE:, :]], axis=0)

        ckvn = _rms(z[:, o_ckv:o_kr], gkv_ref[...]).astype(BF16)
        kn = _dot(ckvn, wuk_ref[...])
        kr_t = z[:, o_kr:o_kr + HEAD_PAD].T
        kr_t = jnp.concatenate([jnp.zeros((B_NOPE, kr_t.shape[1]), F32), kr_t[:B_ROPE, :],
                                jnp.zeros((HEAD_PAD - B_NOPE - B_ROPE, kr_t.shape[1]), F32)], axis=0)
        kr = rope_t(kr_t).T
        for h in range(B_HEADS):
            lo = h * HEAD_PAD
            k_ref[rows, lo:lo + HEAD_PAD] = (kn[:, lo:lo + HEAD_PAD] + kr).astype(k_ref.dtype)

        cqn = _rms(z[:, o_cq:o_ckv], gq_ref[...]).astype(BF16)
        q_t = _dot_nt(wuqt_ref[...], cqn)
        for h in range(B_HEADS):
            lo = h * HEAD_PAD
            qt_ref[0, lo:lo + HEAD_PAD, rows] = (rope_t(q_t[lo:lo + HEAD_PAD, :]) * scale).astype(qt_ref.dtype)
        vt_ref[0, :, rows] = _dot_nt(wuvvt_ref[...], ckvn).astype(vt_ref.dtype)

    blocks = [slice(r, r + PRE_ROW_BLOCK) for r in range(0, tm, PRE_ROW_BLOCK)]
    cur = in_proj(blocks[0])
    for r, rows in enumerate(blocks):
        nxt = in_proj(blocks[r + 1]) if r + 1 < len(blocks) else None
        finish(rows, *cur)
        cur = nxt


def _even_pre(x, pos_row, inv_col, *params, layer):
    n = x.shape[0]
    tm = TILES["attn"]
    row = lambda w: pl.BlockSpec((tm, w), lambda i: (i, 0))
    hp = B_HEADS * HEAD_PAD
    return pl.pallas_call(
        _even_pre_kernel,
        grid=(n // tm,),
        in_specs=[row(D_MODEL), pl.BlockSpec((1, tm), lambda i: (0, i)),
                  pl.BlockSpec(inv_col.shape, lambda i: (0, 0))] + [_layer_spec(a, layer) for a in params],
        out_specs=[row(A_WIDTH),
                   pl.BlockSpec((1, hp, tm), lambda i: (i, 0, 0)),
                   row(hp),
                   pl.BlockSpec((1, B_HEADS * B_VDIM, tm), lambda i: (i, 0, 0))],
        out_shape=[jax.ShapeDtypeStruct((n, A_WIDTH), BF16),
                   jax.ShapeDtypeStruct((n // tm, hp, tm), BF16),
                   jax.ShapeDtypeStruct((n, hp), BF16),
                   jax.ShapeDtypeStruct((n // tm, B_HEADS * B_VDIM, tm), BF16)],
        compiler_params=pltpu.CompilerParams(dimension_semantics=("parallel",),
                                             vmem_limit_bytes=VMEM_LIMIT),
        name="even_pre",
    )(x, pos_row, inv_col, *params)


def _attn_kernel(qt_ref, k_ref, vt_ref, o_ref, sa_ref, sb_ref):
    t = qt_ref.shape[2]
    ones = jnp.ones((ATTN_SUM_ROWS, t), BF16)

    def one_tile(qi, _):

        def scores(j, dst):
            start = pl.multiple_of(j * t, t)
            for hh in range(ATTN_HEADS):
                lo = hh * HEAD_PAD
                dst[hh] = _dot(k_ref[pl.ds(start, t), lo:lo + HEAD_PAD], qt_ref[0, lo:lo + HEAD_PAD, :])

        def update(j, src, stats, masked):
            vt = vt_ref[j]
            new = []
            for hh in range(ATTN_HEADS):
                m, acc = stats[hh]
                s = src[hh]
                if masked:
                    key_ids = lax.broadcasted_iota(jnp.int32, (t, t), 0)
                    qry_ids = lax.broadcasted_iota(jnp.int32, (t, t), 1)
                    s = jnp.where(key_ids <= qry_ids, s, -jnp.inf)
                m_new = jnp.maximum(m, jnp.max(s, axis=0, keepdims=True))
                p = jnp.exp2(s - m_new).astype(BF16)
                a = jnp.exp2(m - m_new)
                v_ext = jnp.concatenate([vt[hh * B_VDIM:(hh + 1) * B_VDIM, :], ones], axis=0)
                acc = a * acc + _dot(v_ext, p)
                new.append((m_new, acc))
            return tuple(new)

        def two_blocks(i, stats):
            j = 2 * i
            scores(j + 1, sb_ref)
            stats = update(j, sa_ref, stats, False)
            scores(j + 2, sa_ref)
            return update(j + 1, sb_ref, stats, False)

        def four_blocks(i, stats):
            return two_blocks(2 * i + 1, two_blocks(2 * i, stats))

        def tail_odd(stats):
            scores(qi, sb_ref)
            stats = update(qi - 1, sa_ref, stats, False)
            return update(qi, sb_ref, stats, True)

        def tail_even(stats):
            return update(qi, sa_ref, stats, True)

        scores(0, sa_ref)
        init1 = (jnp.full((1, t), -jnp.inf, F32), jnp.zeros((B_VDIM + ATTN_SUM_ROWS, t), F32))
        def eight_blocks(i, stats):
            return four_blocks(2 * i + 1, four_blocks(2 * i, stats))

        stats = lax.fori_loop(0, qi // 8, eight_blocks, (init1,) * ATTN_HEADS)
        stats = lax.fori_loop(2 * (qi // 8), qi // 4, four_blocks, stats)
        stats = lax.fori_loop(2 * (qi // 4), qi // 2, two_blocks, stats)
        stats = lax.cond(qi % 2 == 1, tail_odd, tail_even, stats)
        out_t = jnp.concatenate([acc[:B_VDIM] / acc[B_VDIM:B_VDIM + 1] for (_, acc) in stats], axis=0)
        o_ref[...] = out_t.T.astype(o_ref.dtype)

    one_tile(pl.program_id(2), None)


def _attn(qt, k, vt, *, batch, seq):
    t = qt.shape[2]
    n = k.shape[0]
    nq = seq // t
    return pl.pallas_call(
        _attn_kernel,
        grid=(batch, B_HEADS // ATTN_HEADS, nq),
        in_specs=[pl.BlockSpec((1, ATTN_HEADS * HEAD_PAD, t), lambda b, h, i: (b * nq + i, h, 0)),
                  pl.BlockSpec((seq, ATTN_HEADS * HEAD_PAD), lambda b, h, i: (b, h)),
                  pl.BlockSpec((nq, ATTN_HEADS * B_VDIM, t), lambda b, h, i: (b, h, 0))],
        out_specs=pl.BlockSpec((t, ATTN_HEADS * B_VDIM), lambda b, h, i: (b * nq + i, h)),
        out_shape=jax.ShapeDtypeStruct((n, B_HEADS * B_VDIM), BF16),
        scratch_shapes=[pltpu.VMEM((ATTN_HEADS, t, t), F32), pltpu.VMEM((ATTN_HEADS, t, t), F32)],
        compiler_params=pltpu.CompilerParams(dimension_semantics=("parallel", "parallel", "arbitrary"),
                                             vmem_limit_bytes=VMEM_LIMIT),
        name="attn",
    )(qt, k, vt)


def _odd_pre_kernel(x_ref, win_ref, wzg_ref, wgate_ref, bgate_ref,
                    q_ref, k_ref, v_ref, sg_ref, la_ref):
    hk, hv = C_HEADS * C_DK, C_HEADS * C_DV
    tm = x_ref.shape[0]

    def in_proj(rows):
        xb = x_ref[rows, :].astype(BF16)
        return _dot(xb, win_ref[...]), _dot(xb, wzg_ref[...])

    def finish(rows, z, zg):
        q_ref[rows, :] = z[:, :hk] * (C_DK ** -0.5)
        k_ref[rows, :] = z[:, hk:2 * hk]
        v_ref[rows, :] = z[:, 2 * hk:2 * hk + hv].astype(v_ref.dtype)
        g = z[:, 2 * hk + hv:]
        sg_ref[rows, :] = g * jax.nn.sigmoid(g)
        logits = _dot(zg.astype(BF16), wgate_ref[...]) + bgate_ref[...]
        la_ref[rows, :] = -(jnp.maximum(-logits, 0.0) + jnp.log1p(jnp.exp(-jnp.abs(logits)))) * (1.0 / C_GATE_TAU)

    blocks = [slice(r, r + ODD_ROW_BLOCK) for r in range(0, tm, ODD_ROW_BLOCK)]
    cur = in_proj(blocks[0])
    for r, rows in enumerate(blocks):
        nxt = in_proj(blocks[r + 1]) if r + 1 < len(blocks) else None
        finish(rows, *cur)
        cur = nxt


def _gla_level_matrix():
    c = GLA_CHUNK
    t = jnp.arange(c)[:, None]
    u = jnp.arange(c)[None, :]
    mats = []
    for m in GLA_FINE_LEVELS:
        r = t % (2 * m)
        p = t - r + m - 1
        upper = (r >= m) & (u > p) & (u <= t)
        lower = (r < m) & (u > t) & (u <= p)
        mats.append(upper | lower)
    return jnp.concatenate(mats, axis=0).astype(BF16)


def _split_bf16(x, pieces):
    out = []
    for _ in range(pieces - 1):
        hi = x.astype(BF16)
        out.append(hi)
        x = x - hi.astype(F32)
    out.append(x.astype(BF16))
    return out


def _gla_kernel(q_ref, k_ref, v_ref, la_ref, sg_ref, lng_ref, lnb_ref, tri_ref, lvl_ref,
                y_ref, st_ref, a_scr):
    c = GLA_CHUNK
    hk = C_HEADS * C_DK
    n_chunks = q_ref.shape[0] // c

    @pl.when(pl.program_id(1) == 0)
    def _():
        st_ref[...] = jnp.zeros_like(st_ref)

    row = lax.broadcasted_iota(jnp.int32, (c, 1), 0)
    pair = lax.broadcasted_iota(jnp.int32, (c, c), 0) ^ lax.broadcasted_iota(jnp.int32, (c, c), 1)

    def intra(ci):
        rows = slice(ci * c, (ci + 1) * c)
        q = q_ref[rows, :]
        k = k_ref[rows, :]
        g = _split_bf16(la_ref[rows, :], 3)
        tri = tri_ref[...]
        b = _dot(tri, g[0]) + _dot(tri, g[1]) + _dot(tri, g[2])
        fine = _dot(lvl_ref[...], g[0]) + _dot(lvl_ref[...], g[1])

        qb = q.astype(BF16)
        kb = k.astype(BF16)
        for h in range(C_HEADS):
            hs = slice(h * C_DK, (h + 1) * C_DK)
            a_scr[ci, h] = jnp.where(pair == 0, _dot_nt(qb[:, hs], kb[:, hs]), 0.0)

        for m in GLA_BLOCK_LEVELS:
            for base in range(0, c, 2 * m):
                lower, upper = slice(base, base + m), slice(base + m, base + 2 * m)
                b_ref_row = b[base + m - 1:base + m, :]
                qe = (q[upper, :] * jnp.exp(b[upper, :] - b_ref_row)).astype(BF16)
                ke = (k[lower, :] * jnp.exp(b_ref_row - b[lower, :])).astype(BF16)
                for h in range(C_HEADS):
                    hs = slice(h * C_DK, (h + 1) * C_DK)
                    a_scr[ci, h, upper, lower] = _dot_nt(qe[:, hs], ke[:, hs])

        m = 1
        while m < min(GLA_BLOCK_LEVELS):
            if m in GLA_FINE_LEVELS:
                i = GLA_FINE_LEVELS.index(m)
                d = fine[i * c:(i + 1) * c, :]
            else:
                b3 = b.reshape(c // (2 * m), 2 * m, hk)
                d3 = b3 - b3[:, m - 1:m, :]
                sub = lax.broadcasted_iota(jnp.int32, (1, 2 * m, 1), 1)
                d = jnp.where(sub >= m, d3, -d3).reshape(c, hk)
            e = jnp.exp(d)
            upper = (row & m) != 0
            qe = jnp.where(upper, q * e, 0.0).astype(BF16)
            ke = jnp.where(upper, 0.0, k * e).astype(BF16)
            level = (pair >= m) & (pair < 2 * m)
            for h in range(C_HEADS):
                hs = slice(h * C_DK, (h + 1) * C_DK)
                a_scr[ci, h] = jnp.where(level, _dot_nt(qe[:, hs], ke[:, hs]), a_scr[ci, h])
            m *= 2

        b_last = b[c - 1:c, :]
        qs = (q * jnp.exp(b)).astype(BF16)
        kd = (k * jnp.exp(b_last - b)).astype(BF16)
        return qs, kd, jnp.exp(b_last)

    def carry_state(ci, qs, kd, decay):
        rows = slice(ci * c, (ci + 1) * c)
        for h in range(C_HEADS):
            hs = slice(h * C_DK, (h + 1) * C_DK)
            vs = slice(h * C_DV, (h + 1) * C_DV)
            v = v_ref[rows, vs]
            st = st_ref[h]
            o = _dot(a_scr[ci, h].astype(BF16), v) + _dot_nt(qs[:, hs], st.astype(BF16))
            st_ref[h] = st * decay[:, hs] + _dot_tn(v, kd[:, hs])
            on = _ln(o, lng_ref[...], lnb_ref[...])
            y_ref[rows, vs] = (on * sg_ref[rows, vs]).astype(y_ref.dtype)

    parts = [intra(ci) for ci in range(n_chunks)]
    for ci in range(n_chunks):
        carry_state(ci, *parts[ci])


def _odd_mixer_kernel(x_ref, win_ref, wzg_ref, wgate_ref, bgate_ref, lng_ref, lnb_ref, tri_ref, lvl_ref,
                      y_ref, st_ref, a_scr, q_scr, k_scr, v_scr, sg_scr, la_scr):
    _odd_pre_kernel(x_ref, win_ref, wzg_ref, wgate_ref, bgate_ref, q_scr, k_scr, v_scr, sg_scr, la_scr)
    _gla_kernel(q_scr, k_scr, v_scr, la_scr, sg_scr, lng_ref, lnb_ref, tri_ref, lvl_ref, y_ref, st_ref, a_scr)


def _odd_mixer(x, win, wzg, wgate, bgate, lng, lnb, *, batch, seq, layer):
    n = x.shape[0]
    c = GLA_CHUNK
    tm = TILES["gla"]
    nt = seq // tm
    hk, hv = C_HEADS * C_DK, C_HEADS * C_DV
    tri = jnp.tril(jnp.ones((c, c), F32)).astype(BF16)
    lvl = _gla_level_matrix()
    full2 = lambda a: pl.BlockSpec(a.shape, lambda b, i: (0, 0))
    return pl.pallas_call(
        _odd_mixer_kernel,
        grid=(batch, nt),
        in_specs=[pl.BlockSpec((tm, D_MODEL), lambda b, i: (b * nt + i, 0)),
                  _layer_spec(win, layer, cols=2 * hk + 2 * hv), _layer_spec(wzg, layer),
                  _layer_spec(wgate, layer), _layer_spec(bgate, layer),
                  _layer_spec(lng, layer), _layer_spec(lnb, layer), full2(tri), full2(lvl)],
        out_specs=pl.BlockSpec((tm, hv), lambda b, i: (b * nt + i, 0)),
        out_shape=jax.ShapeDtypeStruct((n, hv), BF16),
        scratch_shapes=[pltpu.VMEM((C_HEADS, C_DV, C_DK), F32), pltpu.VMEM((tm // c, C_HEADS, c, c), F32),
                        pltpu.VMEM((tm, hk), F32), pltpu.VMEM((tm, hk), F32), pltpu.VMEM((tm, hv), BF16),
                        pltpu.VMEM((tm, hv), F32), pltpu.VMEM((tm, hk), F32)],
        compiler_params=pltpu.CompilerParams(dimension_semantics=("parallel", "arbitrary"),
                                             vmem_limit_bytes=VMEM_LIMIT),
        name="odd_mixer",
    )(x, win, wzg, wgate, bgate, lng, lnb, tri, lvl)


def _post_ffn_kernel(x_ref, y1_ref, y2_ref, wo_ref, g1_ref, b1_ref, w1_ref, w2_ref, g2_ref, b2_ref,
                     o_ref, x1_ref, x1b_ref, acc_ref):
    i = pl.program_id(0)
    j = pl.program_id(1)
    nt = pl.num_programs(0) - 1
    half = y1_ref.shape[1]
    tm = x_ref.shape[0]
    blocks = [slice(r, r + FFN_ROW_BLOCK) for r in range(0, tm, FFN_ROW_BLOCK)]

    def mlp(x1b):
        h = _dot(x1b, w1_ref[...])
        return _dot(jnp.square(jnp.maximum(h, 0.0)).astype(BF16), w2_ref[...])

    def close_tile(rows):
        o_ref[rows, :] = _ln(ALPHA * x1_ref[rows, :] + acc_ref[rows, :], g2_ref[...], b2_ref[...])

    def out_proj(rows):
        return _dot(y1_ref[rows, :], wo_ref[0:half, :]) + _dot(y2_ref[rows, :], wo_ref[half:2 * half, :])

    def open_tile(rows, y):
        x1 = _ln(ALPHA * x_ref[rows, :] + y, g1_ref[...], b1_ref[...])
        x1_ref[rows, :] = x1
        x1b = x1.astype(BF16)
        x1b_ref[rows, :] = x1b
        acc_ref[rows, :] = mlp(x1b)

    def open_all(close_first):
        y = out_proj(blocks[0])
        for r, rows in enumerate(blocks):
            y_next = out_proj(blocks[r + 1]) if r + 1 < len(blocks) else None
            if close_first:
                close_tile(rows)
            open_tile(rows, y)
            y = y_next

    @pl.when((j == 0) & (i == 0))
    def _():
        open_all(False)

    @pl.when((j == 0) & (i > 0) & (i < nt))
    def _():
        open_all(True)

    @pl.when((j == 0) & (i == nt))
    def _():
        for rows in blocks:
            close_tile(rows)

    @pl.when((j > 0) & (i < nt))
    def _():
        acc_ref[...] += mlp(x1b_ref[...])


def _post_ffn(x, y1, y2, y2_col, wo, g1, b1, w1, w2, g2, b2, *, layer):
    n = x.shape[0]
    tm, tf = TILES["ffn_rows"], TILES["ffn_cols"]
    nt = n // tm
    half = D_MODEL // 2
    opened = lambda i: jnp.minimum(i, nt - 1)
    vec = pl.BlockSpec((None, 1, D_MODEL), lambda i, j: (layer, 0, 0))
    return pl.pallas_call(
        _post_ffn_kernel,
        grid=(nt + 1, D_FF // tf),
        in_specs=[pl.BlockSpec((tm, D_MODEL), lambda i, j: (opened(i), 0)),
                  pl.BlockSpec((tm, half), lambda i, j: (opened(i), 0)),
                  pl.BlockSpec((tm, half), lambda i, j: (opened(i), y2_col)),
                  pl.BlockSpec((None, D_MODEL, D_MODEL), lambda i, j: (layer, 0, 0)),
                  vec, vec,
                  pl.BlockSpec((None, D_MODEL, tf), lambda i, j: (layer, 0, j)),
                  pl.BlockSpec((None, tf, D_MODEL), lambda i, j: (layer, j, 0)),
                  vec, vec],
        out_specs=pl.BlockSpec((tm, D_MODEL), lambda i, j: (jnp.maximum(i - 1, 0), 0)),
        out_shape=jax.ShapeDtypeStruct((n, D_MODEL), F32),
        scratch_shapes=[pltpu.VMEM((tm, D_MODEL), F32), pltpu.VMEM((tm, D_MODEL), BF16),
                        pltpu.VMEM((tm, D_MODEL), F32)],
        compiler_params=pltpu.CompilerParams(dimension_semantics=("arbitrary", "arbitrary"),
                                             vmem_limit_bytes=VMEM_LIMIT),
        name="post_ffn",
    )(x, y1, y2, wo, g1, b1, w1, w2, g2, b2)


def kernel(x, positions, ln1_g, ln1_b, ln2_g, ln2_b, w_in_even, a_w_s, a_b_s, a_ln_g, a_ln_b, b_q_norm, b_kv_norm, b_w_uq, b_w_ukv, w_out_even, w_in_odd, c_w_gate, c_b_gate, c_ln_g, c_ln_b, w_out_odd, w_ff1, w_ff2):
    bn, s, d = x.shape
    n = bn * s
    xf = x.reshape(n, d)
    half = B_ROPE // 2
    inv = ROPE_THETA ** (-jnp.arange(half, dtype=F32) / half)
    pos_row, inv_col = positions.reshape(1, n).astype(jnp.int32), inv.reshape(half, 1)
    n_even = w_in_even.shape[0]
    vec = lambda a: a.reshape(a.shape[0], 1, -1)

    win_even = jnp.pad(w_in_even, ((0, 0), (0, 0), (0, HEAD_PAD - B_ROPE))).astype(BF16)
    wuq = b_w_uq.reshape(n_even, B_Q_RANK, B_HEADS, B_NOPE + B_ROPE)
    wuq = jnp.pad(wuq, ((0, 0), (0, 0), (0, 0), (0, HEAD_PAD - B_NOPE - B_ROPE)))
    wuqt = wuq.reshape(n_even, B_Q_RANK, B_HEADS * HEAD_PAD).transpose(0, 2, 1).astype(BF16)
    wukv = b_w_ukv.reshape(n_even, B_KV_RANK, B_HEADS, B_NOPE + B_VDIM)
    wuk = jnp.pad(wukv[..., :B_NOPE], ((0, 0), (0, 0), (0, 0), (0, HEAD_PAD - B_NOPE)))
    wuk = wuk.reshape(n_even, B_KV_RANK, B_HEADS * HEAD_PAD).astype(BF16)
    wuvvt = wukv[..., B_NOPE:].reshape(n_even, B_KV_RANK, B_HEADS * B_VDIM).transpose(0, 2, 1).astype(BF16)
    causal = jnp.tril(jnp.ones((A_CHUNK, A_CHUNK), dtype=bool))
    ws = jnp.where(causal, a_w_s, 0.0).astype(BF16)
    bs = jnp.broadcast_to(a_b_s[..., None], a_b_s.shape + (A_GROUP_DIM,))
    even_params = (win_even, ws, bs, a_ln_g, a_ln_b, vec(b_q_norm), vec(b_kv_norm), wuqt, wuk, wuvvt)

    hk, hv = C_HEADS * C_DK, C_HEADS * C_DV
    win_odd = w_in_odd.astype(BF16)
    wzg = jnp.pad(w_in_odd[:, :, 2 * hk + 2 * hv:], ((0, 0), (0, 0), (0, 128 - C_GATE_RANK))).astype(BF16)
    wgate = jnp.pad(c_w_gate, ((0, 0), (0, 128 - C_GATE_RANK), (0, 0))).astype(BF16)
    odd_params = (win_odd, wzg, wgate, vec(c_b_gate))
    c_lng, c_lnb = vec(c_ln_g), vec(c_ln_b)

    w_out = jnp.stack([(w_out_even if layer % 2 == 0 else w_out_odd)[layer // 2]
                       for layer in range(DEPTH)]).astype(BF16)
    w1, w2 = w_ff1.astype(BF16), w_ff2.astype(BF16)
    g1, b1, g2, b2 = vec(ln1_g), vec(ln1_b), vec(ln2_g), vec(ln2_b)

    for layer in range(DEPTH):
        j = layer // 2
        if layer % 2 == 0:
            ya, qt, k, vt = _even_pre(xf, pos_row, inv_col, *even_params, layer=j)
            yb = _attn(qt, k, vt, batch=bn, seq=s)
            y1, y2, y2_col = ya, yb, 0
        else:
            y = _odd_mixer(xf, *odd_params, c_lng, c_lnb, batch=bn, seq=s, layer=j)
            y1, y2, y2_col = y, y, 1
        xf = _post_ffn(xf, y1, y2, y2_col, w_out, g1, b1, w1, w2, g2, b2, layer=layer)
    return xf.reshape(bn, s, d)
```
